```python
import math
import numpy as np
import jax
import jax.numpy as jnp
from jax import lax

D_MODEL = 4096
BATCH = 4
SEQ = 2048
DEPTH = 2

GRID_W = 64
CTX_LEN = 256
N_MOD = 6
D_MIX = D_MODEL
GROUP_W = D_MIX // 4

S5_WIDTH = GROUP_W
S5_GROUP_CH = 16
S5_GROUPS = S5_WIDTH // S5_GROUP_CH
S5_STATE = 64

MLA_HEADS = 8
MLA_NOPE = 128
MLA_ROPE = 64
MLA_V = GROUP_W // MLA_HEADS
MLA_Q_RANK = 512
MLA_KV_RANK = 256

RW_HEAD = 64
RW_WIDTH = GROUP_W
RW_HEADS = RW_WIDTH // RW_HEAD
RW_W_RANK = 64
RW_A_RANK = 64
RW_G_RANK = 128
RW_GN_EPS = 64e-5
RW_SPLITS = (RW_WIDTH, RW_WIDTH, RW_WIDTH, RW_W_RANK, RW_A_RANK, RW_G_RANK)
RW_IN = 3 * RW_WIDTH + RW_W_RANK + RW_A_RANK + RW_G_RANK

DIFF_HEAD = 64
DIFF_HEADS = GROUP_W // (2 * DIFF_HEAD)
DIFF_W = DIFF_HEADS * 2 * DIFF_HEAD

IN_SPLITS = (S5_WIDTH, MLA_Q_RANK, MLA_KV_RANK, MLA_ROPE, RW_IN, DIFF_W, DIFF_W, DIFF_W)
IN_WIDTH = S5_WIDTH + MLA_Q_RANK + MLA_KV_RANK + MLA_ROPE + RW_IN + 3 * DIFF_W

MOE_GROUPS = 4
MOE_PER_GROUP = 8
MOE_EXPERTS = MOE_GROUPS * MOE_PER_GROUP
MOE_TOP_K = 2
MOE_HIDDEN = 512
MOE_BLOCK = 128

Q_BLOCK = 128
ROPE_BASE = 10000.0
NORM_EPS = 1e-6

kernel_name = 'hybrid_head_group_flow_block'


def rms_norm(x, g):
    xf = x.astype(jnp.float32)
    y = xf * lax.rsqrt(jnp.mean(xf * xf, axis=-1, keepdims=True) + NORM_EPS)
    return (y * g.astype(jnp.float32)).astype(x.dtype)


def modulate(h, shift, scale):
    return h * (1 + scale) + shift


def split_cols(z, widths):
    return jnp.split(z, np.cumsum(widths)[:-1].tolist(), axis=-1)


def axial_rope(rows, dim):
    n_freq = dim // 4
    inv = ROPE_BASE ** (-jnp.arange(n_freq, dtype=jnp.float32) / n_freq)
    row = jnp.repeat(jnp.arange(rows, dtype=jnp.float32), GRID_W)
    col = jnp.tile(jnp.arange(GRID_W, dtype=jnp.float32), rows)
    ang = jnp.concatenate([row[:, None] * inv, col[:, None] * inv], axis=-1)
    return jnp.cos(ang), jnp.sin(ang)


def apply_rope(x, cos, sin):
    half = x.shape[-1] // 2
    x1, x2 = x[..., :half], x[..., half:]
    cos, sin = cos.astype(x.dtype), sin.astype(x.dtype)
    return jnp.concatenate([x1 * cos - x2 * sin, x1 * sin + x2 * cos], axis=-1)


def query_blocks(fn, *qs):
    b, n = qs[0].shape[:2]
    nb = n // Q_BLOCK
    blocked = tuple(jnp.swapaxes(q.reshape(b, nb, Q_BLOCK, *q.shape[2:]), 0, 1) for q in qs)
    out = lax.map(lambda args: fn(*args), blocked)
    return jnp.swapaxes(out, 0, 1).reshape(b, n, *out.shape[3:])


def centred_shift(z, mu_prev, mu_next):
    z_prev = jnp.pad(z, ((0, 0), (1, 0), (0, 0)))[:, :-1]
    z_next = jnp.pad(z, ((0, 0), (0, 1), (0, 0)))[:, 1:]
    return z + mu_prev * (z_prev - z) + mu_next * (z_next - z)


def s5_scan(lam_bar, bu, h0, reverse):
    if reverse:
        bu = jnp.flip(bu, 1)
    bu = bu.at[:, 0].add(lam_bar * h0)
    a = jnp.broadcast_to(lam_bar, bu.shape)

    def combine(e1, e2):
        a1, b1 = e1
        a2, b2 = e2
        return a1 * a2, a2 * b1 + b2

    _, h = lax.associative_scan(combine, (a, bu), axis=1)
    return jnp.flip(h, 1) if reverse else h


def s5_mixer(u_c, u_l, a_re, a_im, log_step, b_re, b_im, c_re, c_im, d_skip, w_glu):
    f32 = jnp.float32
    dtype = u_l.dtype
    lam = lax.complex(a_re.astype(f32), a_im.astype(f32))
    lam_bar = jnp.exp(lam * jnp.exp(log_step.astype(f32))[..., None])
    b_mat = lax.complex(b_re.astype(f32), b_im.astype(f32))
    b_bar = ((lam_bar - 1.0) / lam)[..., None] * b_mat
    c_mat = lax.complex(c_re.astype(f32), c_im.astype(f32))
    d = d_skip.astype(f32).reshape(S5_GROUPS, S5_GROUP_CH)

    def grouped(u):
        return u.astype(f32).reshape(u.shape[0], u.shape[1], S5_GROUPS, S5_GROUP_CH)

    def drive(u, dr):
        return jnp.einsum('bsgh,gph->bsgp', u.astype(jnp.complex64), b_bar[dr])

    def readout(h, dr):
        return jnp.einsum('bsgp,ghp->bsgh', h, c_mat[dr]).real

    def glu(y):
        b, n = y.shape[:2]
        y = jax.nn.gelu(y.reshape(b, n, S5_WIDTH)).astype(dtype)
        gv = y @ w_glu
        return gv[..., :S5_WIDTH] * jax.nn.sigmoid(gv[..., S5_WIDTH:])

    uc, ul = grouped(u_c), grouped(u_l)
    zero = jnp.zeros((u_c.shape[0], S5_GROUPS, S5_STATE), jnp.complex64)
    hc_f = s5_scan(lam_bar[0], drive(uc, 0), zero, False)
    hc_b = s5_scan(lam_bar[1], drive(uc, 1), zero, True)
    hl_f = s5_scan(lam_bar[0], drive(ul, 0), hc_f[:, -1], False)
    hl_b = s5_scan(lam_bar[1], drive(ul, 1), hc_b[:, 0], True)
    y_c = glu(readout(hc_f, 0) + readout(hc_b, 1) + d * uc)
    y_l = glu(readout(hl_f, 0) + readout(hl_b, 1) + d * ul)
    return y_c, y_l


def mla_project(cq, ckv, krope, q_a_norm, kv_a_norm, w_uq, w_ukv, q_norm, k_norm):
    b, n, _ = cq.shape
    q = (rms_norm(cq, q_a_norm) @ w_uq).reshape(b, n, MLA_HEADS, MLA_NOPE + MLA_ROPE)
    kv = (rms_norm(ckv, kv_a_norm) @ w_ukv).reshape(b, n, MLA_HEADS, MLA_NOPE + MLA_V)
    q_nope = rms_norm(q[..., :MLA_NOPE], q_norm[:MLA_NOPE])
    q_rope = rms_norm(q[..., MLA_NOPE:], q_norm[MLA_NOPE:])
    k_nope = rms_norm(kv[..., :MLA_NOPE], k_norm[:MLA_NOPE])
    k_rope = rms_norm(krope, k_norm[MLA_NOPE:])
    return q_nope, q_rope, k_nope, k_rope, kv[..., MLA_NOPE:]


def mla_attend(q_nope, q_rope, k_nope, k_rope, v):
    s = (jnp.einsum('bqhd,bkhd->bhqk', q_nope, k_nope)
         + jnp.einsum('bqhr,bkr->bhqk', q_rope, k_rope)).astype(jnp.float32)
    p = jax.nn.softmax(s * (MLA_NOPE + MLA_ROPE) ** -0.5, axis=-1)
    return jnp.einsum('bhqk,bkhd->bqhd', p.astype(v.dtype), v)


def mla_mixer(cq_c, ckv_c, krope_c, cq_l, ckv_l, krope_l, q_a_norm, kv_a_norm, w_uq, w_ukv,
              q_norm, k_norm, cos, sin):
    params = (q_a_norm, kv_a_norm, w_uq, w_ukv, q_norm, k_norm)
    qn_c, qr_c, kn_c, kr_c, v_c = mla_project(cq_c, ckv_c, krope_c, *params)
    qn_l, qr_l, kn_l, kr_l, v_l = mla_project(cq_l, ckv_l, krope_l, *params)
    qr_l = apply_rope(qr_l, cos[:, None], sin[:, None])
    kr_l = apply_rope(kr_l, cos, sin)
    kn = jnp.concatenate([kn_c, kn_l], axis=1)
    kr = jnp.concatenate([kr_c, kr_l], axis=1)
    v = jnp.concatenate([v_c, v_l], axis=1)
    y_c = mla_attend(qn_c, qr_c, kn_c, kr_c, v_c)
    y_l = query_blocks(lambda qn, qr: mla_attend(qn, qr, kn, kr, v), qn_l, qr_l)
    b = y_l.shape[0]
    return y_c.reshape(b, y_c.shape[1], GROUP_W), y_l.reshape(b, y_l.shape[1], GROUP_W)


def rwkv_prepare(z, mu, w0, w2, a0, a2, g2, k_k, k_a):
    b, n, _ = z.shape
    f32 = jnp.float32

    def heads(t):
        return t.astype(f32).reshape(b, n, RW_HEADS, RW_HEAD)

    z = centred_shift(z, mu[0], mu[1])
    r, k, v, wl, al, gl = split_cols(z, RW_SPLITS)
    kk = heads(k * k_k)
    kk = kk * lax.rsqrt(jnp.maximum(jnp.sum(kk * kk, axis=-1, keepdims=True), 1e-12))
    g = jax.nn.sigmoid(gl) @ g2
    ka = k_a.astype(f32).reshape(RW_HEADS, RW_HEAD)
    dirs = []
    for dr in range(2):
        w = -jax.nn.softplus(-heads(w0[dr] + jnp.tanh(wl) @ w2[dr])) - 0.5
        a = jax.nn.sigmoid(heads(a0[dr] + al @ a2[dr]))
        dirs.append((jnp.exp(-jnp.exp(w)), heads(k) * (1 + (a - 1) * ka), a))
    return heads(r), heads(v), kk, g, dirs


def rwkv_scan(s0, r, decay, k, v, kk, a, reverse):
    def step(s, inp):
        r_t, w_t, k_t, v_t, kk_t, a_t = inp
        sa = jnp.einsum('bhvk,bhk->bhv', s, kk_t)
        s = (s * w_t[:, :, None, :] - sa[..., None] * (kk_t * a_t)[:, :, None, :]
             + v_t[..., None] * k_t[:, :, None, :])
        return s, jnp.einsum('bhvk,bhk->bhv', s, r_t)

    xs = tuple(jnp.moveaxis(t, 1, 0) for t in (r, decay, k, v, kk, a))
    s_last, y = lax.scan(step, s0, xs, reverse=reverse)
    return s_last, jnp.moveaxis(y, 0, 1)


def rwkv_mixer(z_c, z_l, mu, w0, w2, a0, a2, g2, k_k, k_a, r_k, ln_g, ln_b):
    dtype = z_l.dtype
    f32 = jnp.float32
    p_c = rwkv_prepare(z_c, mu, w0, w2, a0, a2, g2, k_k, k_a)
    p_l = rwkv_prepare(z_l, mu, w0, w2, a0, a2, g2, k_k, k_a)
    s0 = jnp.zeros((z_c.shape[0], RW_HEADS, RW_HEAD, RW_HEAD), f32)

    def run(p, init, dr):
        r, v, kk, _, dirs = p
        decay, k, a = dirs[dr]
        return rwkv_scan(init, r, decay, k, v, kk, a, dr == 1)

    sc_f, oc_f = run(p_c, s0, 0)
    sc_b, oc_b = run(p_c, s0, 1)
    _, ol_f = run(p_l, sc_f, 0)
    _, ol_b = run(p_l, sc_b, 1)
    rk = r_k.astype(f32)

    def finish(p, o_f, o_b):
        r, v, _, g, dirs = p
        b, n = r.shape[:2]
        bonus = (jnp.sum(r * dirs[0][1] * rk, axis=-1, keepdims=True)
                 + jnp.sum(r * dirs[1][1] * rk, axis=-1, keepdims=True)) * v
        o = o_f + o_b + bonus
        mean = jnp.mean(o, axis=-1, keepdims=True)
        var = jnp.mean(jnp.square(o - mean), axis=-1, keepdims=True)
        o = ((o - mean) * lax.rsqrt(var + RW_GN_EPS)).reshape(b, n, RW_WIDTH)
        o = o * ln_g.astype(f32) + ln_b.astype(f32)
        return (o * g.astype(f32)).astype(dtype)

    return finish(p_c, oc_f, oc_b), finish(p_l, ol_f, ol_b)


def diff_attend(q, k, v, lam):
    s = jnp.einsum('bqhcd,bkhcd->bhcqk', q, k).astype(jnp.float32) * DIFF_HEAD ** -0.5
    p = jax.nn.softmax(s, axis=-1)
    attn = p[:, :, 0] - lam * p[:, :, 1]
    return jnp.einsum('bhqk,bkhe->bqhe', attn.astype(v.dtype), v)


def diff_mixer(q_c, k_c, v_c, q_l, k_l, v_l, q_norm, k_norm, lam_vecs, sub_norm, lam_init, cos, sin):
    def heads(q, k, v):
        b, n, _ = q.shape
        q = rms_norm(q.reshape(b, n, DIFF_HEADS, 2, DIFF_HEAD), q_norm)
        k = rms_norm(k.reshape(b, n, DIFF_HEADS, 2, DIFF_HEAD), k_norm)
        return q, k, v.reshape(b, n, DIFF_HEADS, 2 * DIFF_HEAD)

    qc, kc, vc = heads(q_c, k_c, v_c)
    ql, kl, vl = heads(q_l, k_l, v_l)
    ql = apply_rope(ql, cos[:, None, None], sin[:, None, None])
    kl = apply_rope(kl, cos[:, None, None], sin[:, None, None])
    lv = lam_vecs.astype(jnp.float32)
    lam = jnp.exp(jnp.sum(lv[0] * lv[1])) - jnp.exp(jnp.sum(lv[2] * lv[3])) + lam_init
    k_all = jnp.concatenate([kc, kl], axis=1)
    v_all = jnp.concatenate([vc, vl], axis=1)
    y_c = diff_attend(qc, kc, vc, lam)
    y_l = query_blocks(lambda qb: diff_attend(qb, k_all, v_all, lam), ql)

    def post(y):
        b, n = y.shape[:2]
        return (rms_norm(y, sub_norm) * (1.0 - lam_init)).reshape(b, n, GROUP_W)

    return post(y_c), post(y_l)


def expert_ffn(h, expert, weight, w_gate, w_up, w_down):
    t, d = h.shape
    n_exp = w_gate.shape[0]
    flat_e = expert.reshape(-1)
    n_assign = flat_e.shape[0]
    order = jnp.argsort(flat_e)
    sorted_e = flat_e[order]
    counts = jnp.bincount(flat_e, length=n_exp)
    padded = (counts + MOE_BLOCK - 1) // MOE_BLOCK * MOE_BLOCK
    pad_end = jnp.cumsum(padded)
    pad_start = pad_end - padded
    seg_start = jnp.cumsum(counts) - counts
    dest = pad_start[sorted_e] + jnp.arange(n_assign) - seg_start[sorted_e]
    n_blocks = -(-(n_assign + n_exp * (MOE_BLOCK - 1)) // MOE_BLOCK)
    cap = n_blocks * MOE_BLOCK
    slot_tok = jnp.full((cap,), t, jnp.int32).at[dest].set((order // MOE_TOP_K).astype(jnp.int32))
    slot_w = jnp.zeros((cap,), jnp.float32).at[dest].set(weight.reshape(-1)[order])
    block_exp = jnp.minimum(jnp.searchsorted(pad_end, jnp.arange(n_blocks) * MOE_BLOCK, side='right'),
                            n_exp - 1)
    h_pad = jnp.concatenate([h, jnp.zeros((1, d), h.dtype)], axis=0)
    xb = h_pad[slot_tok].reshape(n_blocks, MOE_BLOCK, d)

    def expert_block(args):
        xs, e = args
        return (jax.nn.silu(xs @ w_gate[e]) * (xs @ w_up[e])) @ w_down[e]

    yb = lax.map(expert_block, (xb, block_exp)).reshape(cap, d)
    y = jnp.zeros((t + 1, d), h.dtype).at[slot_tok].add(yb * slot_w[:, None].astype(h.dtype))
    return y[:t]


def hier_moe(h, wr_coarse, br_coarse, wr_fine, br_fine, w_gate, w_up, w_down):
    t = h.shape[0]
    p_group = jax.nn.softmax((h @ wr_coarse + br_coarse).astype(jnp.float32), axis=-1)
    group = jnp.argmax(p_group, axis=-1).astype(jnp.int32)
    p_sel = jnp.take_along_axis(p_group, group[:, None], axis=-1)
    fine = (h @ wr_fine + br_fine).astype(jnp.float32).reshape(t, MOE_GROUPS, MOE_PER_GROUP)
    fine = jnp.take_along_axis(fine, group[:, None, None], axis=1)[:, 0]
    top_v, top_i = lax.top_k(fine, MOE_TOP_K)
    weight = p_sel * jax.nn.softmax(top_v, axis=-1)
    expert = group[:, None] * MOE_PER_GROUP + top_i.astype(jnp.int32)
    return expert_ffn(h, expert, weight, w_gate, w_up, w_down)


def setup_inputs(seed: int = 0) -> dict:
    key = jax.random.key(seed)
    keys = iter(jax.random.split(key, 64))
    f32 = jnp.float32
    L = DEPTH
    D = D_MODEL

    def nrm(shape, scale):
        return scale * jax.random.normal(next(keys), shape, f32)

    def uni(shape, lo, hi):
        return jax.random.uniform(next(keys), shape, f32, lo, hi)

    def gain(shape):
        return 1.0 + nrm(shape, 0.02)

    return {
        'x': nrm((BATCH, SEQ, D), 1.0),
        'c': nrm((BATCH, D), 1.0),
        'ctx': nrm((BATCH, CTX_LEN, D), 1.0),
        'c_ctx': nrm((D,), 1.0),
        'mod_w': nrm((L, D, N_MOD * D), 0.5 * D ** -0.5),
        'mod_b': nrm((L, N_MOD * D), 0.02),
        'norm1_g': gain((L, D)),
        'norm2_g': gain((L, D)),
        'w_in': nrm((L, D, IN_WIDTH), D ** -0.5),
        'w_out': nrm((L, D_MIX, D), D_MIX ** -0.5),
        's5_a_re': -0.5 + nrm((L, 2, S5_GROUPS, S5_STATE), 0.01),
        's5_a_im': math.pi * jnp.arange(S5_STATE, dtype=f32) + nrm((L, 2, S5_GROUPS, S5_STATE), 0.01),
        's5_log_step': uni((L, 2, S5_GROUPS), math.log(1e-3), math.log(1e-1)),
        's5_b_re': nrm((L, 2, S5_GROUPS, S5_STATE, S5_GROUP_CH), (2 * S5_GROUP_CH) ** -0.5),
        's5_b_im': nrm((L, 2, S5_GROUPS, S5_STATE, S5_GROUP_CH), (2 * S5_GROUP_CH) ** -0.5),
        's5_c_re': nrm((L, 2, S5_GROUPS, S5_GROUP_CH, S5_STATE), (2 * S5_STATE) ** -0.5),
        's5_c_im': nrm((L, 2, S5_GROUPS, S5_GROUP_CH, S5_STATE), (2 * S5_STATE) ** -0.5),
        's5_d': nrm((L, S5_WIDTH), 0.5),
        's5_w_glu': nrm((L, S5_WIDTH, 2 * S5_WIDTH), S5_WIDTH ** -0.5),
        'mla_q_a_norm': gain((L, MLA_Q_RANK)),
        'mla_kv_a_norm': gain((L, MLA_KV_RANK)),
        'mla_w_uq': nrm((L, MLA_Q_RANK, MLA_HEADS * (MLA_NOPE + MLA_ROPE)), MLA_Q_RANK ** -0.5),
        'mla_w_ukv': nrm((L, MLA_KV_RANK, MLA_HEADS * (MLA_NOPE + MLA_V)), MLA_KV_RANK ** -0.5),
        'mla_q_norm': gain((L, MLA_NOPE + MLA_ROPE)),
        'mla_k_norm': gain((L, MLA_NOPE + MLA_ROPE)),
        'rw_mu': uni((L, 2, RW_IN), 0.0, 0.5),
        'rw_w0': uni((L, 2, RW_WIDTH), -6.0, -1.0),
        'rw_w2': nrm((L, 2, RW_W_RANK, RW_WIDTH), 0.1 * RW_W_RANK ** -0.5),
        'rw_a0': nrm((L, 2, RW_WIDTH), 0.1),
        'rw_a2': nrm((L, 2, RW_A_RANK, RW_WIDTH), 0.3 * RW_A_RANK ** -0.5),
        'rw_g2': nrm((L, RW_G_RANK, RW_WIDTH), RW_G_RANK ** -0.5),
        'rw_k_k': 0.85 + nrm((L, RW_WIDTH), 0.02),
        'rw_k_a': gain((L, RW_WIDTH)),
        'rw_r_k': nrm((L, RW_HEADS, RW_HEAD), 0.1),
        'rw_ln_g': gain((L, RW_WIDTH)),
        'rw_ln_b': nrm((L, RW_WIDTH), 0.02),
        'diff_q_norm': gain((L, DIFF_HEAD)),
        'diff_k_norm': gain((L, DIFF_HEAD)),
        'diff_lambda': nrm((L, 4, DIFF_HEAD), 0.1),
        'diff_sub_norm': gain((L, 2 * DIFF_HEAD)),
        'moe_wr_coarse': nrm((L, D, MOE_GROUPS), D ** -0.5),
        'moe_br_coarse': nrm((L, MOE_GROUPS), 0.01),
        'moe_wr_fine': nrm((L, D, MOE_EXPERTS), D ** -0.5),
        'moe_br_fine': nrm((L, MOE_EXPERTS), 0.01),
        'moe_w_gate': nrm((L, MOE_EXPERTS, D, MOE_HIDDEN), D ** -0.5),
        'moe_w_up': nrm((L, MOE_EXPERTS, D, MOE_HIDDEN), D ** -0.5),
        'moe_w_down': nrm((L, MOE_EXPERTS, MOE_HIDDEN, D), MOE_HIDDEN ** -0.5),
    }


def reference(x, c, ctx, c_ctx, mod_w, mod_b, norm1_g, norm2_g, w_in, w_out,
              s5_a_re, s5_a_im, s5_log_step, s5_b_re, s5_b_im, s5_c_re, s5_c_im, s5_d, s5_w_glu,
              mla_q_a_norm, mla_kv_a_norm, mla_w_uq, mla_w_ukv, mla_q_norm, mla_k_norm,
              rw_mu, rw_w0, rw_w2, rw_a0, rw_a2, rw_g2, rw_k_k, rw_k_a, rw_r_k, rw_ln_g, rw_ln_b,
              diff_q_norm, diff_k_norm, diff_lambda, diff_sub_norm,
              moe_wr_coarse, moe_br_coarse, moe_wr_fine, moe_br_fine, moe_w_gate, moe_w_up, moe_w_down):
    b, n, d = x.shape
    lc = ctx.shape[1]
    rows = n // GRID_W
    cos_m, sin_m = axial_rope(rows, MLA_ROPE)
    cos_d, sin_d = axial_rope(rows, DIFF_HEAD)
    act_c = jax.nn.silu(c)
    act_ctx = jax.nn.silu(c_ctx)
    for l in range(DEPTH):
        last = l == DEPTH - 1
        lam_init = 0.8 - 0.6 * math.exp(-0.3 * l)
        mod_lat = jnp.split((act_c @ mod_w[l] + mod_b[l])[:, None, :], N_MOD, axis=-1)
        mod_ctx = jnp.split(act_ctx @ mod_w[l] + mod_b[l], N_MOD, axis=-1)

        h = jnp.concatenate([modulate(rms_norm(ctx, norm1_g[l]), mod_ctx[0], mod_ctx[1]),
                             modulate(rms_norm(x, norm1_g[l]), mod_lat[0], mod_lat[1])], axis=1)
        z = h @ w_in[l]
        u_c, cq_c, ckv_c, kr_c, rw_c, dq_c, dk_c, dv_c = split_cols(z[:, :lc], IN_SPLITS)
        u_l, cq_l, ckv_l, kr_l, rw_l, dq_l, dk_l, dv_l = split_cols(z[:, lc:], IN_SPLITS)
        ya_c, ya_l = s5_mixer(u_c, u_l, s5_a_re[l], s5_a_im[l], s5_log_step[l], s5_b_re[l], s5_b_im[l],
                              s5_c_re[l], s5_c_im[l], s5_d[l], s5_w_glu[l])
        yb_c, yb_l = mla_mixer(cq_c, ckv_c, kr_c, cq_l, ckv_l, kr_l, mla_q_a_norm[l], mla_kv_a_norm[l],
                               mla_w_uq[l], mla_w_ukv[l], mla_q_norm[l], mla_k_norm[l], cos_m, sin_m)
        yc_c, yc_l = rwkv_mixer(rw_c, rw_l, rw_mu[l], rw_w0[l], rw_w2[l], rw_a0[l], rw_a2[l], rw_g2[l],
                                rw_k_k[l], rw_k_a[l], rw_r_k[l], rw_ln_g[l], rw_ln_b[l])
        yd_c, yd_l = diff_mixer(dq_c, dk_c, dv_c, dq_l, dk_l, dv_l, diff_q_norm[l], diff_k_norm[l],
                                diff_lambda[l], diff_sub_norm[l], lam_init, cos_d, sin_d)
        x = x + mod_lat[2] * (jnp.concatenate([ya_l, yb_l, yc_l, yd_l], axis=-1) @ w_out[l])
        if not last:
            ctx = ctx + mod_ctx[2] * (jnp.concatenate([ya_c, yb_c, yc_c, yd_c], axis=-1) @ w_out[l])

        moe_params = (moe_wr_coarse[l], moe_br_coarse[l], moe_wr_fine[l], moe_br_fine[l],
                      moe_w_gate[l], moe_w_up[l], moe_w_down[l])
        h2_l = modulate(rms_norm(x, norm2_g[l]), mod_lat[3], mod_lat[4])
        if last:
            y_l = hier_moe(h2_l.reshape(-1, d), *moe_params).reshape(b, n, d)
        else:
            h2_c = modulate(rms_norm(ctx, norm2_g[l]), mod_ctx[3], mod_ctx[4])
            y = hier_moe(jnp.concatenate([h2_c, h2_l], axis=1).reshape(-1, d), *moe_params)
            y = y.reshape(b, lc + n, d)
            ctx = ctx + mod_ctx[5] * y[:, :lc]
            y_l = y[:, lc:]
        x = x + mod_lat[5] * y_l
    return x
```

```python
import functools
import math

import jax
import jax.numpy as jnp
import numpy as np
from jax import lax
from jax.experimental import pallas as pl
from jax.experimental.pallas import tpu as pltpu

F32 = jnp.float32
BF16 = jnp.bfloat16

D_MODEL = 4096
N_MOD = 6
GROUP_W = D_MODEL // 4
GRID_W = 64
ROPE_BASE = 10000.0
NORM_EPS = 1e-6

S5_GROUP_CH = 16
S5_GROUPS = GROUP_W // S5_GROUP_CH
S5_STATE = 64
S5_CHUNK = 16
S5_ROWS_PER_CHUNK = 8

MLA_HEADS = 8
MLA_NOPE = 128
MLA_ROPE = 64
MLA_V = GROUP_W // MLA_HEADS
MLA_Q_RANK = 512
MLA_KV_RANK = 256

RW_HEAD = 64
RW_HEADS = GROUP_W // RW_HEAD
RW_W_RANK = 64
RW_A_RANK = 64
RW_G_RANK = 128
RW_GN_EPS = 64e-5
RW_IN = 3 * GROUP_W + RW_W_RANK + RW_A_RANK + RW_G_RANK

DIFF_HEAD = 64
DIFF_HEADS = GROUP_W // (2 * DIFF_HEAD)

MOE_GROUPS = 4
MOE_PER_GROUP = 8
MOE_EXPERTS = MOE_GROUPS * MOE_PER_GROUP
MOE_TOP_K = 2
MOE_HIDDEN = 512
MOE_BLOCK = 128

KROPE_PAD = 256
Z_U = 0
Z_CQ = Z_U + GROUP_W
Z_CKV = Z_CQ + MLA_Q_RANK
Z_KR = Z_CKV + MLA_KV_RANK
Z_RW = Z_KR + KROPE_PAD
Z_DQ = Z_RW + RW_IN
Z_DK = Z_DQ + GROUP_W
Z_DV = Z_DK + GROUP_W
Z_WIDTH = Z_DV + GROUP_W

VMEM_LIMIT = 56 * 1024 * 1024


def _cparams(sem):
    return pltpu.CompilerParams(dimension_semantics=sem, vmem_limit_bytes=VMEM_LIMIT)


def _mm_kernel(a_ref, b_ref, o_ref):
    a = a_ref[...].astype(BF16)
    b = b_ref[...].astype(BF16)
    o_ref[...] = jnp.dot(a, b, preferred_element_type=F32).astype(o_ref.dtype)


def _pick(n, prefs):
    for p in prefs:
        if n % p == 0:
            return p
    return n


def matmul(a, b, out_dtype=F32, tm=None, tn=None):
    m, k = a.shape
    _, n = b.shape
    tm = tm or _pick(m, (512, 256, 128))
    tn = tn or _pick(n, (1024, 768, 512, 256, 128))
    return pl.pallas_call(
        _mm_kernel,
        grid=(m // tm, n // tn),
        in_specs=[pl.BlockSpec((tm, k), lambda i, j: (i, 0)),
                  pl.BlockSpec((k, tn), lambda i, j: (0, j))],
        out_specs=pl.BlockSpec((tm, tn), lambda i, j: (i, j)),
        out_shape=jax.ShapeDtypeStruct((m, n), out_dtype),
        compiler_params=_cparams(("parallel", "parallel")),
        name="matmul",
    )(a, b)


def _normmod_kernel(x_ref, g_ref, sh_ref, sc_ref, o_ref):
    x = x_ref[0]
    ms = jnp.mean(x * x, axis=-1, keepdims=True)
    y = x * lax.rsqrt(ms + NORM_EPS) * g_ref[...]
    o_ref[0] = (y * (1.0 + sc_ref[0]) + sh_ref[0]).astype(o_ref.dtype)


def normmod(xs, g, shift, scale, lc, out_dtype=BF16):
    b, n, d = xs.shape
    tb = lc
    sh = shift.reshape(b * 2, 1, d)
    sc = scale.reshape(b * 2, 1, d)
    seg = lambda bi, i: (bi * 2 + jnp.minimum(i, 1), 0, 0)
    return pl.pallas_call(
        _normmod_kernel,
        grid=(b, n // tb),
        in_specs=[pl.BlockSpec((1, tb, d), lambda bi, i: (bi, i, 0)),
                  pl.BlockSpec((1, d), lambda bi, i: (0, 0)),
                  pl.BlockSpec((1, 1, d), seg),
                  pl.BlockSpec((1, 1, d), seg)],
        out_specs=pl.BlockSpec((1, tb, d), lambda bi, i: (bi, i, 0)),
        out_shape=jax.ShapeDtypeStruct((b, n, d), out_dtype),
        compiler_params=_cparams(("parallel", "parallel")),
        name="normmod",
    )(xs, g.reshape(1, d), sh, sc)


def _mla_attn_kernel(q_ref, k_ref, v_ref, o_ref, *, lc, scale):
    q = q_ref[0, 0]

    def attend(nk):
        k = k_ref[0, 0, :nk]
        v = v_ref[0, 0, :nk]
        s = lax.dot_general(q, k, (((1,), (1,)), ((), ())), preferred_element_type=F32) * scale
        m = jnp.max(s, axis=-1, keepdims=True)
        p = jnp.exp(s - m)
        l = jnp.sum(p, axis=-1, keepdims=True)
        o = jnp.dot(p.astype(BF16), v, preferred_element_type=F32)
        o_ref[0] = (o / l).astype(o_ref.dtype)

    i = pl.program_id(2)

    @pl.when(i == 0)
    def _():
        attend(lc)

    @pl.when(i > 0)
    def _():
        attend(k_ref.shape[2])


def mla_attention(q, k, v, lc):
    b, h, n, dq = q.shape
    dv = v.shape[-1]
    tq = lc
    kern = functools.partial(_mla_attn_kernel, lc=lc, scale=float(dq) ** -0.5)
    return pl.pallas_call(
        kern,
        grid=(b, h, n // tq),
        in_specs=[pl.BlockSpec((1, 1, tq, dq), lambda bi, hi, i: (bi, hi, i, 0)),
                  pl.BlockSpec((1, 1, n, dq), lambda bi, hi, i: (bi, hi, 0, 0)),
                  pl.BlockSpec((1, 1, n, dv), lambda bi, hi, i: (bi, hi, 0, 0))],
        out_specs=pl.BlockSpec((1, tq, dv), lambda bi, hi, i: (bi, i, hi)),
        out_shape=jax.ShapeDtypeStruct((b, n, h * dv), F32),
        compiler_params=_cparams(("parallel", "parallel", "arbitrary")),
        name="mla_attention",
    )(q, k, v)


def _diff_attn_kernel(q0_ref, q1_ref, k0_ref, k1_ref, v_ref, lam_ref, g_ref, o_ref, *, lc, scale, post):
    q0 = q0_ref[0, 0]
    q1 = q1_ref[0, 0]
    lam = lam_ref[...]

    def probs(q, k):
        s = lax.dot_general(q, k, (((1,), (1,)), ((), ())), preferred_element_type=F32) * scale
        m = jnp.max(s, axis=-1, keepdims=True)
        p = jnp.exp(s - m)
        return p, jnp.sum(p, axis=-1, keepdims=True)

    def attend(nk):
        p0, l0 = probs(q0, k0_ref[0, 0, :nk])
        p1, l1 = probs(q1, k1_ref[0, 0, :nk])
        attn = p0 * (1.0 / l0) - p1 * (lam[:, :1] / l1)
        o = jnp.dot(attn.astype(BF16), v_ref[0, 0, :nk], preferred_element_type=F32)
        ms = jnp.mean(o * o, axis=-1, keepdims=True)
        o_ref[0] = (o * lax.rsqrt(ms + NORM_EPS) * g_ref[...] * post).astype(o_ref.dtype)

    i = pl.program_id(2)

    @pl.when(i == 0)
    def _():
        attend(lc)

    @pl.when(i > 0)
    def _():
        attend(v_ref.shape[2])


def diff_attention(q0, q1, k0, k1, v, lam, sub_norm, lc, post):
    b, h, n, dh = q0.shape
    dv = v.shape[-1]
    tq = lc
    kern = functools.partial(_diff_attn_kernel, lc=lc, scale=float(dh) ** -0.5, post=post)
    qspec = pl.BlockSpec((1, 1, tq, dh), lambda bi, hi, i: (bi, hi, i, 0))
    kspec = pl.BlockSpec((1, 1, n, dh), lambda bi, hi, i: (bi, hi, 0, 0))
    return pl.pallas_call(
        kern,
        grid=(b, h, n // tq),
        in_specs=[qspec, qspec, kspec, kspec,
                  pl.BlockSpec((1, 1, n, dv), lambda bi, hi, i: (bi, hi, 0, 0)),
                  pl.BlockSpec((1, dv), lambda bi, hi, i: (0, 0)),
                  pl.BlockSpec((1, dv), lambda bi, hi, i: (0, 0))],
        out_specs=pl.BlockSpec((1, tq, dv), lambda bi, hi, i: (bi, i, hi)),
        out_shape=jax.ShapeDtypeStruct((b, n, h * dv), F32),
        compiler_params=_cparams(("parallel", "parallel", "arbitrary")),
        name="diff_attention",
    )(q0, q1, k0, k1, v, jnp.broadcast_to(lam.reshape(1, 1), (1, dv)).astype(F32),
      sub_norm.reshape(1, dv))


def _s5_kernel(u_ref, toep_ref, wbr_ref, wbi_ref, wcr_ref, wci_ref, lr_ref, li_ref, y_ref,
               er, ei, hr, hi):
    hp = lax.Precision.HIGHEST
    u = u_ref[0, 0]
    er[...] = jnp.dot(u, wbr_ref[0, 0], precision=hp, preferred_element_type=F32)
    ei[...] = jnp.dot(u, wbi_ref[0, 0], precision=hp, preferred_element_type=F32)
    a = lr_ref[0, 0]
    b = li_ref[0, 0]
    rows = S5_ROWS_PER_CHUNK

    def body(c, carry):
        xr, xi = carry
        off = pl.multiple_of(c * rows, rows)
        hr[pl.ds(off, rows), :] = xr
        hi[pl.ds(off, rows), :] = xi
        return (a * xr - b * xi + er[pl.ds(off, rows), :],
                a * xi + b * xr + ei[pl.ds(off, rows), :])

    zero = jnp.zeros((rows, S5_STATE), F32)
    lax.fori_loop(0, u.shape[0] // rows, body, (zero, zero))
    y = jnp.dot(u, toep_ref[0, 0], precision=hp, preferred_element_type=F32)
    y += jnp.dot(hr[...], wcr_ref[0, 0], precision=hp, preferred_element_type=F32)
    y += jnp.dot(hi[...], wci_ref[0, 0], precision=hp, preferred_element_type=F32)
    y_ref[0, 0] = y


def s5_scan(u, mats):
    toep, wbr, wbi, wcr, wci, lr, li = mats
    _, g, r, w = u.shape
    p = S5_STATE
    spec = lambda *shape: pl.BlockSpec((1, 1) + shape, lambda d, gi: (d, gi, 0, 0))
    return pl.pallas_call(
        _s5_kernel,
        grid=(2, g),
        in_specs=[spec(r, w), spec(w, w), spec(w, p), spec(w, p), spec(p, w), spec(p, w),
                  spec(1, p), spec(1, p)],
        out_specs=spec(r, w),
        out_shape=jax.ShapeDtypeStruct(u.shape, F32),
        scratch_shapes=[pltpu.VMEM((r, p), F32)] * 4,
        compiler_params=_cparams(("parallel", "parallel")),
        name="s5_scan",
    )(u, toep, wbr, wbi, wcr, wci, lr, li)


def _cmul(ar, ai, br, bi):
    return ar * br - ai * bi, ar * bi + ai * br


def s5_matrices(a_re, a_im, log_step, b_re, b_im, c_re, c_im):
    lc = S5_CHUNK
    step = jnp.exp(log_step)[..., None]
    taus = jnp.arange(lc + 1, dtype=F32)
    mag = jnp.exp((a_re * step)[..., None] * taus)
    ang = (a_im * step)[..., None] * taus
    pr, pi = mag * jnp.cos(ang), mag * jnp.sin(ang)
    den = a_re * a_re + a_im * a_im
    nr, ni = pr[..., 1] - 1.0, pi[..., 1]
    fr = (nr * a_re + ni * a_im) / den
    fi = (ni * a_re - nr * a_im) / den
    bbr, bbi = _cmul(fr[..., None], fi[..., None], b_re, b_im)
    cpr, cpi = _cmul(c_re[..., None], c_im[..., None], pr[:, :, None], pi[:, :, None])
    kern = (jnp.einsum('dgopt,dgpi->dgtoi', cpr[..., :lc], bbr, precision='highest')
            - jnp.einsum('dgopt,dgpi->dgtoi', cpi[..., :lc], bbi, precision='highest'))
    j = np.arange(lc)[:, None]
    s = np.arange(lc)[None, :]
    lag = np.clip(s - j, 0, lc - 1)
    toep = kern[:, :, lag] * jnp.asarray((s >= j)[..., None, None], F32)
    toep = toep.transpose(0, 1, 2, 5, 3, 4).reshape(2, S5_GROUPS, lc * S5_GROUP_CH, lc * S5_GROUP_CH)
    rev = lc - 1 - np.arange(lc)
    wr, wi = _cmul(pr[..., rev][..., None], pi[..., rev][..., None], bbr[:, :, :, None], bbi[:, :, :, None])
    wbr = wr.transpose(0, 1, 3, 4, 2).reshape(2, S5_GROUPS, lc * S5_GROUP_CH, S5_STATE)
    wbi = wi.transpose(0, 1, 3, 4, 2).reshape(2, S5_GROUPS, lc * S5_GROUP_CH, S5_STATE)
    wcr = cpr[..., 1:].transpose(0, 1, 3, 4, 2).reshape(2, S5_GROUPS, S5_STATE, lc * S5_GROUP_CH)
    wci = -cpi[..., 1:].transpose(0, 1, 3, 4, 2).reshape(2, S5_GROUPS, S5_STATE, lc * S5_GROUP_CH)
    lr = pr[..., lc][:, :, None, :]
    li = pi[..., lc][:, :, None, :]
    return toep, wbr, wbi, wcr, wci, lr, li


def _rwkv_kernel(w_ref, kk_ref, b_ref, k_ref, r_ref, v_ref, y_ref, s_ref):
    @pl.when(pl.program_id(0) == 0)
    def _():
        s_ref[...] = jnp.zeros_like(s_ref)

    def step(t, carry):
        def vgroup(vg, carry2):
            off = pl.multiple_of(vg * 8, 8)
            vrows = v_ref[t, pl.ds(off, 8), :]
            outs = []
            for j in range(8):
                sv = s_ref[off + j]
                sa = jnp.sum(sv * kk_ref[t], axis=0, keepdims=True)
                sn = sv * w_ref[t] - sa * b_ref[t] + vrows[j:j + 1] * k_ref[t]
                s_ref[off + j] = sn
                outs.append(jnp.sum(sn * r_ref[t], axis=0, keepdims=True))
            y_ref[t, pl.ds(off, 8), :] = jnp.concatenate(outs, axis=0)
            return carry2

        return lax.fori_loop(0, RW_HEAD // 8, vgroup, carry)

    lax.fori_loop(0, w_ref.shape[0], step, 0)


def rwkv_scan(w, kk, bv, k, r, v, tb=32):
    t, _, chains = w.shape
    spec = pl.BlockSpec((tb, RW_HEAD, chains), lambda i: (i, 0, 0))
    return pl.pallas_call(
        _rwkv_kernel,
        grid=(t // tb,),
        in_specs=[spec] * 6,
        out_specs=spec,
        out_shape=jax.ShapeDtypeStruct((t, RW_HEAD, chains), F32),
        scratch_shapes=[pltpu.VMEM((RW_HEAD, RW_HEAD, chains), F32)],
        compiler_params=_cparams(("arbitrary",)),
        name="rwkv_scan",
    )(w, kk, bv, k, r, v)


def _moe_kernel(be_ref, x_ref, wg_ref, wu_ref, wd_ref, o_ref):
    del be_ref
    x = x_ref[...]
    g = jnp.dot(x, wg_ref[0], preferred_element_type=F32)
    u = jnp.dot(x, wu_ref[0], preferred_element_type=F32)
    h = (g * jax.nn.sigmoid(g) * u).astype(BF16)
    o_ref[...] = jnp.dot(h, wd_ref[0], preferred_element_type=F32)


def moe_ffn(block_exp, xb, wg, wu, wd):
    cap, d = xb.shape
    hid = wg.shape[-1]
    nb = cap // MOE_BLOCK
    grid_spec = pltpu.PrefetchScalarGridSpec(
        num_scalar_prefetch=1,
        grid=(nb,),
        in_specs=[pl.BlockSpec((MOE_BLOCK, d), lambda i, be: (i, 0)),
                  pl.BlockSpec((1, d, hid), lambda i, be: (be[i], 0, 0)),
                  pl.BlockSpec((1, d, hid), lambda i, be: (be[i], 0, 0)),
                  pl.BlockSpec((1, hid, d), lambda i, be: (be[i], 0, 0))],
        out_specs=pl.BlockSpec((MOE_BLOCK, d), lambda i, be: (i, 0)),
    )
    return pl.pallas_call(
        _moe_kernel,
        grid_spec=grid_spec,
        out_shape=jax.ShapeDtypeStruct((cap, d), F32),
        compiler_params=_cparams(("arbitrary",)),
        name="moe_ffn",
    )(block_exp, xb, wg, wu, wd)


def _rms(x, g):
    return x * lax.rsqrt(jnp.mean(x * x, axis=-1, keepdims=True) + NORM_EPS) * g


def _axial_rope(rows, dim):
    n_freq = dim // 4
    inv = ROPE_BASE ** (-jnp.arange(n_freq, dtype=F32) / n_freq)
    row = jnp.repeat(jnp.arange(rows, dtype=F32), GRID_W)
    col = jnp.tile(jnp.arange(GRID_W, dtype=F32), rows)
    ang = jnp.concatenate([row[:, None] * inv, col[:, None] * inv], axis=-1)
    return jnp.cos(ang), jnp.sin(ang)


def _rope(x, cos, sin):
    half = x.shape[-1] // 2
    x1, x2 = x[..., :half], x[..., half:]
    return jnp.concatenate([x1 * cos - x2 * sin, x1 * sin + x2 * cos], axis=-1)


def _rope_lat(x, cos, sin, lc):
    extra = (None,) * (x.ndim - 3)
    cs = cos[(slice(None),) + extra]
    sn = sin[(slice(None),) + extra]
    return jnp.concatenate([x[:, :lc], _rope(x[:, lc:], cs, sn)], axis=1)


def _seq_rev(t, lc):
    return jnp.concatenate([jnp.flip(t[:, :lc], 1), jnp.flip(t[:, lc:], 1)], axis=1)


def _mm3(x, w, **kw):
    b, n, k = x.shape
    return matmul(x.reshape(b * n, k), w, **kw).reshape(b, n, -1)


def _s5_mixer(u, lc, mats, d_skip, w_glu):
    b, n, _ = u.shape
    nc = n // S5_CHUNK

    def rows(x):
        x = x.reshape(b, nc, S5_CHUNK, S5_GROUPS, S5_GROUP_CH).transpose(3, 1, 0, 2, 4)
        x = jnp.pad(x, ((0, 0), (0, 0), (0, S5_ROWS_PER_CHUNK - b), (0, 0), (0, 0)))
        return x.reshape(S5_GROUPS, nc * S5_ROWS_PER_CHUNK, S5_CHUNK * S5_GROUP_CH)

    def unrows(y):
        y = y.reshape(S5_GROUPS, nc, S5_ROWS_PER_CHUNK, S5_CHUNK, S5_GROUP_CH)[:, :, :b]
        return y.transpose(2, 1, 3, 0, 4).reshape(b, n, GROUP_W)

    y = s5_scan(jnp.stack([rows(u), rows(_seq_rev(u, lc))]), mats)
    y = unrows(y[0]) + _seq_rev(unrows(y[1]), lc) + d_skip * u
    y = jax.nn.gelu(y)
    gv = _mm3(y, w_glu)
    return gv[..., :GROUP_W] * jax.nn.sigmoid(gv[..., GROUP_W:])


def _mla_mixer(cq, ckv, krope, lc, q_a_norm, kv_a_norm, w_uq, w_ukv, q_norm, k_norm, cos, sin):
    b, n, _ = cq.shape
    q = _mm3(_rms(cq, q_a_norm), w_uq).reshape(b, n, MLA_HEADS, MLA_NOPE + MLA_ROPE)
    kv = _mm3(_rms(ckv, kv_a_norm), w_ukv).reshape(b, n, MLA_HEADS, MLA_NOPE + MLA_V)
    q_nope = _rms(q[..., :MLA_NOPE], q_norm[:MLA_NOPE])
    q_rope = _rope_lat(_rms(q[..., MLA_NOPE:], q_norm[MLA_NOPE:]), cos, sin, lc)
    k_nope = _rms(kv[..., :MLA_NOPE], k_norm[:MLA_NOPE])
    k_rope = _rope_lat(_rms(krope, k_norm[MLA_NOPE:]), cos, sin, lc)
    qf = jnp.concatenate([q_nope, q_rope], axis=-1)
    kf = jnp.concatenate([k_nope, jnp.broadcast_to(k_rope[:, :, None], (b, n, MLA_HEADS, MLA_ROPE))], axis=-1)
    t = lambda x: x.transpose(0, 2, 1, 3).astype(BF16)
    return mla_attention(t(qf), t(kf), t(kv[..., MLA_NOPE:]), lc)


def _diff_mixer(dq, dk, dv, lc, q_norm, k_norm, lam_vecs, sub_norm, lam_init, cos, sin):
    b, n, _ = dq.shape
    q = _rope_lat(_rms(dq.reshape(b, n, DIFF_HEADS, 2, DIFF_HEAD), q_norm), cos, sin, lc)
    k = _rope_lat(_rms(dk.reshape(b, n, DIFF_HEADS, 2, DIFF_HEAD), k_norm), cos, sin, lc)
    lam = jnp.exp(jnp.sum(lam_vecs[0] * lam_vecs[1])) - jnp.exp(jnp.sum(lam_vecs[2] * lam_vecs[3])) + lam_init
    t = lambda x: x.transpose(0, 2, 1, 3).astype(BF16)
    return diff_attention(t(q[:, :, :, 0]), t(q[:, :, :, 1]), t(k[:, :, :, 0]), t(k[:, :, :, 1]),
                          t(dv.reshape(b, n, DIFF_HEADS, 2 * DIFF_HEAD)), lam, sub_norm, lc,
                          1.0 - lam_init)


def _centred_shift(z, mu_prev, mu_next):
    z_prev = jnp.pad(z, ((0, 0), (1, 0), (0, 0)))[:, :-1]
    z_next = jnp.pad(z, ((0, 0), (0, 1), (0, 0)))[:, 1:]
    return z + mu_prev * (z_prev - z) + mu_next * (z_next - z)


def _rwkv_mixer(z, lc, mu, w0, w2, a0, a2, g2, k_k, k_a, r_k, ln_g, ln_b):
    b, n, _ = z.shape
    heads = lambda t: t.reshape(b, n, RW_HEADS, RW_HEAD)
    z = jnp.concatenate([_centred_shift(z[:, :lc], mu[0], mu[1]),
                         _centred_shift(z[:, lc:], mu[0], mu[1])], axis=1)
    r, k, v = z[..., :GROUP_W], z[..., GROUP_W:2 * GROUP_W], z[..., 2 * GROUP_W:3 * GROUP_W]
    o = 3 * GROUP_W
    wl, al, gl = z[..., o:o + 64], z[..., o + 64:o + 128], z[..., o + 128:o + 256]
    kk = heads(k * k_k)
    kk = kk * lax.rsqrt(jnp.maximum(jnp.sum(kk * kk, axis=-1, keepdims=True), 1e-12))
    lr_in = jnp.concatenate([jnp.tanh(wl), al, jax.nn.sigmoid(gl)], axis=-1)
    zw = jnp.zeros((RW_W_RANK, GROUP_W), F32)
    zg = jnp.zeros((RW_G_RANK, GROUP_W), F32)
    w_lr = jnp.concatenate([
        jnp.concatenate([w2[0], w2[1], zw, zw, zw], axis=1),
        jnp.concatenate([zw, zw, a2[0], a2[1], zw], axis=1),
        jnp.concatenate([zg, zg, zg, zg, g2], axis=1)], axis=0)
    lr = _mm3(lr_in, w_lr)
    g = lr[..., 4 * GROUP_W:]
    ka = k_a.reshape(RW_HEADS, RW_HEAD)
    rh, vh, kh = heads(r), heads(v), heads(k)
    per_dir = []
    for dr in range(2):
        w = -jax.nn.softplus(-heads(w0[dr] + lr[..., dr * GROUP_W:(dr + 1) * GROUP_W])) - 0.5
        a = jax.nn.sigmoid(heads(a0[dr] + lr[..., (2 + dr) * GROUP_W:(3 + dr) * GROUP_W]))
        per_dir.append((jnp.exp(-jnp.exp(w)), kh * (1 + (a - 1) * ka), a))

    def lanes(fwd, bwd):
        both = jnp.stack([fwd, _seq_rev(bwd, lc)])
        return both.transpose(2, 4, 0, 1, 3).reshape(n, RW_HEAD, 2 * b * RW_HEADS)

    y = rwkv_scan(lanes(per_dir[0][0], per_dir[1][0]), lanes(kk, kk),
                  lanes(kk * per_dir[0][2], kk * per_dir[1][2]),
                  lanes(per_dir[0][1], per_dir[1][1]), lanes(rh, rh), lanes(vh, vh))
    y = y.reshape(n, RW_HEAD, 2, b, RW_HEADS).transpose(2, 3, 0, 4, 1)
    o_f, o_b = y[0], _seq_rev(y[1], lc)
    bonus = (jnp.sum(rh * per_dir[0][1] * r_k, axis=-1, keepdims=True)
             + jnp.sum(rh * per_dir[1][1] * r_k, axis=-1, keepdims=True)) * vh
    o = o_f + o_b + bonus
    mean = jnp.mean(o, axis=-1, keepdims=True)
    var = jnp.mean(jnp.square(o - mean), axis=-1, keepdims=True)
    o = ((o - mean) * lax.rsqrt(var + RW_GN_EPS)).reshape(b, n, GROUP_W)
    return (o * ln_g + ln_b) * g


def _hier_moe(h, wr, br_coarse, br_fine, wg, wu, wd):
    t, d = h.shape
    logits = matmul(h, wr)
    p_group = jax.nn.softmax(logits[:, :MOE_GROUPS] + br_coarse, axis=-1)
    group = jnp.argmax(p_group, axis=-1).astype(jnp.int32)
    p_sel = jnp.take_along_axis(p_group, group[:, None], axis=-1)
    fine = (logits[:, MOE_GROUPS:MOE_GROUPS + MOE_EXPERTS] + br_fine).reshape(t, MOE_GROUPS, MOE_PER_GROUP)
    fine = jnp.take_along_axis(fine, group[:, None, None], axis=1)[:, 0]
    top_v, top_i = lax.top_k(fine, MOE_TOP_K)
    weight = p_sel * jax.nn.softmax(top_v, axis=-1)
    expert = group[:, None] * MOE_PER_GROUP + top_i.astype(jnp.int32)
    flat_e = expert.reshape(-1)
    n_assign = flat_e.shape[0]
    order = jnp.argsort(flat_e)
    sorted_e = flat_e[order]
    counts = jnp.bincount(flat_e, length=MOE_EXPERTS)
    padded = (counts + MOE_BLOCK - 1) // MOE_BLOCK * MOE_BLOCK
    pad_end = jnp.cumsum(padded)
    pad_start = pad_end - padded
    seg_start = jnp.cumsum(counts) - counts
    dest = (pad_start[sorted_e] + jnp.arange(n_assign) - seg_start[sorted_e]).astype(jnp.int32)
    n_blocks = -(-(n_assign + MOE_EXPERTS * (MOE_BLOCK - 1)) // MOE_BLOCK)
    cap = n_blocks * MOE_BLOCK
    slot_tok = jnp.full((cap,), t, jnp.int32).at[dest].set((order // MOE_TOP_K).astype(jnp.int32))
    block_exp = jnp.minimum(jnp.searchsorted(pad_end, jnp.arange(n_blocks) * MOE_BLOCK, side='right'),
                            MOE_EXPERTS - 1).astype(jnp.int32)
    h_pad = jnp.concatenate([h, jnp.zeros((1, d), h.dtype)], axis=0)
    yb = moe_ffn(block_exp, h_pad[slot_tok], wg, wu, wd)
    slot_of = jnp.zeros((n_assign,), jnp.int32).at[order].set(dest).reshape(t, MOE_TOP_K)
    return weight[:, 0:1] * yb[slot_of[:, 0]] + weight[:, 1:2] * yb[slot_of[:, 1]]


def kernel(x, c, ctx, c_ctx, mod_w, mod_b, norm1_g, norm2_g, w_in, w_out, s5_a_re, s5_a_im, s5_log_step, s5_b_re, s5_b_im, s5_c_re, s5_c_im, s5_d, s5_w_glu, mla_q_a_norm, mla_kv_a_norm, mla_w_uq, mla_w_ukv, mla_q_norm, mla_k_norm, rw_mu, rw_w0, rw_w2, rw_a0, rw_a2, rw_g2, rw_k_k, rw_k_a, rw_r_k, rw_ln_g, rw_ln_b, diff_q_norm, diff_k_norm, diff_lambda, diff_sub_norm, moe_wr_coarse, moe_br_coarse, moe_wr_fine, moe_br_fine, moe_w_gate, moe_w_up, moe_w_down):
    b, n, d = x.shape
    lc = ctx.shape[1]
    depth = mod_w.shape[0]
    rows = n // GRID_W
    cos_m, sin_m = _axial_rope(rows, MLA_ROPE)
    cos_d, sin_d = _axial_rope(rows, DIFF_HEAD)
    act = jnp.concatenate([jax.nn.silu(c), jax.nn.silu(c_ctx)[None],
                           jnp.zeros((8 - b - 1, d), F32)], axis=0)
    xs = jnp.concatenate([ctx, x], axis=1)

    for l in range(depth):
        last = l == depth - 1
        lam_init = 0.8 - 0.6 * math.exp(-0.3 * l)
        mod = matmul(act, mod_w[l], tm=8) + mod_b[l]
        mod_lat = mod[:b].reshape(b, N_MOD, d)
        mod_ctx = jnp.broadcast_to(mod[b].reshape(1, N_MOD, d), (b, N_MOD, d))
        mods = jnp.stack([mod_ctx, mod_lat], axis=2)
        seg = lambda m: jnp.concatenate([jnp.broadcast_to(m[:, 0:1], (b, lc, d)),
                                         jnp.broadcast_to(m[:, 1:2], (b, n, d))], axis=1)

        h = normmod(xs, norm1_g[l], mods[:, 0], mods[:, 1], lc)
        w_in_p = jnp.concatenate([w_in[l][:, :Z_KR + MLA_ROPE],
                                  jnp.zeros((d, KROPE_PAD - MLA_ROPE), F32),
                                  w_in[l][:, Z_KR + MLA_ROPE:]], axis=1).astype(BF16)
        z = _mm3(h, w_in_p)
        ya = _s5_mixer(z[..., Z_U:Z_CQ], lc,
                       s5_matrices(s5_a_re[l], s5_a_im[l], s5_log_step[l], s5_b_re[l], s5_b_im[l],
                                   s5_c_re[l], s5_c_im[l]), s5_d[l], s5_w_glu[l])
        yb = _mla_mixer(z[..., Z_CQ:Z_CKV], z[..., Z_CKV:Z_KR], z[..., Z_KR:Z_KR + MLA_ROPE], lc,
                        mla_q_a_norm[l], mla_kv_a_norm[l], mla_w_uq[l], mla_w_ukv[l],
                        mla_q_norm[l], mla_k_norm[l], cos_m, sin_m)
        yc = _rwkv_mixer(z[..., Z_RW:Z_DQ], lc, rw_mu[l], rw_w0[l], rw_w2[l], rw_a0[l], rw_a2[l],
                         rw_g2[l], rw_k_k[l], rw_k_a[l], rw_r_k[l], rw_ln_g[l], rw_ln_b[l])
        yd = _diff_mixer(z[..., Z_DQ:Z_DK], z[..., Z_DK:Z_DV], z[..., Z_DV:Z_WIDTH], lc,
                         diff_q_norm[l], diff_k_norm[l], diff_lambda[l], diff_sub_norm[l], lam_init,
                         cos_d, sin_d)
        mix = jnp.concatenate([ya, yb, yc, yd], axis=-1).astype(BF16)
        xs = xs + seg(mods[:, 2]) * _mm3(mix, w_out[l].astype(BF16))

        h2 = normmod(xs, norm2_g[l], mods[:, 3], mods[:, 4], lc)
        wr = jnp.concatenate([moe_wr_coarse[l], moe_wr_fine[l],
                              jnp.zeros((d, 128 - MOE_GROUPS - MOE_EXPERTS), F32)], axis=1)
        experts = (moe_w_gate[l].astype(BF16), moe_w_up[l].astype(BF16), moe_w_down[l].astype(BF16))
        if last:
            y = _hier_moe(h2[:, lc:].reshape(b * n, d), wr, moe_br_coarse[l], moe_br_fine[l], *experts)
            return xs[:, lc:] + mods[:, 5, 1:2] * y.reshape(b, n, d)
        y = _hier_moe(h2.reshape(b * (lc + n), d), wr, moe_br_coarse[l], moe_br_fine[l], *experts)
        xs = xs + seg(mods[:, 5]) * y.reshape(b, lc + n, d)
    return xs[:, lc:]
```

```python
import functools
import math

import jax
import jax.numpy as jnp
from jax import lax
from jax.experimental import pallas as pl
from jax.experimental.pallas import tpu as pltpu

F32 = jnp.float32
BF16 = jnp.bfloat16

D_MODEL = 4096
N_MOD = 6
GROUP_W = D_MODEL // 4
GRID_W = 64
ROPE_BASE = 10000.0
NORM_EPS = 1e-6

S5_GROUP_CH = 16
S5_GROUPS = GROUP_W // S5_GROUP_CH
S5_STATE = 64

MLA_HEADS = 8
MLA_NOPE = 128
MLA_ROPE = 64
MLA_V = GROUP_W // MLA_HEADS
MLA_Q_RANK = 512
MLA_KV_RANK = 256

RW_HEAD = 64
RW_HEADS = GROUP_W // RW_HEAD
RW_W_RANK = 64
RW_A_RANK = 64
RW_G_RANK = 128
RW_GN_EPS = 64e-5
RW_IN = 3 * GROUP_W + RW_W_RANK + RW_A_RANK + RW_G_RANK

DIFF_HEAD = 64
DIFF_HEADS = GROUP_W // (2 * DIFF_HEAD)

MOE_GROUPS = 4
MOE_PER_GROUP = 8
MOE_EXPERTS = MOE_GROUPS * MOE_PER_GROUP
MOE_TOP_K = 2
MOE_BLOCK = 128

ZA_CQ = GROUP_W
ZA_CKV = ZA_CQ + MLA_Q_RANK
ZA_KR = ZA_CKV + MLA_KV_RANK
ZA_SPLIT = ZA_KR + MLA_ROPE
ZA_WIDTH = ZA_KR + 128
ZB_DQ = RW_IN
ZB_DK = ZB_DQ + GROUP_W
ZB_DV = ZB_DK + GROUP_W
ZB_WIDTH = ZB_DV + GROUP_W

VMEM_LIMIT = 56 * 1024 * 1024


def _cparams(sem):
    return pltpu.CompilerParams(dimension_semantics=sem, vmem_limit_bytes=VMEM_LIMIT)


def _pick(n, prefs):
    for p in prefs:
        if n % p == 0:
            return p
    return n


def _mm_kernel(a_ref, b_ref, o_ref):
    a = a_ref[...].astype(BF16)
    b = b_ref[...].astype(BF16)
    o_ref[...] = jnp.dot(a, b, preferred_element_type=F32).astype(o_ref.dtype)


def matmul(a, b, out_dtype=F32, tm=None, tn=None, layer=None, n_cols=None):
    m, k = a.shape
    n = n_cols or b.shape[-1]
    tm = tm or _pick(m, (512, 256, 128))
    tn = tn or _pick(n, (1024, 768, 512, 256, 128))
    if layer is None:
        b_spec = pl.BlockSpec((k, tn), lambda i, j: (0, j))
    else:
        b_spec = pl.BlockSpec((None, k, tn), lambda i, j: (layer, 0, j))
    return pl.pallas_call(
        _mm_kernel,
        grid=(m // tm, n // tn),
        in_specs=[pl.BlockSpec((tm, k), lambda i, j: (i, 0)), b_spec],
        out_specs=pl.BlockSpec((tm, tn), lambda i, j: (i, j)),
        out_shape=jax.ShapeDtypeStruct((m, n), out_dtype),
        compiler_params=_cparams(("parallel", "parallel")),
        name="matmul",
    )(a, b)


def mod_matmul(act, mod_w, tn=1024):
    m, k = act.shape
    depth, _, n = mod_w.shape
    return pl.pallas_call(
        _mm_kernel,
        grid=(depth, n // tn),
        in_specs=[pl.BlockSpec((m, k), lambda l, j: (0, 0)),
                  pl.BlockSpec((None, k, tn), lambda l, j: (l, 0, j))],
        out_specs=pl.BlockSpec((None, m, tn), lambda l, j: (l, 0, j)),
        out_shape=jax.ShapeDtypeStruct((depth, m, n), F32),
        compiler_params=_cparams(("parallel", "parallel")),
        name="mod_matmul",
    )(act, mod_w)


def _normmod_kernel(x_ref, g_ref, sh_ref, sc_ref, o_ref):
    x = x_ref[0]
    ms = jnp.mean(x * x, axis=-1, keepdims=True)
    y = x * lax.rsqrt(ms + NORM_EPS) * g_ref[...]
    o_ref[0] = (y * (1.0 + sc_ref[0]) + sh_ref[0]).astype(o_ref.dtype)


def normmod(xs, g, shift, scale, lc, out_dtype=BF16):
    b, n, d = xs.shape
    tb = lc
    sh = shift.reshape(b * 2, 1, d)
    sc = scale.reshape(b * 2, 1, d)
    seg = lambda bi, i: (bi * 2 + jnp.minimum(i, 1), 0, 0)
    return pl.pallas_call(
        _normmod_kernel,
        grid=(b, n // tb),
        in_specs=[pl.BlockSpec((1, tb, d), lambda bi, i: (bi, i, 0)),
                  pl.BlockSpec((1, d), lambda bi, i: (0, 0)),
                  pl.BlockSpec((1, 1, d), seg),
                  pl.BlockSpec((1, 1, d), seg)],
        out_specs=pl.BlockSpec((1, tb, d), lambda bi, i: (bi, i, 0)),
        out_shape=jax.ShapeDtypeStruct((b, n, d), out_dtype),
        compiler_params=_cparams(("parallel", "parallel")),
        name="normmod",
    )(xs, g.reshape(1, d), sh, sc)


def _out_proj_kernel(ya_ref, yb_ref, yc_ref, yd_ref, w_ref, x_ref, g_ref, o_ref, *, lc, blocks_per_seq):
    acc = jnp.dot(ya_ref[...], w_ref[0:GROUP_W], preferred_element_type=F32)
    acc += jnp.dot(yb_ref[...], w_ref[GROUP_W:2 * GROUP_W], preferred_element_type=F32)
    acc += jnp.dot(yc_ref[...], w_ref[2 * GROUP_W:3 * GROUP_W], preferred_element_type=F32)
    acc += jnp.dot(yd_ref[...], w_ref[3 * GROUP_W:4 * GROUP_W], preferred_element_type=F32)
    tm = acc.shape[0]
    tok = lax.broadcasted_iota(jnp.int32, (tm, 1), 0) + (pl.program_id(0) % blocks_per_seq) * tm
    gate = jnp.where(tok < lc, g_ref[0:1], g_ref[1:2])
    o_ref[...] = x_ref[...] + gate * acc


def out_proj(groups, w, xs, gate, lc):
    b, n, d = xs.shape
    tm = _pick(n, (768, 384, 256, 128))
    tn = 1024
    bps = n // tm
    row = pl.BlockSpec((tm, GROUP_W), lambda i, j: (i, 0))
    kern = functools.partial(_out_proj_kernel, lc=lc, blocks_per_seq=bps)
    out = pl.pallas_call(
        kern,
        grid=(b * bps, d // tn),
        in_specs=[row, row, row, row,
                  pl.BlockSpec((4 * GROUP_W, tn), lambda i, j: (0, j)),
                  pl.BlockSpec((tm, tn), lambda i, j: (i, j)),
                  pl.BlockSpec((None, 2, tn), lambda i, j: (i // bps, 0, j))],
        out_specs=pl.BlockSpec((tm, tn), lambda i, j: (i, j)),
        out_shape=jax.ShapeDtypeStruct((b * n, d), F32),
        compiler_params=_cparams(("parallel", "parallel")),
        name="out_proj",
    )(*groups, w, xs.reshape(b * n, d), gate)
    return out.reshape(b, n, d)


def _mla_attn_kernel(q_ref, k_ref, v_ref, o_ref, *, lc, scale):
    q = q_ref[0, 0]

    def attend(nk):
        k = k_ref[0, 0, :nk]
        v = v_ref[0, 0, :nk]
        s = lax.dot_general(q, k, (((1,), (1,)), ((), ())), preferred_element_type=F32) * scale
        m = jnp.max(s, axis=-1, keepdims=True)
        p = jnp.exp(s - m)
        l = jnp.sum(p, axis=-1, keepdims=True)
        o = jnp.dot(p.astype(BF16), v, preferred_element_type=F32)
        o_ref[0] = (o / l).astype(o_ref.dtype)

    i = pl.program_id(2)

    @pl.when(i == 0)
    def _():
        attend(lc)

    @pl.when(i > 0)
    def _():
        attend(k_ref.shape[2])


def mla_attention(q, k, v, lc):
    b, h, n, dq = q.shape
    dv = v.shape[-1]
    tq = lc
    kern = functools.partial(_mla_attn_kernel, lc=lc, scale=float(dq) ** -0.5)
    return pl.pallas_call(
        kern,
        grid=(b, h, n // tq),
        in_specs=[pl.BlockSpec((1, 1, tq, dq), lambda bi, hi, i: (bi, hi, i, 0)),
                  pl.BlockSpec((1, 1, n, dq), lambda bi, hi, i: (bi, hi, 0, 0)),
                  pl.BlockSpec((1, 1, n, dv), lambda bi, hi, i: (bi, hi, 0, 0))],
        out_specs=pl.BlockSpec((1, tq, dv), lambda bi, hi, i: (bi, i, hi)),
        out_shape=jax.ShapeDtypeStruct((b, n, h * dv), BF16),
        compiler_params=_cparams(("parallel", "parallel", "arbitrary")),
        name="mla_attention",
    )(q, k, v)


def _diff_attn_kernel(q0_ref, q1_ref, k0_ref, k1_ref, v_ref, lam_ref, g_ref, o_ref, *, lc, scale, post):
    q0 = q0_ref[0, 0]
    q1 = q1_ref[0, 0]
    lam = lam_ref[...]

    def probs(q, k):
        s = lax.dot_general(q, k, (((1,), (1,)), ((), ())), preferred_element_type=F32) * scale
        m = jnp.max(s, axis=-1, keepdims=True)
        p = jnp.exp(s - m)
        return p, jnp.sum(p, axis=-1, keepdims=True)

    def attend(nk):
        p0, l0 = probs(q0, k0_ref[0, 0, :nk])
        p1, l1 = probs(q1, k1_ref[0, 0, :nk])
        attn = p0 * (1.0 / l0) - p1 * (lam[:, :1] / l1)
        o = jnp.dot(attn.astype(BF16), v_ref[0, 0, :nk], preferred_element_type=F32)
        ms = jnp.mean(o * o, axis=-1, keepdims=True)
        o_ref[0] = (o * lax.rsqrt(ms + NORM_EPS) * g_ref[...] * post).astype(o_ref.dtype)

    i = pl.program_id(2)

    @pl.when(i == 0)
    def _():
        attend(lc)

    @pl.when(i > 0)
    def _():
        attend(v_ref.shape[2])


def diff_attention(q0, q1, k0, k1, v, lam, sub_norm, lc, post):
    b, h, n, dh = q0.shape
    dv = v.shape[-1]
    tq = lc
    kern = functools.partial(_diff_attn_kernel, lc=lc, scale=float(dh) ** -0.5, post=post)
    qspec = pl.BlockSpec((1, 1, tq, dh), lambda bi, hi, i: (bi, hi, i, 0))
    kspec = pl.BlockSpec((1, 1, n, dh), lambda bi, hi, i: (bi, hi, 0, 0))
    return pl.pallas_call(
        kern,
        grid=(b, h, n // tq),
        in_specs=[qspec, qspec, kspec, kspec,
                  pl.BlockSpec((1, 1, n, dv), lambda bi, hi, i: (bi, hi, 0, 0)),
                  pl.BlockSpec((1, dv), lambda bi, hi, i: (0, 0)),
                  pl.BlockSpec((1, dv), lambda bi, hi, i: (0, 0))],
        out_specs=pl.BlockSpec((1, tq, dv), lambda bi, hi, i: (bi, i, hi)),
        out_shape=jax.ShapeDtypeStruct((b, n, h * dv), BF16),
        compiler_params=_cparams(("parallel", "parallel", "arbitrary")),
        name="diff_attention",
    )(q0, q1, k0, k1, v, jnp.broadcast_to(lam.reshape(1, 1), (1, dv)).astype(F32),
      sub_norm.reshape(1, dv))


S5_TB = 64
S5_GPT = 128 // S5_GROUP_CH
S5_TILES_IN = GROUP_W // 128
S5_TILES_ST = S5_GROUPS * S5_STATE // 128
S5_ST_PER_IN = S5_TILES_ST // S5_TILES_IN


def _s5_kernel(uf_ref, ub_ref, wd_ref, wr_ref, lam_ref, yf_ref, yb_ref, bu_re, bu_im, st_re, st_im):
    nb, tb, _ = uf_ref.shape
    rows = nb * tb
    half = S5_ST_PER_IN * 128

    @pl.when(pl.program_id(0) == 0)
    def _():
        st_re[...] = jnp.zeros_like(st_re)
        st_im[...] = jnp.zeros_like(st_im)

    for d, u_ref in enumerate((uf_ref, ub_ref)):
        x = u_ref[...].reshape(rows, GROUP_W).astype(BF16)
        for jt in range(S5_TILES_IN):
            bu = jnp.dot(x[:, jt * 128:(jt + 1) * 128], wd_ref[d, jt], preferred_element_type=F32)
            for c in range(S5_ST_PER_IN):
                bu_re[jt * S5_ST_PER_IN + c, d * rows:(d + 1) * rows, :] = bu[:, c * 128:(c + 1) * 128]
                bu_im[jt * S5_ST_PER_IN + c, d * rows:(d + 1) * rows, :] = bu[:, half + c * 128:half + (c + 1) * 128]

    fwd = lax.broadcasted_iota(jnp.int32, (2 * nb, 128), 0) < nb

    def step(t, carry):
        tr = tb - 1 - t
        at_t = pl.ds(t, 2 * nb, stride=tb)
        at_tr = pl.ds(tr, 2 * nb, stride=tb)
        for c in range(S5_TILES_ST):
            r_t, r_tr = bu_re[c, at_t, :], bu_re[c, at_tr, :]
            i_t, i_tr = bu_im[c, at_t, :], bu_im[c, at_tr, :]
            hr, hi = st_re[c], st_im[c]
            ar, ai = lam_ref[0, c], lam_ref[1, c]
            nr = ar * hr - ai * hi + jnp.where(fwd, r_t, r_tr)
            ni = ar * hi + ai * hr + jnp.where(fwd, i_t, i_tr)
            st_re[c] = nr
            st_im[c] = ni
            bu_re[c, at_t, :] = jnp.where(fwd, nr, r_t)
            bu_re[c, at_tr, :] = jnp.where(fwd, r_tr, nr)
            bu_im[c, at_t, :] = jnp.where(fwd, ni, i_t)
            bu_im[c, at_tr, :] = jnp.where(fwd, i_tr, ni)
        return carry

    lax.fori_loop(0, tb, step, 0)

    for d, y_ref in enumerate((yf_ref, yb_ref)):
        for jt in range(S5_TILES_IN):
            tiles = [bu_re[jt * S5_ST_PER_IN + c, d * rows:(d + 1) * rows, :] for c in range(S5_ST_PER_IN)]
            tiles += [bu_im[jt * S5_ST_PER_IN + c, d * rows:(d + 1) * rows, :] for c in range(S5_ST_PER_IN)]
            h = jnp.concatenate(tiles, axis=1).astype(BF16)
            y = jnp.dot(h, wr_ref[d, jt], preferred_element_type=F32)
            y_ref[:, :, jt * 128:(jt + 1) * 128] = y.reshape(nb, tb, 128)


def _rev_block(i, nlc, ntot):
    return jnp.where(i < nlc, nlc - 1 - i, ntot - 1 - (i - nlc))


def s5_scan(z, lc, wd, wr, lam):
    b, n, _ = z.shape
    tb = min(S5_TB, lc)
    nlc, ntot = lc // tb, n // tb
    blk = (b, tb, GROUP_W)
    full = lambda a: pl.BlockSpec(a.shape, lambda i: (0,) * a.ndim)
    return pl.pallas_call(
        _s5_kernel,
        grid=(ntot,),
        in_specs=[pl.BlockSpec(blk, lambda i: (0, i, 0)),
                  pl.BlockSpec(blk, lambda i: (0, _rev_block(i, nlc, ntot), 0)),
                  full(wd), full(wr), full(lam)],
        out_specs=[pl.BlockSpec(blk, lambda i: (0, i, 0)),
                   pl.BlockSpec(blk, lambda i: (0, _rev_block(i, nlc, ntot), 0))],
        out_shape=[jax.ShapeDtypeStruct((b, n, GROUP_W), F32)] * 2,
        scratch_shapes=[pltpu.VMEM((S5_TILES_ST, 2 * b * tb, 128), F32)] * 2
        + [pltpu.VMEM((S5_TILES_ST, 2 * b, 128), F32)] * 2,
        compiler_params=_cparams(("arbitrary",)),
        name="s5_scan",
    )(z, z, wd, wr, lam)


def _cmul(ar, ai, br, bi):
    return ar * br - ai * bi, ar * bi + ai * br


def s5_operators(a_re, a_im, log_step, b_re, b_im, c_re, c_im, nb):
    step = jnp.exp(log_step)[..., None]
    mag = jnp.exp(a_re * step)
    lr, li = mag * jnp.cos(a_im * step), mag * jnp.sin(a_im * step)
    den = a_re * a_re + a_im * a_im
    nr, ni = lr - 1.0, li
    fr = (nr * a_re + ni * a_im) / den
    fi = (ni * a_re - nr * a_im) / den
    bbr, bbi = _cmul(fr[..., None], fi[..., None], b_re, b_im)
    eye = jnp.eye(S5_GPT, dtype=F32)

    def drive(w):
        w = w.reshape(2, S5_TILES_IN, S5_GPT, S5_STATE, S5_GROUP_CH)
        w = jnp.einsum('dtgph,gk->dtghkp', w, eye)
        return w.reshape(2, S5_TILES_IN, 128, S5_GPT * S5_STATE)

    def readout(w):
        w = w.reshape(2, S5_TILES_IN, S5_GPT, S5_GROUP_CH, S5_STATE)
        w = jnp.einsum('dtghp,gk->dtgpkh', w, eye)
        return w.reshape(2, S5_TILES_IN, S5_GPT * S5_STATE, 128)

    wd = jnp.concatenate([drive(bbr), drive(bbi)], axis=-1).astype(BF16)
    wr = jnp.concatenate([readout(c_re), readout(-c_im)], axis=-2).astype(BF16)

    def rows(v):
        v = jnp.broadcast_to(v.reshape(2, 1, S5_TILES_ST, 128), (2, nb, S5_TILES_ST, 128))
        return v.reshape(2 * nb, S5_TILES_ST, 128).transpose(1, 0, 2)

    return wd, wr, jnp.stack([rows(lr), rows(li)])


def _s5_post_kernel(yf_ref, yb_ref, u_ref, d_ref, w_ref, o_ref):
    y = jax.nn.gelu(yf_ref[...] + yb_ref[...] + d_ref[...] * u_ref[...])
    gv = jnp.dot(y.astype(BF16), w_ref[...], preferred_element_type=F32)
    o_ref[...] = (gv[:, :GROUP_W] * jax.nn.sigmoid(gv[:, GROUP_W:])).astype(o_ref.dtype)


def s5_post(yf, yb, z, d_skip, w_glu):
    t = yf.shape[0]
    tm = _pick(t, (512, 256, 128))
    row = pl.BlockSpec((tm, GROUP_W), lambda i: (i, 0))
    return pl.pallas_call(
        _s5_post_kernel,
        grid=(t // tm,),
        in_specs=[row, row, row, pl.BlockSpec((1, GROUP_W), lambda i: (0, 0)),
                  pl.BlockSpec((GROUP_W, 2 * GROUP_W), lambda i: (0, 0))],
        out_specs=row,
        out_shape=jax.ShapeDtypeStruct((t, GROUP_W), BF16),
        compiler_params=_cparams(("parallel",)),
        name="s5_post",
    )(yf, yb, z, d_skip.reshape(1, GROUP_W), w_glu)


def _rwkv_kernel(w_ref, kk_ref, b_ref, k_ref, r_ref, v_ref, y_ref, s_ref):
    @pl.when(pl.program_id(0) == 0)
    def _():
        s_ref[...] = jnp.zeros_like(s_ref)

    def step(t, carry):
        def vgroup(vg, carry2):
            off = pl.multiple_of(vg * 8, 8)
            vrows = v_ref[t, pl.ds(off, 8), :]
            outs = []
            for j in range(8):
                sv = s_ref[off + j]
                sa = jnp.sum(sv * kk_ref[t], axis=0, keepdims=True)
                sn = sv * w_ref[t] - sa * b_ref[t] + vrows[j:j + 1] * k_ref[t]
                s_ref[off + j] = sn
                outs.append(jnp.sum(sn * r_ref[t], axis=0, keepdims=True))
            y_ref[t, pl.ds(off, 8), :] = jnp.concatenate(outs, axis=0)
            return carry2

        return lax.fori_loop(0, RW_HEAD // 8, vgroup, carry)

    lax.fori_loop(0, w_ref.shape[0], step, 0)


def rwkv_scan(w, kk, bv, k, r, v, tb=32):
    t, _, chains = w.shape
    spec = pl.BlockSpec((tb, RW_HEAD, chains), lambda i: (i, 0, 0))
    return pl.pallas_call(
        _rwkv_kernel,
        grid=(t // tb,),
        in_specs=[spec] * 6,
        out_specs=spec,
        out_shape=jax.ShapeDtypeStruct((t, RW_HEAD, chains), F32),
        scratch_shapes=[pltpu.VMEM((RW_HEAD, RW_HEAD, chains), F32)],
        compiler_params=_cparams(("arbitrary",)),
        name="rwkv_scan",
    )(w, kk, bv, k, r, v)


def _moe_kernel(be_ref, nu_ref, x_ref, wg_ref, wu_ref, wd_ref, o_ref, g_s, u_s, d_s):
    i = pl.program_id(0)
    used = i < nu_ref[0]
    changed = jnp.logical_or(i == 0, be_ref[i] != be_ref[jnp.maximum(i - 1, 0)])

    @pl.when(jnp.logical_and(used, changed))
    def _():
        g_s[...] = wg_ref[...].astype(BF16)
        u_s[...] = wu_ref[...].astype(BF16)
        d_s[...] = wd_ref[...].astype(BF16)

    @pl.when(used)
    def _():
        x = x_ref[...]
        g = jnp.dot(x, g_s[...], preferred_element_type=F32)
        u = jnp.dot(x, u_s[...], preferred_element_type=F32)
        h = (g * jax.nn.sigmoid(g) * u).astype(BF16)
        o_ref[...] = jnp.dot(h, d_s[...], preferred_element_type=F32)

    @pl.when(jnp.logical_not(used))
    def _():
        o_ref[...] = jnp.zeros_like(o_ref)


def moe_ffn(block_exp, n_used, xb, wg, wu, wd, layer):
    cap, d = xb.shape
    hid = wg.shape[-1]
    nb = cap // MOE_BLOCK
    once = pl.Buffered(1)
    grid_spec = pltpu.PrefetchScalarGridSpec(
        num_scalar_prefetch=2,
        grid=(nb,),
        in_specs=[pl.BlockSpec((MOE_BLOCK, d), lambda i, be, nu: (i, 0)),
                  pl.BlockSpec((None, None, d, hid), lambda i, be, nu: (layer, be[i], 0, 0), pipeline_mode=once),
                  pl.BlockSpec((None, None, d, hid), lambda i, be, nu: (layer, be[i], 0, 0), pipeline_mode=once),
                  pl.BlockSpec((None, None, hid, d), lambda i, be, nu: (layer, be[i], 0, 0), pipeline_mode=once)],
        out_specs=pl.BlockSpec((MOE_BLOCK, d), lambda i, be, nu: (i, 0)),
        scratch_shapes=[pltpu.VMEM((d, hid), BF16), pltpu.VMEM((d, hid), BF16), pltpu.VMEM((hid, d), BF16)],
    )
    return pl.pallas_call(
        _moe_kernel,
        grid_spec=grid_spec,
        out_shape=jax.ShapeDtypeStruct((cap, d), F32),
        compiler_params=_cparams(("arbitrary",)),
        name="moe_ffn",
    )(block_exp, n_used, xb, wg, wu, wd)


def _rms(x, g):
    return x * lax.rsqrt(jnp.mean(x * x, axis=-1, keepdims=True) + NORM_EPS) * g


def _axial_rope(rows, dim):
    n_freq = dim // 4
    inv = ROPE_BASE ** (-jnp.arange(n_freq, dtype=F32) / n_freq)
    row = jnp.repeat(jnp.arange(rows, dtype=F32), GRID_W)
    col = jnp.tile(jnp.arange(GRID_W, dtype=F32), rows)
    ang = jnp.concatenate([row[:, None] * inv, col[:, None] * inv], axis=-1)
    return jnp.cos(ang), jnp.sin(ang)


def _rope(x, cos, sin):
    half = x.shape[-1] // 2
    x1, x2 = x[..., :half], x[..., half:]
    return jnp.concatenate([x1 * cos - x2 * sin, x1 * sin + x2 * cos], axis=-1)


def _rope_lat(x, cos, sin, lc):
    extra = (None,) * (x.ndim - 3)
    cs = cos[(slice(None),) + extra]
    sn = sin[(slice(None),) + extra]
    return jnp.concatenate([x[:, :lc], _rope(x[:, lc:], cs, sn)], axis=1)


def _seq_rev(t, lc):
    return jnp.concatenate([jnp.flip(t[:, :lc], 1), jnp.flip(t[:, lc:], 1)], axis=1)


def _mm3(x, w, **kw):
    b, n, k = x.shape
    return matmul(x.reshape(b * n, k), w, **kw).reshape(b, n, -1)


def _mla_mixer(cq, ckv, krope, lc, q_a_norm, kv_a_norm, w_uq, w_ukv, q_norm, k_norm, cos, sin):
    b, n, _ = cq.shape
    q = _mm3(_rms(cq, q_a_norm), w_uq).reshape(b, n, MLA_HEADS, MLA_NOPE + MLA_ROPE)
    kv = _mm3(_rms(ckv, kv_a_norm), w_ukv).reshape(b, n, MLA_HEADS, MLA_NOPE + MLA_V)
    q_nope = _rms(q[..., :MLA_NOPE], q_norm[:MLA_NOPE])
    q_rope = _rope_lat(_rms(q[..., MLA_NOPE:], q_norm[MLA_NOPE:]), cos, sin, lc)
    k_nope = _rms(kv[..., :MLA_NOPE], k_norm[:MLA_NOPE])
    k_rope = _rope_lat(_rms(krope, k_norm[MLA_NOPE:]), cos, sin, lc)
    qf = jnp.concatenate([q_nope, q_rope], axis=-1)
    kf = jnp.concatenate([k_nope, jnp.broadcast_to(k_rope[:, :, None], (b, n, MLA_HEADS, MLA_ROPE))], axis=-1)
    t = lambda x: x.transpose(0, 2, 1, 3).astype(BF16)
    return mla_attention(t(qf), t(kf), t(kv[..., MLA_NOPE:]), lc)


def _diff_mixer(dq, dk, dv, lc, q_norm, k_norm, lam_vecs, sub_norm, lam_init, cos, sin):
    b, n, _ = dq.shape
    q = _rope_lat(_rms(dq.reshape(b, n, DIFF_HEADS, 2, DIFF_HEAD), q_norm), cos, sin, lc)
    k = _rope_lat(_rms(dk.reshape(b, n, DIFF_HEADS, 2, DIFF_HEAD), k_norm), cos, sin, lc)
    lam = jnp.exp(jnp.sum(lam_vecs[0] * lam_vecs[1])) - jnp.exp(jnp.sum(lam_vecs[2] * lam_vecs[3])) + lam_init
    t = lambda x: x.transpose(0, 2, 1, 3).astype(BF16)
    return diff_attention(t(q[:, :, :, 0]), t(q[:, :, :, 1]), t(k[:, :, :, 0]), t(k[:, :, :, 1]),
                          t(dv.reshape(b, n, DIFF_HEADS, 2 * DIFF_HEAD)), lam, sub_norm, lc,
                          1.0 - lam_init)


def _centred_shift(z, mu_prev, mu_next):
    z_prev = jnp.pad(z, ((0, 0), (1, 0), (0, 0)))[:, :-1]
    z_next = jnp.pad(z, ((0, 0), (0, 1), (0, 0)))[:, 1:]
    return z + mu_prev * (z_prev - z) + mu_next * (z_next - z)


def _rwkv_mixer(z, lc, mu, w0, w2, a0, a2, g2, k_k, k_a, r_k, ln_g, ln_b):
    b, n, _ = z.shape
    heads = lambda t: t.reshape(b, n, RW_HEADS, RW_HEAD)
    z = jnp.concatenate([_centred_shift(z[:, :lc], mu[0], mu[1]),
                         _centred_shift(z[:, lc:], mu[0], mu[1])], axis=1)
    r, k, v = z[..., :GROUP_W], z[..., GROUP_W:2 * GROUP_W], z[..., 2 * GROUP_W:3 * GROUP_W]
    o = 3 * GROUP_W
    wl, al, gl = z[..., o:o + 64], z[..., o + 64:o + 128], z[..., o + 128:o + 256]
    kk = heads(k * k_k)
    kk = kk * lax.rsqrt(jnp.maximum(jnp.sum(kk * kk, axis=-1, keepdims=True), 1e-12))
    lr_in = jnp.concatenate([jnp.tanh(wl), al, jax.nn.sigmoid(gl)], axis=-1)
    zw = jnp.zeros((RW_W_RANK, GROUP_W), F32)
    zg = jnp.zeros((RW_G_RANK, GROUP_W), F32)
    w_lr = jnp.concatenate([
        jnp.concatenate([w2[0], w2[1], zw, zw, zw], axis=1),
        jnp.concatenate([zw, zw, a2[0], a2[1], zw], axis=1),
        jnp.concatenate([zg, zg, zg, zg, g2], axis=1)], axis=0)
    lr = _mm3(lr_in, w_lr)
    g = lr[..., 4 * GROUP_W:]
    ka = k_a.reshape(RW_HEADS, RW_HEAD)
    rh, vh, kh = heads(r), heads(v), heads(k)
    per_dir = []
    for dr in range(2):
        w = -jax.nn.softplus(-heads(w0[dr] + lr[..., dr * GROUP_W:(dr + 1) * GROUP_W])) - 0.5
        a = jax.nn.sigmoid(heads(a0[dr] + lr[..., (2 + dr) * GROUP_W:(3 + dr) * GROUP_W]))
        per_dir.append((jnp.exp(-jnp.exp(w)), kh * (1 + (a - 1) * ka), a))

    def lanes(fwd, bwd):
        both = jnp.stack([fwd, _seq_rev(bwd, lc)])
        return both.transpose(2, 4, 0, 1, 3).reshape(n, RW_HEAD, 2 * b * RW_HEADS)

    y = rwkv_scan(lanes(per_dir[0][0], per_dir[1][0]), lanes(kk, kk),
                  lanes(kk * per_dir[0][2], kk * per_dir[1][2]),
                  lanes(per_dir[0][1], per_dir[1][1]), lanes(rh, rh), lanes(vh, vh))
    y = y.reshape(n, RW_HEAD, 2, b, RW_HEADS).transpose(2, 3, 0, 4, 1)
    o_f, o_b = y[0], _seq_rev(y[1], lc)
    bonus = (jnp.sum(rh * per_dir[0][1] * r_k, axis=-1, keepdims=True)
             + jnp.sum(rh * per_dir[1][1] * r_k, axis=-1, keepdims=True)) * vh
    o = o_f + o_b + bonus
    mean = jnp.mean(o, axis=-1, keepdims=True)
    var = jnp.mean(jnp.square(o - mean), axis=-1, keepdims=True)
    o = ((o - mean) * lax.rsqrt(var + RW_GN_EPS)).reshape(b, n, GROUP_W)
    return ((o * ln_g + ln_b) * g).astype(BF16)


def _hier_moe(h, wr, br_coarse, br_fine, wg, wu, wd, layer):
    t, d = h.shape
    logits = matmul(h, wr)
    p_group = jax.nn.softmax(logits[:, :MOE_GROUPS] + br_coarse, axis=-1)
    group = jnp.argmax(p_group, axis=-1).astype(jnp.int32)
    p_sel = jnp.take_along_axis(p_group, group[:, None], axis=-1)
    fine = (logits[:, MOE_GROUPS:MOE_GROUPS + MOE_EXPERTS] + br_fine).reshape(t, MOE_GROUPS, MOE_PER_GROUP)
    fine = jnp.take_along_axis(fine, group[:, None, None], axis=1)[:, 0]
    top_v, top_i = lax.top_k(fine, MOE_TOP_K)
    weight = p_sel * jax.nn.softmax(top_v, axis=-1)
    expert = group[:, None] * MOE_PER_GROUP + top_i.astype(jnp.int32)
    flat_e = expert.reshape(-1)
    n_assign = flat_e.shape[0]
    order = jnp.argsort(flat_e)
    sorted_e = flat_e[order]
    counts = jnp.bincount(flat_e, length=MOE_EXPERTS)
    padded = (counts + MOE_BLOCK - 1) // MOE_BLOCK * MOE_BLOCK
    pad_end = jnp.cumsum(padded)
    pad_start = pad_end - padded
    seg_start = jnp.cumsum(counts) - counts
    dest = (pad_start[sorted_e] + jnp.arange(n_assign) - seg_start[sorted_e]).astype(jnp.int32)
    n_blocks = -(-(n_assign + MOE_EXPERTS * (MOE_BLOCK - 1)) // MOE_BLOCK)
    cap = n_blocks * MOE_BLOCK
    slot_tok = jnp.full((cap,), t, jnp.int32).at[dest].set((order // MOE_TOP_K).astype(jnp.int32))
    block_exp = jnp.minimum(jnp.searchsorted(pad_end, jnp.arange(n_blocks) * MOE_BLOCK, side='right'),
                            MOE_EXPERTS - 1).astype(jnp.int32)
    n_used = (pad_end[-1:] // MOE_BLOCK).astype(jnp.int32)
    h_pad = jnp.concatenate([h, jnp.zeros((1, d), h.dtype)], axis=0)
    yb = moe_ffn(block_exp, n_used, h_pad[slot_tok], wg, wu, wd, layer)
    slot_of = jnp.zeros((n_assign,), jnp.int32).at[order].set(dest).reshape(t, MOE_TOP_K)
    return weight[:, 0:1] * yb[slot_of[:, 0]] + weight[:, 1:2] * yb[slot_of[:, 1]]


def kernel(x, c, ctx, c_ctx, mod_w, mod_b, norm1_g, norm2_g, w_in, w_out, s5_a_re, s5_a_im, s5_log_step, s5_b_re, s5_b_im, s5_c_re, s5_c_im, s5_d, s5_w_glu, mla_q_a_norm, mla_kv_a_norm, mla_w_uq, mla_w_ukv, mla_q_norm, mla_k_norm, rw_mu, rw_w0, rw_w2, rw_a0, rw_a2, rw_g2, rw_k_k, rw_k_a, rw_r_k, rw_ln_g, rw_ln_b, diff_q_norm, diff_k_norm, diff_lambda, diff_sub_norm, moe_wr_coarse, moe_br_coarse, moe_wr_fine, moe_br_fine, moe_w_gate, moe_w_up, moe_w_down):
    b, n, d = x.shape
    lc = ctx.shape[1]
    ntok = lc + n
    depth = mod_w.shape[0]
    rows = n // GRID_W
    cos_m, sin_m = _axial_rope(rows, MLA_ROPE)
    cos_d, sin_d = _axial_rope(rows, DIFF_HEAD)
    act = jnp.concatenate([jax.nn.silu(c), jax.nn.silu(c_ctx)[None],
                           jnp.zeros((8 - b - 1, d), F32)], axis=0)
    mod_all = mod_matmul(act, mod_w) + mod_b[:, None, :]
    xs = jnp.concatenate([ctx, x], axis=1)

    for l in range(depth):
        last = l == depth - 1
        lam_init = 0.8 - 0.6 * math.exp(-0.3 * l)
        mod_lat = mod_all[l, :b].reshape(b, N_MOD, d)
        mod_ctx = jnp.broadcast_to(mod_all[l, b].reshape(1, N_MOD, d), (b, N_MOD, d))
        mods = jnp.stack([mod_ctx, mod_lat], axis=2)
        seg = lambda m: jnp.concatenate([jnp.broadcast_to(m[:, 0:1], (b, lc, d)),
                                         jnp.broadcast_to(m[:, 1:2], (b, n, d))], axis=1)

        h = normmod(xs, norm1_g[l], mods[:, 0], mods[:, 1], lc).reshape(b * ntok, d)
        za = matmul(h, w_in, layer=l, n_cols=ZA_WIDTH, tn=384).reshape(b, ntok, ZA_WIDTH)
        zb = matmul(h, w_in[l][:, ZA_SPLIT:].astype(BF16), tm=_pick(b * ntok, (1024, 512, 256, 128)),
                    tn=_pick(ZB_WIDTH, (1280, 640, 128))).reshape(b, ntok, ZB_WIDTH)

        wd, wr, lam = s5_operators(s5_a_re[l], s5_a_im[l], s5_log_step[l], s5_b_re[l], s5_b_im[l],
                                   s5_c_re[l], s5_c_im[l], b)
        yf, yr = s5_scan(za, lc, wd, wr, lam)
        ya = s5_post(yf.reshape(b * ntok, GROUP_W), yr.reshape(b * ntok, GROUP_W),
                     za.reshape(b * ntok, ZA_WIDTH), s5_d[l], s5_w_glu[l].astype(BF16))
        yb = _mla_mixer(za[..., ZA_CQ:ZA_CKV], za[..., ZA_CKV:ZA_KR], za[..., ZA_KR:ZA_SPLIT], lc,
                        mla_q_a_norm[l], mla_kv_a_norm[l], mla_w_uq[l], mla_w_ukv[l],
                        mla_q_norm[l], mla_k_norm[l], cos_m, sin_m)
        yc = _rwkv_mixer(zb[..., :ZB_DQ], lc, rw_mu[l], rw_w0[l], rw_w2[l], rw_a0[l], rw_a2[l],
                         rw_g2[l], rw_k_k[l], rw_k_a[l], rw_r_k[l], rw_ln_g[l], rw_ln_b[l])
        yd = _diff_mixer(zb[..., ZB_DQ:ZB_DK], zb[..., ZB_DK:ZB_DV], zb[..., ZB_DV:], lc,
                         diff_q_norm[l], diff_k_norm[l], diff_lambda[l], diff_sub_norm[l], lam_init,
                         cos_d, sin_d)
        flat = lambda y: y.reshape(b * ntok, GROUP_W)
        xs = out_proj((ya, flat(yb), flat(yc), flat(yd)), w_out[l].astype(BF16), xs, mods[:, 2], lc)

        h2 = normmod(xs, norm2_g[l], mods[:, 3], mods[:, 4], lc)
        wrt = jnp.concatenate([moe_wr_coarse[l], moe_wr_fine[l],
                               jnp.zeros((d, 128 - MOE_GROUPS - MOE_EXPERTS), F32)], axis=1)
        experts = (moe_w_gate, moe_w_up, moe_w_down, l)
        if last:
            y = _hier_moe(h2[:, lc:].reshape(b * n, d), wrt, moe_br_coarse[l], moe_br_fine[l], *experts)
            return xs[:, lc:] + mods[:, 5, 1:2] * y.reshape(b, n, d)
        y = _hier_moe(h2.reshape(b * ntok, d), wrt, moe_br_coarse[l], moe_br_fine[l], *experts)
        xs = xs + seg(mods[:, 5]) * y.reshape(b, ntok, d)
    return xs[:, lc:]
```

```python
import functools
import math

import jax
import jax.numpy as jnp
from jax import lax
from jax.experimental import pallas as pl
from jax.experimental.pallas import tpu as pltpu

F32 = jnp.float32
BF16 = jnp.bfloat16

D_MODEL = 4096
N_MOD = 6
GROUP_W = D_MODEL // 4
GRID_W = 64
ROPE_BASE = 10000.0
NORM_EPS = 1e-6

S5_GROUP_CH = 16
S5_GROUPS = GROUP_W // S5_GROUP_CH
S5_STATE = 64

MLA_HEADS = 8
MLA_NOPE = 128
MLA_ROPE = 64
MLA_V = GROUP_W // MLA_HEADS
MLA_Q_RANK = 512
MLA_KV_RANK = 256

RW_HEAD = 64
RW_HEADS = GROUP_W // RW_HEAD
RW_W_RANK = 64
RW_A_RANK = 64
RW_G_RANK = 128
RW_GN_EPS = 64e-5
RW_IN = 3 * GROUP_W + RW_W_RANK + RW_A_RANK + RW_G_RANK

DIFF_HEAD = 64
DIFF_HEADS = GROUP_W // (2 * DIFF_HEAD)

MOE_GROUPS = 4
MOE_PER_GROUP = 8
MOE_EXPERTS = MOE_GROUPS * MOE_PER_GROUP
MOE_TOP_K = 2
MOE_BLOCK = 128

ZA_CQ = GROUP_W
ZA_CKV = ZA_CQ + MLA_Q_RANK
ZA_KR = ZA_CKV + MLA_KV_RANK
ZA_SPLIT = ZA_KR + MLA_ROPE
ZA_WIDTH = ZA_KR + 128
ZB_DQ = RW_IN
ZB_DK = ZB_DQ + GROUP_W
ZB_DV = ZB_DK + GROUP_W
ZB_WIDTH = ZB_DV + GROUP_W

VMEM_LIMIT = 56 * 1024 * 1024


def _cparams(sem):
    return pltpu.CompilerParams(dimension_semantics=sem, vmem_limit_bytes=VMEM_LIMIT)


def _pick(n, prefs):
    for p in prefs:
        if n % p == 0:
            return p
    return n


def _mm_kernel(a_ref, b_ref, o_ref):
    a = a_ref[...].astype(BF16)
    b = b_ref[...].astype(BF16)
    o_ref[...] = jnp.dot(a, b, preferred_element_type=F32).astype(o_ref.dtype)


def matmul(a, b, out_dtype=F32, tm=None, tn=None, layer=None, n_cols=None):
    m, k = a.shape
    n = n_cols or b.shape[-1]
    tm = tm or _pick(m, (512, 256, 128))
    tn = tn or _pick(n, (1024, 768, 512, 256, 128))
    if layer is None:
        b_spec = pl.BlockSpec((k, tn), lambda i, j: (0, j))
    else:
        b_spec = pl.BlockSpec((None, k, tn), lambda i, j: (layer, 0, j))
    return pl.pallas_call(
        _mm_kernel,
        grid=(m // tm, n // tn),
        in_specs=[pl.BlockSpec((tm, k), lambda i, j: (i, 0)), b_spec],
        out_specs=pl.BlockSpec((tm, tn), lambda i, j: (i, j)),
        out_shape=jax.ShapeDtypeStruct((m, n), out_dtype),
        compiler_params=_cparams(("parallel", "parallel")),
        name="matmul",
    )(a, b)


def mod_matmul(act, mod_w, tn=1024):
    m, k = act.shape
    depth, _, n = mod_w.shape
    return pl.pallas_call(
        _mm_kernel,
        grid=(depth, n // tn),
        in_specs=[pl.BlockSpec((m, k), lambda l, j: (0, 0)),
                  pl.BlockSpec((None, k, tn), lambda l, j: (l, 0, j))],
        out_specs=pl.BlockSpec((None, m, tn), lambda l, j: (l, 0, j)),
        out_shape=jax.ShapeDtypeStruct((depth, m, n), F32),
        compiler_params=_cparams(("parallel", "parallel")),
        name="mod_matmul",
    )(act, mod_w)


def _normmod_kernel(x_ref, g_ref, sh_ref, sc_ref, o_ref):
    x = x_ref[0]
    ms = jnp.mean(x * x, axis=-1, keepdims=True)
    y = x * lax.rsqrt(ms + NORM_EPS) * g_ref[...]
    o_ref[0] = (y * (1.0 + sc_ref[0]) + sh_ref[0]).astype(o_ref.dtype)


def normmod(xs, g, shift, scale, lc, out_dtype=BF16):
    b, n, d = xs.shape
    tb = lc
    sh = shift.reshape(b * 2, 1, d)
    sc = scale.reshape(b * 2, 1, d)
    seg = lambda bi, i: (bi * 2 + jnp.minimum(i, 1), 0, 0)
    return pl.pallas_call(
        _normmod_kernel,
        grid=(b, n // tb),
        in_specs=[pl.BlockSpec((1, tb, d), lambda bi, i: (bi, i, 0)),
                  pl.BlockSpec((1, d), lambda bi, i: (0, 0)),
                  pl.BlockSpec((1, 1, d), seg),
                  pl.BlockSpec((1, 1, d), seg)],
        out_specs=pl.BlockSpec((1, tb, d), lambda bi, i: (bi, i, 0)),
        out_shape=jax.ShapeDtypeStruct((b, n, d), out_dtype),
        compiler_params=_cparams(("parallel", "parallel")),
        name="normmod",
    )(xs, g.reshape(1, d), sh, sc)


def _out_proj_kernel(ya_ref, yb_ref, yc_ref, yd_ref, w_ref, x_ref, g_ref, o_ref, *, lc, blocks_per_seq):
    acc = jnp.dot(ya_ref[...], w_ref[0:GROUP_W], preferred_element_type=F32)
    acc += jnp.dot(yb_ref[...], w_ref[GROUP_W:2 * GROUP_W], preferred_element_type=F32)
    acc += jnp.dot(yc_ref[...], w_ref[2 * GROUP_W:3 * GROUP_W], preferred_element_type=F32)
    acc += jnp.dot(yd_ref[...], w_ref[3 * GROUP_W:4 * GROUP_W], preferred_element_type=F32)
    tm = acc.shape[0]
    tok = lax.broadcasted_iota(jnp.int32, (tm, 1), 0) + (pl.program_id(0) % blocks_per_seq) * tm
    gate = jnp.where(tok < lc, g_ref[0:1], g_ref[1:2])
    o_ref[...] = x_ref[...] + gate * acc


def out_proj(groups, w, xs, gate, lc):
    b, n, d = xs.shape
    tm = _pick(n, (768, 384, 256, 128))
    tn = 1024
    bps = n // tm
    row = pl.BlockSpec((tm, GROUP_W), lambda i, j: (i, 0))
    kern = functools.partial(_out_proj_kernel, lc=lc, blocks_per_seq=bps)
    out = pl.pallas_call(
        kern,
        grid=(b * bps, d // tn),
        in_specs=[row, row, row, row,
                  pl.BlockSpec((4 * GROUP_W, tn), lambda i, j: (0, j)),
                  pl.BlockSpec((tm, tn), lambda i, j: (i, j)),
                  pl.BlockSpec((None, 2, tn), lambda i, j: (i // bps, 0, j))],
        out_specs=pl.BlockSpec((tm, tn), lambda i, j: (i, j)),
        out_shape=jax.ShapeDtypeStruct((b * n, d), F32),
        compiler_params=_cparams(("parallel", "parallel")),
        name="out_proj",
    )(*groups, w, xs.reshape(b * n, d), gate)
    return out.reshape(b, n, d)


def _mla_attn_kernel(q_ref, k_ref, v_ref, o_ref, *, lc, scale):
    q = q_ref[0, 0]

    def attend(nk):
        k = k_ref[0, 0, :nk]
        v = v_ref[0, 0, :nk]
        s = lax.dot_general(q, k, (((1,), (1,)), ((), ())), preferred_element_type=F32) * scale
        m = jnp.max(s, axis=-1, keepdims=True)
        p = jnp.exp(s - m)
        l = jnp.sum(p, axis=-1, keepdims=True)
        o = jnp.dot(p.astype(BF16), v, preferred_element_type=F32)
        o_ref[0] = (o / l).astype(o_ref.dtype)

    i = pl.program_id(2)

    @pl.when(i == 0)
    def _():
        attend(lc)

    @pl.when(i > 0)
    def _():
        attend(k_ref.shape[2])


def mla_attention(q, k, v, lc):
    b, h, n, dq = q.shape
    dv = v.shape[-1]
    tq = lc
    kern = functools.partial(_mla_attn_kernel, lc=lc, scale=float(dq) ** -0.5)
    return pl.pallas_call(
        kern,
        grid=(b, h, n // tq),
        in_specs=[pl.BlockSpec((1, 1, tq, dq), lambda bi, hi, i: (bi, hi, i, 0)),
                  pl.BlockSpec((1, 1, n, dq), lambda bi, hi, i: (bi, hi, 0, 0)),
                  pl.BlockSpec((1, 1, n, dv), lambda bi, hi, i: (bi, hi, 0, 0))],
        out_specs=pl.BlockSpec((1, tq, dv), lambda bi, hi, i: (bi, i, hi)),
        out_shape=jax.ShapeDtypeStruct((b, n, h * dv), BF16),
        compiler_params=_cparams(("parallel", "parallel", "arbitrary")),
        name="mla_attention",
    )(q, k, v)


def _diff_attn_kernel(q0_ref, q1_ref, k0_ref, k1_ref, v_ref, lam_ref, g_ref, o_ref, *, lc, scale, post):
    q0 = q0_ref[0, 0]
    q1 = q1_ref[0, 0]
    lam = lam_ref[...]

    def probs(q, k):
        s = lax.dot_general(q, k, (((1,), (1,)), ((), ())), preferred_element_type=F32) * scale
        m = jnp.max(s, axis=-1, keepdims=True)
        p = jnp.exp(s - m)
        return p, jnp.sum(p, axis=-1, keepdims=True)

    def attend(nk):
        p0, l0 = probs(q0, k0_ref[0, 0, :nk])
        p1, l1 = probs(q1, k1_ref[0, 0, :nk])
        attn = p0 * (1.0 / l0) - p1 * (lam[:, :1] / l1)
        o = jnp.dot(attn.astype(BF16), v_ref[0, 0, :nk], preferred_element_type=F32)
        ms = jnp.mean(o * o, axis=-1, keepdims=True)
        o_ref[0] = (o * lax.rsqrt(ms + NORM_EPS) * g_ref[...] * post).astype(o_ref.dtype)

    i = pl.program_id(2)

    @pl.when(i == 0)
    def _():
        attend(lc)

    @pl.when(i > 0)
    def _():
        attend(v_ref.shape[2])


def diff_attention(q0, q1, k0, k1, v, lam, sub_norm, lc, post):
    b, h, n, dh = q0.shape
    dv = v.shape[-1]
    tq = lc
    kern = functools.partial(_diff_attn_kernel, lc=lc, scale=float(dh) ** -0.5, post=post)
    qspec = pl.BlockSpec((1, 1, tq, dh), lambda bi, hi, i: (bi, hi, i, 0))
    kspec = pl.BlockSpec((1, 1, n, dh), lambda bi, hi, i: (bi, hi, 0, 0))
    return pl.pallas_call(
        kern,
        grid=(b, h, n // tq),
        in_specs=[qspec, qspec, kspec, kspec,
                  pl.BlockSpec((1, 1, n, dv), lambda bi, hi, i: (bi, hi, 0, 0)),
                  pl.BlockSpec((1, dv), lambda bi, hi, i: (0, 0)),
                  pl.BlockSpec((1, dv), lambda bi, hi, i: (0, 0))],
        out_specs=pl.BlockSpec((1, tq, dv), lambda bi, hi, i: (bi, i, hi)),
        out_shape=jax.ShapeDtypeStruct((b, n, h * dv), BF16),
        compiler_params=_cparams(("parallel", "parallel", "arbitrary")),
        name="diff_attention",
    )(q0, q1, k0, k1, v, jnp.broadcast_to(lam.reshape(1, 1), (1, dv)).astype(F32),
      sub_norm.reshape(1, dv))


S5_TB = 64
S5_GPT = 128 // S5_GROUP_CH
S5_TILES_IN = GROUP_W // 128
S5_TILES_ST = S5_GROUPS * S5_STATE // 128
S5_ST_PER_IN = S5_TILES_ST // S5_TILES_IN


def _s5_kernel(uf_ref, ub_ref, wd_ref, wr_ref, tab_ref, yf_ref, yb_ref, bu_re, bu_im, st_re, st_im):
    nb, tb, _ = uf_ref.shape
    rows = nb * tb
    half = S5_ST_PER_IN * 128

    @pl.when(pl.program_id(0) == 0)
    def _():
        st_re[...] = jnp.zeros_like(st_re)
        st_im[...] = jnp.zeros_like(st_im)

    for d, u_ref in enumerate((uf_ref, ub_ref)):
        x = u_ref[...].reshape(rows, GROUP_W).astype(BF16)
        for jt in range(S5_TILES_IN):
            bu = jnp.dot(x[:, jt * 128:(jt + 1) * 128], wd_ref[d, jt], preferred_element_type=F32)
            for c in range(S5_ST_PER_IN):
                bu_re[jt * S5_ST_PER_IN + c, d * rows:(d + 1) * rows, :] = bu[:, c * 128:(c + 1) * 128]
                bu_im[jt * S5_ST_PER_IN + c, d * rows:(d + 1) * rows, :] = bu[:, half + c * 128:half + (c + 1) * 128]

    def lane_tile(c, carry):
        for d in range(2):
            levels = [(tab_ref[d, 2 * k, c], tab_ref[d, 2 * k + 1, c], (1 << k) if d == 0 else 8 - (1 << k))
                      for k in range(3)]
            cr, ci = tab_ref[d, 6, c], tab_ref[d, 7, c]
            tiles = range(tb // 8) if d == 0 else range(tb // 8 - 1, -1, -1)
            for bi in range(nb):
                row = d * nb + bi
                car_r = st_re[c, row:row + 1, :]
                car_i = st_im[c, row:row + 1, :]
                for tt in tiles:
                    off = row * tb + tt * 8
                    hr = bu_re[c, off:off + 8, :]
                    hi = bu_im[c, off:off + 8, :]
                    for ar, ai, shift in levels:
                        pr = pltpu.roll(hr, shift, axis=0)
                        pi = pltpu.roll(hi, shift, axis=0)
                        hr, hi = hr + (ar * pr - ai * pi), hi + (ar * pi + ai * pr)
                    hr, hi = hr + (cr * car_r - ci * car_i), hi + (cr * car_i + ci * car_r)
                    bu_re[c, off:off + 8, :] = hr
                    bu_im[c, off:off + 8, :] = hi
                    last = 7 if d == 0 else 0
                    car_r, car_i = hr[last:last + 1], hi[last:last + 1]
                st_re[c, row:row + 1, :] = car_r
                st_im[c, row:row + 1, :] = car_i
        return carry

    lax.fori_loop(0, S5_TILES_ST, lane_tile, 0)

    for d, y_ref in enumerate((yf_ref, yb_ref)):
        for jt in range(S5_TILES_IN):
            tiles = [bu_re[jt * S5_ST_PER_IN + c, d * rows:(d + 1) * rows, :] for c in range(S5_ST_PER_IN)]
            tiles += [bu_im[jt * S5_ST_PER_IN + c, d * rows:(d + 1) * rows, :] for c in range(S5_ST_PER_IN)]
            h = jnp.concatenate(tiles, axis=1).astype(BF16)
            y = jnp.dot(h, wr_ref[d, jt], preferred_element_type=F32)
            y_ref[:, :, jt * 128:(jt + 1) * 128] = y.reshape(nb, tb, 128)


def _rev_block(i, nlc, ntot):
    return jnp.where(i < nlc, nlc - 1 - i, ntot - 1 - (i - nlc))


def s5_scan(z, lc, wd, wr, tab):
    b, n, _ = z.shape
    tb = min(S5_TB, lc)
    nlc, ntot = lc // tb, n // tb
    blk = (b, tb, GROUP_W)
    full = lambda a: pl.BlockSpec(a.shape, lambda i: (0,) * a.ndim)
    return pl.pallas_call(
        _s5_kernel,
        grid=(ntot,),
        in_specs=[pl.BlockSpec(blk, lambda i: (0, i, 0)),
                  pl.BlockSpec(blk, lambda i: (0, _rev_block(i, nlc, ntot), 0)),
                  full(wd), full(wr), full(tab)],
        out_specs=[pl.BlockSpec(blk, lambda i: (0, i, 0)),
                   pl.BlockSpec(blk, lambda i: (0, _rev_block(i, nlc, ntot), 0))],
        out_shape=[jax.ShapeDtypeStruct((b, n, GROUP_W), F32)] * 2,
        scratch_shapes=[pltpu.VMEM((S5_TILES_ST, 2 * b * tb, 128), F32)] * 2
        + [pltpu.VMEM((S5_TILES_ST, 8, 128), F32)] * 2,
        compiler_params=_cparams(("arbitrary",)),
        name="s5_scan",
    )(z, z, wd, wr, tab)


def _cmul(ar, ai, br, bi):
    return ar * br - ai * bi, ar * bi + ai * br


def s5_operators(a_re, a_im, log_step, b_re, b_im, c_re, c_im, nb):
    step = jnp.exp(log_step)[..., None]
    mag = jnp.exp(a_re * step)
    lr, li = mag * jnp.cos(a_im * step), mag * jnp.sin(a_im * step)
    den = a_re * a_re + a_im * a_im
    nr, ni = lr - 1.0, li
    fr = (nr * a_re + ni * a_im) / den
    fi = (ni * a_re - nr * a_im) / den
    bbr, bbi = _cmul(fr[..., None], fi[..., None], b_re, b_im)
    eye = jnp.eye(S5_GPT, dtype=F32)

    def drive(w):
        w = w.reshape(2, S5_TILES_IN, S5_GPT, S5_STATE, S5_GROUP_CH)
        w = jnp.einsum('dtgph,gk->dtghkp', w, eye)
        return w.reshape(2, S5_TILES_IN, 128, S5_GPT * S5_STATE)

    def readout(w):
        w = w.reshape(2, S5_TILES_IN, S5_GPT, S5_GROUP_CH, S5_STATE)
        w = jnp.einsum('dtghp,gk->dtgpkh', w, eye)
        return w.reshape(2, S5_TILES_IN, S5_GPT * S5_STATE, 128)

    wd = jnp.concatenate([drive(bbr), drive(bbi)], axis=-1).astype(BF16)
    wr = jnp.concatenate([readout(c_re), readout(-c_im)], axis=-2).astype(BF16)

    log_mag = (a_re * step).reshape(2, S5_TILES_ST, 1, 128)
    ang = (a_im * step).reshape(2, S5_TILES_ST, 1, 128)
    r8 = jnp.arange(8, dtype=F32).reshape(1, 1, 8, 1)

    def power(e, keep):
        e = jnp.broadcast_to(jnp.asarray(e, F32), (2, S5_TILES_ST, 8, 128))
        m = jnp.exp(log_mag * e)
        k = jnp.broadcast_to(keep, e.shape)
        return jnp.where(k, m * jnp.cos(ang * e), 0.0), jnp.where(k, m * jnp.sin(ang * e), 0.0)

    is_fwd = jnp.arange(2).reshape(2, 1, 1, 1) == 0
    tabs = []
    for s in (1, 2, 4):
        tabs += power(float(s), jnp.where(is_fwd, r8 >= s, r8 < 8 - s))
    tabs += power(jnp.where(is_fwd, r8 + 1.0, 8.0 - r8), True)
    return wd, wr, jnp.stack(tabs, axis=1)


def _s5_post_kernel(yf_ref, yb_ref, u_ref, d_ref, w_ref, o_ref):
    y = jax.nn.gelu(yf_ref[...] + yb_ref[...] + d_ref[...] * u_ref[...])
    gv = jnp.dot(y.astype(BF16), w_ref[...], preferred_element_type=F32)
    o_ref[...] = (gv[:, :GROUP_W] * jax.nn.sigmoid(gv[:, GROUP_W:])).astype(o_ref.dtype)


def s5_post(yf, yb, z, d_skip, w_glu):
    t = yf.shape[0]
    tm = _pick(t, (512, 256, 128))
    row = pl.BlockSpec((tm, GROUP_W), lambda i: (i, 0))
    return pl.pallas_call(
        _s5_post_kernel,
        grid=(t // tm,),
        in_specs=[row, row, row, pl.BlockSpec((1, GROUP_W), lambda i: (0, 0)),
                  pl.BlockSpec((GROUP_W, 2 * GROUP_W), lambda i: (0, 0))],
        out_specs=row,
        out_shape=jax.ShapeDtypeStruct((t, GROUP_W), BF16),
        compiler_params=_cparams(("parallel",)),
        name="s5_post",
    )(yf, yb, z, d_skip.reshape(1, GROUP_W), w_glu)


def _rwkv_kernel(w_ref, kk_ref, b_ref, k_ref, r_ref, v_ref, y_ref, s_ref):
    @pl.when(pl.program_id(0) == 0)
    def _():
        s_ref[...] = jnp.zeros_like(s_ref)

    def step(t, carry):
        def vgroup(vg, carry2):
            off = pl.multiple_of(vg * 8, 8)
            vrows = v_ref[t, pl.ds(off, 8), :]
            outs = []
            for j in range(8):
                sv = s_ref[off + j]
                sa = jnp.sum(sv * kk_ref[t], axis=0, keepdims=True)
                sn = sv * w_ref[t] - sa * b_ref[t] + vrows[j:j + 1] * k_ref[t]
                s_ref[off + j] = sn
                outs.append(jnp.sum(sn * r_ref[t], axis=0, keepdims=True))
            y_ref[t, pl.ds(off, 8), :] = jnp.concatenate(outs, axis=0)
            return carry2

        return lax.fori_loop(0, RW_HEAD // 8, vgroup, carry)

    lax.fori_loop(0, w_ref.shape[0], step, 0)


def rwkv_scan(w, kk, bv, k, r, v, tb=32):
    t, _, chains = w.shape
    spec = pl.BlockSpec((tb, RW_HEAD, chains), lambda i: (i, 0, 0))
    return pl.pallas_call(
        _rwkv_kernel,
        grid=(t // tb,),
        in_specs=[spec] * 6,
        out_specs=spec,
        out_shape=jax.ShapeDtypeStruct((t, RW_HEAD, chains), F32),
        scratch_shapes=[pltpu.VMEM((RW_HEAD, RW_HEAD, chains), F32)],
        compiler_params=_cparams(("arbitrary",)),
        name="rwkv_scan",
    )(w, kk, bv, k, r, v)


def _moe_kernel(be_ref, nu_ref, x_ref, wg_ref, wu_ref, wd_ref, o_ref, g_s, u_s, d_s):
    i = pl.program_id(0)
    used = i < nu_ref[0]
    changed = jnp.logical_or(i == 0, be_ref[i] != be_ref[jnp.maximum(i - 1, 0)])

    @pl.when(jnp.logical_and(used, changed))
    def _():
        g_s[...] = wg_ref[...].astype(BF16)
        u_s[...] = wu_ref[...].astype(BF16)
        d_s[...] = wd_ref[...].astype(BF16)

    @pl.when(used)
    def _():
        x = x_ref[...]
        g = jnp.dot(x, g_s[...], preferred_element_type=F32)
        u = jnp.dot(x, u_s[...], preferred_element_type=F32)
        h = (g * jax.nn.sigmoid(g) * u).astype(BF16)
        o_ref[...] = jnp.dot(h, d_s[...], preferred_element_type=F32)

    @pl.when(jnp.logical_not(used))
    def _():
        o_ref[...] = jnp.zeros_like(o_ref)


def moe_ffn(block_exp, n_used, xb, wg, wu, wd, layer):
    cap, d = xb.shape
    hid = wg.shape[-1]
    nb = cap // MOE_BLOCK
    once = pl.Buffered(1)
    grid_spec = pltpu.PrefetchScalarGridSpec(
        num_scalar_prefetch=2,
        grid=(nb,),
        in_specs=[pl.BlockSpec((MOE_BLOCK, d), lambda i, be, nu: (i, 0)),
                  pl.BlockSpec((None, None, d, hid), lambda i, be, nu: (layer, be[i], 0, 0), pipeline_mode=once),
                  pl.BlockSpec((None, None, d, hid), lambda i, be, nu: (layer, be[i], 0, 0), pipeline_mode=once),
                  pl.BlockSpec((None, None, hid, d), lambda i, be, nu: (layer, be[i], 0, 0), pipeline_mode=once)],
        out_specs=pl.BlockSpec((MOE_BLOCK, d), lambda i, be, nu: (i, 0)),
        scratch_shapes=[pltpu.VMEM((d, hid), BF16), pltpu.VMEM((d, hid), BF16), pltpu.VMEM((hid, d), BF16)],
    )
    return pl.pallas_call(
        _moe_kernel,
        grid_spec=grid_spec,
        out_shape=jax.ShapeDtypeStruct((cap, d), F32),
        compiler_params=_cparams(("arbitrary",)),
        name="moe_ffn",
    )(block_exp, n_used, xb, wg, wu, wd)


def _rms(x, g):
    return x * lax.rsqrt(jnp.mean(x * x, axis=-1, keepdims=True) + NORM_EPS) * g


def _axial_rope(rows, dim):
    n_freq = dim // 4
    inv = ROPE_BASE ** (-jnp.arange(n_freq, dtype=F32) / n_freq)
    row = jnp.repeat(jnp.arange(rows, dtype=F32), GRID_W)
    col = jnp.tile(jnp.arange(GRID_W, dtype=F32), rows)
    ang = jnp.concatenate([row[:, None] * inv, col[:, None] * inv], axis=-1)
    return jnp.cos(ang), jnp.sin(ang)


def _rope(x, cos, sin):
    half = x.shape[-1] // 2
    x1, x2 = x[..., :half], x[..., half:]
    return jnp.concatenate([x1 * cos - x2 * sin, x1 * sin + x2 * cos], axis=-1)


def _rope_lat(x, cos, sin, lc):
    extra = (None,) * (x.ndim - 3)
    cs = cos[(slice(None),) + extra]
    sn = sin[(slice(None),) + extra]
    return jnp.concatenate([x[:, :lc], _rope(x[:, lc:], cs, sn)], axis=1)


def _seq_rev(t, lc):
    return jnp.concatenate([jnp.flip(t[:, :lc], 1), jnp.flip(t[:, lc:], 1)], axis=1)


def _mm3(x, w, **kw):
    b, n, k = x.shape
    return matmul(x.reshape(b * n, k), w, **kw).reshape(b, n, -1)


def _mla_mixer(cq, ckv, krope, lc, q_a_norm, kv_a_norm, w_uq, w_ukv, q_norm, k_norm, cos, sin):
    b, n, _ = cq.shape
    q = _mm3(_rms(cq, q_a_norm), w_uq).reshape(b, n, MLA_HEADS, MLA_NOPE + MLA_ROPE)
    kv = _mm3(_rms(ckv, kv_a_norm), w_ukv).reshape(b, n, MLA_HEADS, MLA_NOPE + MLA_V)
    q_nope = _rms(q[..., :MLA_NOPE], q_norm[:MLA_NOPE])
    q_rope = _rope_lat(_rms(q[..., MLA_NOPE:], q_norm[MLA_NOPE:]), cos, sin, lc)
    k_nope = _rms(kv[..., :MLA_NOPE], k_norm[:MLA_NOPE])
    k_rope = _rope_lat(_rms(krope, k_norm[MLA_NOPE:]), cos, sin, lc)
    qf = jnp.concatenate([q_nope, q_rope], axis=-1)
    kf = jnp.concatenate([k_nope, jnp.broadcast_to(k_rope[:, :, None], (b, n, MLA_HEADS, MLA_ROPE))], axis=-1)
    t = lambda x: x.transpose(0, 2, 1, 3).astype(BF16)
    return mla_attention(t(qf), t(kf), t(kv[..., MLA_NOPE:]), lc)


def _diff_mixer(dq, dk, dv, lc, q_norm, k_norm, lam_vecs, sub_norm, lam_init, cos, sin):
    b, n, _ = dq.shape
    q = _rope_lat(_rms(dq.reshape(b, n, DIFF_HEADS, 2, DIFF_HEAD), q_norm), cos, sin, lc)
    k = _rope_lat(_rms(dk.reshape(b, n, DIFF_HEADS, 2, DIFF_HEAD), k_norm), cos, sin, lc)
    lam = jnp.exp(jnp.sum(lam_vecs[0] * lam_vecs[1])) - jnp.exp(jnp.sum(lam_vecs[2] * lam_vecs[3])) + lam_init
    t = lambda x: x.transpose(0, 2, 1, 3).astype(BF16)
    return diff_attention(t(q[:, :, :, 0]), t(q[:, :, :, 1]), t(k[:, :, :, 0]), t(k[:, :, :, 1]),
                          t(dv.reshape(b, n, DIFF_HEADS, 2 * DIFF_HEAD)), lam, sub_norm, lc,
                          1.0 - lam_init)


def _centred_shift(z, mu_prev, mu_next):
    z_prev = jnp.pad(z, ((0, 0), (1, 0), (0, 0)))[:, :-1]
    z_next = jnp.pad(z, ((0, 0), (0, 1), (0, 0)))[:, 1:]
    return z + mu_prev * (z_prev - z) + mu_next * (z_next - z)


def _rwkv_mixer(z, lc, mu, w0, w2, a0, a2, g2, k_k, k_a, r_k, ln_g, ln_b):
    b, n, _ = z.shape
    heads = lambda t: t.reshape(b, n, RW_HEADS, RW_HEAD)
    z = jnp.concatenate([_centred_shift(z[:, :lc], mu[0], mu[1]),
                         _centred_shift(z[:, lc:], mu[0], mu[1])], axis=1)
    r, k, v = z[..., :GROUP_W], z[..., GROUP_W:2 * GROUP_W], z[..., 2 * GROUP_W:3 * GROUP_W]
    o = 3 * GROUP_W
    wl, al, gl = z[..., o:o + 64], z[..., o + 64:o + 128], z[..., o + 128:o + 256]
    kk = heads(k * k_k)
    kk = kk * lax.rsqrt(jnp.maximum(jnp.sum(kk * kk, axis=-1, keepdims=True), 1e-12))
    lr_in = jnp.concatenate([jnp.tanh(wl), al, jax.nn.sigmoid(gl)], axis=-1)
    zw = jnp.zeros((RW_W_RANK, GROUP_W), F32)
    zg = jnp.zeros((RW_G_RANK, GROUP_W), F32)
    w_lr = jnp.concatenate([
        jnp.concatenate([w2[0], w2[1], zw, zw, zw], axis=1),
        jnp.concatenate([zw, zw, a2[0], a2[1], zw], axis=1),
        jnp.concatenate([zg, zg, zg, zg, g2], axis=1)], axis=0)
    lr = _mm3(lr_in, w_lr)
    g = lr[..., 4 * GROUP_W:]
    ka = k_a.reshape(RW_HEADS, RW_HEAD)
    rh, vh, kh = heads(r), heads(v), heads(k)
    per_dir = []
    for dr in range(2):
        w = -jax.nn.softplus(-heads(w0[dr] + lr[..., dr * GROUP_W:(dr + 1) * GROUP_W])) - 0.5
        a = jax.nn.sigmoid(heads(a0[dr] + lr[..., (2 + dr) * GROUP_W:(3 + dr) * GROUP_W]))
        per_dir.append((jnp.exp(-jnp.exp(w)), kh * (1 + (a - 1) * ka), a))

    def lanes(fwd, bwd):
        both = jnp.stack([fwd, _seq_rev(bwd, lc)])
        return both.transpose(2, 4, 0, 1, 3).reshape(n, RW_HEAD, 2 * b * RW_HEADS)

    y = rwkv_scan(lanes(per_dir[0][0], per_dir[1][0]), lanes(kk, kk),
                  lanes(kk * per_dir[0][2], kk * per_dir[1][2]),
                  lanes(per_dir[0][1], per_dir[1][1]), lanes(rh, rh), lanes(vh, vh))
    y = y.reshape(n, RW_HEAD, 2, b, RW_HEADS).transpose(2, 3, 0, 4, 1)
    o_f, o_b = y[0], _seq_rev(y[1], lc)
    bonus = (jnp.sum(rh * per_dir[0][1] * r_k, axis=-1, keepdims=True)
             + jnp.sum(rh * per_dir[1][1] * r_k, axis=-1, keepdims=True)) * vh
    o = o_f + o_b + bonus
    mean = jnp.mean(o, axis=-1, keepdims=True)
    var = jnp.mean(jnp.square(o - mean), axis=-1, keepdims=True)
    o = ((o - mean) * lax.rsqrt(var + RW_GN_EPS)).reshape(b, n, GROUP_W)
    return ((o * ln_g + ln_b) * g).astype(BF16)


def _hier_moe(h, wr, br_coarse, br_fine, wg, wu, wd, layer):
    t, d = h.shape
    logits = matmul(h, wr)
    p_group = jax.nn.softmax(logits[:, :MOE_GROUPS] + br_coarse, axis=-1)
    group = jnp.argmax(p_group, axis=-1).astype(jnp.int32)
    p_sel = jnp.take_along_axis(p_group, group[:, None], axis=-1)
    fine = (logits[:, MOE_GROUPS:MOE_GROUPS + MOE_EXPERTS] + br_fine).reshape(t, MOE_GROUPS, MOE_PER_GROUP)
    fine = jnp.take_along_axis(fine, group[:, None, None], axis=1)[:, 0]
    top_v, top_i = lax.top_k(fine, MOE_TOP_K)
    weight = p_sel * jax.nn.softmax(top_v, axis=-1)
    expert = group[:, None] * MOE_PER_GROUP + top_i.astype(jnp.int32)
    flat_e = expert.reshape(-1)
    n_assign = flat_e.shape[0]
    order = jnp.argsort(flat_e)
    sorted_e = flat_e[order]
    counts = jnp.bincount(flat_e, length=MOE_EXPERTS)
    padded = (counts + MOE_BLOCK - 1) // MOE_BLOCK * MOE_BLOCK
    pad_end = jnp.cumsum(padded)
    pad_start = pad_end - padded
    seg_start = jnp.cumsum(counts) - counts
    dest = (pad_start[sorted_e] + jnp.arange(n_assign) - seg_start[sorted_e]).astype(jnp.int32)
    n_blocks = -(-(n_assign + MOE_EXPERTS * (MOE_BLOCK - 1)) // MOE_BLOCK)
    cap = n_blocks * MOE_BLOCK
    slot_tok = jnp.full((cap,), t, jnp.int32).at[dest].set((order // MOE_TOP_K).astype(jnp.int32))
    block_exp = jnp.minimum(jnp.searchsorted(pad_end, jnp.arange(n_blocks) * MOE_BLOCK, side='right'),
                            MOE_EXPERTS - 1).astype(jnp.int32)
    n_used = (pad_end[-1:] // MOE_BLOCK).astype(jnp.int32)
    h_pad = jnp.concatenate([h, jnp.zeros((1, d), h.dtype)], axis=0)
    yb = moe_ffn(block_exp, n_used, h_pad[slot_tok], wg, wu, wd, layer)
    slot_of = jnp.zeros((n_assign,), jnp.int32).at[order].set(dest).reshape(t, MOE_TOP_K)
    return weight[:, 0:1] * yb[slot_of[:, 0]] + weight[:, 1:2] * yb[slot_of[:, 1]]


def kernel(x, c, ctx, c_ctx, mod_w, mod_b, norm1_g, norm2_g, w_in, w_out, s5_a_re, s5_a_im, s5_log_step, s5_b_re, s5_b_im, s5_c_re, s5_c_im, s5_d, s5_w_glu, mla_q_a_norm, mla_kv_a_norm, mla_w_uq, mla_w_ukv, mla_q_norm, mla_k_norm, rw_mu, rw_w0, rw_w2, rw_a0, rw_a2, rw_g2, rw_k_k, rw_k_a, rw_r_k, rw_ln_g, rw_ln_b, diff_q_norm, diff_k_norm, diff_lambda, diff_sub_norm, moe_wr_coarse, moe_br_coarse, moe_wr_fine, moe_br_fine, moe_w_gate, moe_w_up, moe_w_down):
    b, n, d = x.shape
    lc = ctx.shape[1]
    ntok = lc + n
    depth = mod_w.shape[0]
    rows = n // GRID_W
    cos_m, sin_m = _axial_rope(rows, MLA_ROPE)
    cos_d, sin_d = _axial_rope(rows, DIFF_HEAD)
    act = jnp.concatenate([jax.nn.silu(c), jax.nn.silu(c_ctx)[None],
                           jnp.zeros((8 - b - 1, d), F32)], axis=0)
    mod_all = mod_matmul(act, mod_w) + mod_b[:, None, :]
    xs = jnp.concatenate([ctx, x], axis=1)

    for l in range(depth):
        last = l == depth - 1
        lam_init = 0.8 - 0.6 * math.exp(-0.3 * l)
        mod_lat = mod_all[l, :b].reshape(b, N_MOD, d)
        mod_ctx = jnp.broadcast_to(mod_all[l, b].reshape(1, N_MOD, d), (b, N_MOD, d))
        mods = jnp.stack([mod_ctx, mod_lat], axis=2)
        seg = lambda m: jnp.concatenate([jnp.broadcast_to(m[:, 0:1], (b, lc, d)),
                                         jnp.broadcast_to(m[:, 1:2], (b, n, d))], axis=1)

        h = normmod(xs, norm1_g[l], mods[:, 0], mods[:, 1], lc).reshape(b * ntok, d)
        tm_in = _pick(b * ntok, (1024, 512, 256, 128))
        za = matmul(h, w_in[l][:, :ZA_WIDTH].astype(BF16), tn=ZA_WIDTH).reshape(b, ntok, ZA_WIDTH)
        zb = matmul(h, w_in[l][:, ZA_SPLIT:].astype(BF16), tm=tm_in,
                    tn=_pick(ZB_WIDTH, (1280, 640, 128))).reshape(b, ntok, ZB_WIDTH)

        wd, wr, lam = s5_operators(s5_a_re[l], s5_a_im[l], s5_log_step[l], s5_b_re[l], s5_b_im[l],
                                   s5_c_re[l], s5_c_im[l], b)
        yf, yr = s5_scan(za, lc, wd, wr, lam)
        ya = s5_post(yf.reshape(b * ntok, GROUP_W), yr.reshape(b * ntok, GROUP_W),
                     za.reshape(b * ntok, ZA_WIDTH), s5_d[l], s5_w_glu[l].astype(BF16))
        yb = _mla_mixer(za[..., ZA_CQ:ZA_CKV], za[..., ZA_CKV:ZA_KR], za[..., ZA_KR:ZA_SPLIT], lc,
                        mla_q_a_norm[l], mla_kv_a_norm[l], mla_w_uq[l], mla_w_ukv[l],
                        mla_q_norm[l], mla_k_norm[l], cos_m, sin_m)
        yc = _rwkv_mixer(zb[..., :ZB_DQ], lc, rw_mu[l], rw_w0[l], rw_w2[l], rw_a0[l], rw_a2[l],
                         rw_g2[l], rw_k_k[l], rw_k_a[l], rw_r_k[l], rw_ln_g[l], rw_ln_b[l])
        yd = _diff_mixer(zb[..., ZB_DQ:ZB_DK], zb[..., ZB_DK:ZB_DV], zb[..., ZB_DV:], lc,
                         diff_q_norm[l], diff_k_norm[l], diff_lambda[l], diff_sub_norm[l], lam_init,
                         cos_d, sin_d)
        flat = lambda y: y.reshape(b * ntok, GROUP_W)
        xs = out_proj((ya, flat(yb), flat(yc), flat(yd)), w_out[l].astype(BF16), xs, mods[:, 2], lc)

        h2 = normmod(xs, norm2_g[l], mods[:, 3], mods[:, 4], lc)
        wrt = jnp.concatenate([moe_wr_coarse[l], moe_wr_fine[l],
                               jnp.zeros((d, 128 - MOE_GROUPS - MOE_EXPERTS), F32)], axis=1)
        experts = (moe_w_gate, moe_w_up, moe_w_down, l)
        if last:
            y = _hier_moe(h2[:, lc:].reshape(b * n, d), wrt, moe_br_coarse[l], moe_br_fine[l], *experts)
            return xs[:, lc:] + mods[:, 5, 1:2] * y.reshape(b, n, d)
        y = _hier_moe(h2.reshape(b * ntok, d), wrt, moe_br_coarse[l], moe_br_fine[l], *experts)
        xs = xs + seg(mods[:, 5]) * y.reshape(b, ntok, d)
    return xs[:, lc:]
```

```python
import functools
import math

import jax
import jax.numpy as jnp
from jax import lax
from jax.experimental import pallas as pl
from jax.experimental.pallas import tpu as pltpu

F32 = jnp.float32
BF16 = jnp.bfloat16

D_MODEL = 4096
N_MOD = 6
GROUP_W = D_MODEL // 4
GRID_W = 64
ROPE_BASE = 10000.0
NORM_EPS = 1e-6

S5_GROUP_CH = 16
S5_GROUPS = GROUP_W // S5_GROUP_CH
S5_STATE = 64

MLA_HEADS = 8
MLA_NOPE = 128
MLA_ROPE = 64
MLA_V = GROUP_W // MLA_HEADS
MLA_Q_RANK = 512
MLA_KV_RANK = 256

RW_HEAD = 64
RW_HEADS = GROUP_W // RW_HEAD
RW_W_RANK = 64
RW_A_RANK = 64
RW_G_RANK = 128
RW_GN_EPS = 64e-5
RW_IN = 3 * GROUP_W + RW_W_RANK + RW_A_RANK + RW_G_RANK

DIFF_HEAD = 64
DIFF_HEADS = GROUP_W // (2 * DIFF_HEAD)

MOE_GROUPS = 4
MOE_PER_GROUP = 8
MOE_EXPERTS = MOE_GROUPS * MOE_PER_GROUP
MOE_TOP_K = 2
MOE_BLOCK = 128

ZA_CQ = GROUP_W
ZA_CKV = ZA_CQ + MLA_Q_RANK
ZA_KR = ZA_CKV + MLA_KV_RANK
ZA_SPLIT = ZA_KR + MLA_ROPE
ZA_WIDTH = ZA_KR + 128
ZB_DQ = RW_IN
ZB_DK = ZB_DQ + GROUP_W
ZB_DV = ZB_DK + GROUP_W
ZB_WIDTH = ZB_DV + GROUP_W

VMEM_LIMIT = 56 * 1024 * 1024


def _cparams(sem):
    return pltpu.CompilerParams(dimension_semantics=sem, vmem_limit_bytes=VMEM_LIMIT)


def _pick(n, prefs):
    for p in prefs:
        if n % p == 0:
            return p
    return n


def _mm_kernel(a_ref, b_ref, o_ref):
    a = a_ref[...].astype(BF16)
    b = b_ref[...].astype(BF16)
    o_ref[...] = jnp.dot(a, b, preferred_element_type=F32).astype(o_ref.dtype)


def matmul(a, b, out_dtype=F32, tm=None, tn=None, layer=None, n_cols=None):
    m, k = a.shape
    n = n_cols or b.shape[-1]
    tm = tm or _pick(m, (512, 256, 128))
    tn = tn or _pick(n, (1024, 768, 512, 256, 128))
    if layer is None:
        b_spec = pl.BlockSpec((k, tn), lambda i, j: (0, j))
    else:
        b_spec = pl.BlockSpec((None, k, tn), lambda i, j: (layer, 0, j))
    return pl.pallas_call(
        _mm_kernel,
        grid=(m // tm, n // tn),
        in_specs=[pl.BlockSpec((tm, k), lambda i, j: (i, 0)), b_spec],
        out_specs=pl.BlockSpec((tm, tn), lambda i, j: (i, j)),
        out_shape=jax.ShapeDtypeStruct((m, n), out_dtype),
        compiler_params=_cparams(("parallel", "parallel")),
        name="matmul",
    )(a, b)


def mod_matmul(act, mod_w, tn=1024):
    m, k = act.shape
    depth, _, n = mod_w.shape
    return pl.pallas_call(
        _mm_kernel,
        grid=(depth, n // tn),
        in_specs=[pl.BlockSpec((m, k), lambda l, j: (0, 0)),
                  pl.BlockSpec((None, k, tn), lambda l, j: (l, 0, j))],
        out_specs=pl.BlockSpec((None, m, tn), lambda l, j: (l, 0, j)),
        out_shape=jax.ShapeDtypeStruct((depth, m, n), F32),
        compiler_params=_cparams(("parallel", "parallel")),
        name="mod_matmul",
    )(act, mod_w)


def _normmod_kernel(x_ref, g_ref, sh_ref, sc_ref, o_ref):
    x = x_ref[0]
    ms = jnp.mean(x * x, axis=-1, keepdims=True)
    y = x * lax.rsqrt(ms + NORM_EPS) * g_ref[...]
    o_ref[0] = (y * (1.0 + sc_ref[0]) + sh_ref[0]).astype(o_ref.dtype)


def normmod(xs, g, shift, scale, lc, out_dtype=BF16):
    b, n, d = xs.shape
    tb = lc
    sh = shift.reshape(b * 2, 1, d)
    sc = scale.reshape(b * 2, 1, d)
    seg = lambda bi, i: (bi * 2 + jnp.minimum(i, 1), 0, 0)
    return pl.pallas_call(
        _normmod_kernel,
        grid=(b, n // tb),
        in_specs=[pl.BlockSpec((1, tb, d), lambda bi, i: (bi, i, 0)),
                  pl.BlockSpec((1, d), lambda bi, i: (0, 0)),
                  pl.BlockSpec((1, 1, d), seg),
                  pl.BlockSpec((1, 1, d), seg)],
        out_specs=pl.BlockSpec((1, tb, d), lambda bi, i: (bi, i, 0)),
        out_shape=jax.ShapeDtypeStruct((b, n, d), out_dtype),
        compiler_params=_cparams(("parallel", "parallel")),
        name="normmod",
    )(xs, g.reshape(1, d), sh, sc)


def _out_proj_kernel(ya_ref, yb_ref, yc_ref, yd_ref, w_ref, x_ref, g_ref, o_ref, *, lc, blocks_per_seq):
    acc = jnp.dot(ya_ref[...], w_ref[0:GROUP_W], preferred_element_type=F32)
    acc += jnp.dot(yb_ref[...], w_ref[GROUP_W:2 * GROUP_W], preferred_element_type=F32)
    acc += jnp.dot(yc_ref[...], w_ref[2 * GROUP_W:3 * GROUP_W], preferred_element_type=F32)
    acc += jnp.dot(yd_ref[...], w_ref[3 * GROUP_W:4 * GROUP_W], preferred_element_type=F32)
    tm = acc.shape[0]
    tok = lax.broadcasted_iota(jnp.int32, (tm, 1), 0) + (pl.program_id(0) % blocks_per_seq) * tm
    gate = jnp.where(tok < lc, g_ref[0:1], g_ref[1:2])
    o_ref[...] = x_ref[...] + gate * acc


def out_proj(groups, w, xs, gate, lc):
    b, n, d = xs.shape
    tm = _pick(n, (768, 384, 256, 128))
    tn = 1024
    bps = n // tm
    row = pl.BlockSpec((tm, GROUP_W), lambda i, j: (i, 0))
    kern = functools.partial(_out_proj_kernel, lc=lc, blocks_per_seq=bps)
    out = pl.pallas_call(
        kern,
        grid=(b * bps, d // tn),
        in_specs=[row, row, row, row,
                  pl.BlockSpec((4 * GROUP_W, tn), lambda i, j: (0, j)),
                  pl.BlockSpec((tm, tn), lambda i, j: (i, j)),
                  pl.BlockSpec((None, 2, tn), lambda i, j: (i // bps, 0, j))],
        out_specs=pl.BlockSpec((tm, tn), lambda i, j: (i, j)),
        out_shape=jax.ShapeDtypeStruct((b * n, d), F32),
        compiler_params=_cparams(("parallel", "parallel")),
        name="out_proj",
    )(*groups, w, xs.reshape(b * n, d), gate)
    return out.reshape(b, n, d)


def _mla_attn_kernel(q_ref, k_ref, v_ref, o_ref, *, lc, scale):
    q = q_ref[0, 0]

    def attend(nk):
        k = k_ref[0, 0, :nk]
        v = v_ref[0, 0, :nk]
        s = lax.dot_general(q, k, (((1,), (1,)), ((), ())), preferred_element_type=F32) * scale
        m = jnp.max(s, axis=-1, keepdims=True)
        p = jnp.exp(s - m)
        l = jnp.sum(p, axis=-1, keepdims=True)
        o = jnp.dot(p.astype(BF16), v, preferred_element_type=F32)
        o_ref[0] = (o / l).astype(o_ref.dtype)

    i = pl.program_id(2)

    @pl.when(i == 0)
    def _():
        attend(lc)

    @pl.when(i > 0)
    def _():
        attend(k_ref.shape[2])


def mla_attention(q, k, v, lc):
    b, h, n, dq = q.shape
    dv = v.shape[-1]
    tq = lc
    kern = functools.partial(_mla_attn_kernel, lc=lc, scale=float(dq) ** -0.5)
    return pl.pallas_call(
        kern,
        grid=(b, h, n // tq),
        in_specs=[pl.BlockSpec((1, 1, tq, dq), lambda bi, hi, i: (bi, hi, i, 0)),
                  pl.BlockSpec((1, 1, n, dq), lambda bi, hi, i: (bi, hi, 0, 0)),
                  pl.BlockSpec((1, 1, n, dv), lambda bi, hi, i: (bi, hi, 0, 0))],
        out_specs=pl.BlockSpec((1, tq, dv), lambda bi, hi, i: (bi, i, hi)),
        out_shape=jax.ShapeDtypeStruct((b, n, h * dv), BF16),
        compiler_params=_cparams(("parallel", "parallel", "arbitrary")),
        name="mla_attention",
    )(q, k, v)


def _diff_attn_kernel(q0_ref, q1_ref, k0_ref, k1_ref, v_ref, lam_ref, g_ref, o_ref, *, lc, scale, post):
    q0 = q0_ref[0, 0]
    q1 = q1_ref[0, 0]
    lam = lam_ref[...]

    def probs(q, k):
        s = lax.dot_general(q, k, (((1,), (1,)), ((), ())), preferred_element_type=F32) * scale
        m = jnp.max(s, axis=-1, keepdims=True)
        p = jnp.exp(s - m)
        return p, jnp.sum(p, axis=-1, keepdims=True)

    def attend(nk):
        p0, l0 = probs(q0, k0_ref[0, 0, :nk])
        p1, l1 = probs(q1, k1_ref[0, 0, :nk])
        attn = p0 * (1.0 / l0) - p1 * (lam[:, :1] / l1)
        o = jnp.dot(attn.astype(BF16), v_ref[0, 0, :nk], preferred_element_type=F32)
        ms = jnp.mean(o * o, axis=-1, keepdims=True)
        o_ref[0] = (o * lax.rsqrt(ms + NORM_EPS) * g_ref[...] * post).astype(o_ref.dtype)

    i = pl.program_id(2)

    @pl.when(i == 0)
    def _():
        attend(lc)

    @pl.when(i > 0)
    def _():
        attend(v_ref.shape[2])


def diff_attention(q0, q1, k0, k1, v, lam, sub_norm, lc, post):
    b, h, n, dh = q0.shape
    dv = v.shape[-1]
    tq = lc
    kern = functools.partial(_diff_attn_kernel, lc=lc, scale=float(dh) ** -0.5, post=post)
    qspec = pl.BlockSpec((1, 1, tq, dh), lambda bi, hi, i: (bi, hi, i, 0))
    kspec = pl.BlockSpec((1, 1, n, dh), lambda bi, hi, i: (bi, hi, 0, 0))
    return pl.pallas_call(
        kern,
        grid=(b, h, n // tq),
        in_specs=[qspec, qspec, kspec, kspec,
                  pl.BlockSpec((1, 1, n, dv), lambda bi, hi, i: (bi, hi, 0, 0)),
                  pl.BlockSpec((1, dv), lambda bi, hi, i: (0, 0)),
                  pl.BlockSpec((1, dv), lambda bi, hi, i: (0, 0))],
        out_specs=pl.BlockSpec((1, tq, dv), lambda bi, hi, i: (bi, i, hi)),
        out_shape=jax.ShapeDtypeStruct((b, n, h * dv), BF16),
        compiler_params=_cparams(("parallel", "parallel", "arbitrary")),
        name="diff_attention",
    )(q0, q1, k0, k1, v, jnp.broadcast_to(lam.reshape(1, 1), (1, dv)).astype(F32),
      sub_norm.reshape(1, dv))


S5_TB = 64
S5_GPT = 128 // S5_GROUP_CH
S5_TILES_IN = GROUP_W // 128
S5_TILES_ST = S5_GROUPS * S5_STATE // 128
S5_ST_PER_IN = S5_TILES_ST // S5_TILES_IN


def _s5_kernel(uf_ref, ub_ref, wd_ref, wr_ref, tab_ref, yf_ref, yb_ref, bu_re, bu_im, st_re, st_im):
    nb, tb, _ = uf_ref.shape
    rows = nb * tb
    half = S5_ST_PER_IN * 128

    @pl.when(pl.program_id(0) == 0)
    def _():
        st_re[...] = jnp.zeros_like(st_re)
        st_im[...] = jnp.zeros_like(st_im)

    for d, u_ref in enumerate((uf_ref, ub_ref)):
        x = u_ref[...].reshape(rows, GROUP_W).astype(BF16)
        for jt in range(S5_TILES_IN):
            bu = jnp.dot(x[:, jt * 128:(jt + 1) * 128], wd_ref[d, jt], preferred_element_type=F32)
            for c in range(S5_ST_PER_IN):
                bu_re[jt * S5_ST_PER_IN + c, d * rows:(d + 1) * rows, :] = bu[:, c * 128:(c + 1) * 128]
                bu_im[jt * S5_ST_PER_IN + c, d * rows:(d + 1) * rows, :] = bu[:, half + c * 128:half + (c + 1) * 128]

    def lane_tile(c, carry):
        for d in range(2):
            levels = [(tab_ref[d, 2 * k, c], tab_ref[d, 2 * k + 1, c], (1 << k) if d == 0 else 8 - (1 << k))
                      for k in range(3)]
            cr, ci = tab_ref[d, 6, c], tab_ref[d, 7, c]
            tiles = range(tb // 8) if d == 0 else range(tb // 8 - 1, -1, -1)
            for bi in range(nb):
                row = d * nb + bi
                car_r = st_re[c, row:row + 1, :]
                car_i = st_im[c, row:row + 1, :]
                for tt in tiles:
                    off = row * tb + tt * 8
                    hr = bu_re[c, off:off + 8, :]
                    hi = bu_im[c, off:off + 8, :]
                    for ar, ai, shift in levels:
                        pr = pltpu.roll(hr, shift, axis=0)
                        pi = pltpu.roll(hi, shift, axis=0)
                        hr, hi = hr + (ar * pr - ai * pi), hi + (ar * pi + ai * pr)
                    hr, hi = hr + (cr * car_r - ci * car_i), hi + (cr * car_i + ci * car_r)
                    bu_re[c, off:off + 8, :] = hr
                    bu_im[c, off:off + 8, :] = hi
                    last = 7 if d == 0 else 0
                    car_r, car_i = hr[last:last + 1], hi[last:last + 1]
                st_re[c, row:row + 1, :] = car_r
                st_im[c, row:row + 1, :] = car_i
        return carry

    lax.fori_loop(0, S5_TILES_ST, lane_tile, 0)

    for d, y_ref in enumerate((yf_ref, yb_ref)):
        for jt in range(S5_TILES_IN):
            tiles = [bu_re[jt * S5_ST_PER_IN + c, d * rows:(d + 1) * rows, :] for c in range(S5_ST_PER_IN)]
            tiles += [bu_im[jt * S5_ST_PER_IN + c, d * rows:(d + 1) * rows, :] for c in range(S5_ST_PER_IN)]
            h = jnp.concatenate(tiles, axis=1).astype(BF16)
            y = jnp.dot(h, wr_ref[d, jt], preferred_element_type=F32)
            y_ref[:, :, jt * 128:(jt + 1) * 128] = y.reshape(nb, tb, 128)


def _rev_block(i, nlc, ntot):
    return jnp.where(i < nlc, nlc - 1 - i, ntot - 1 - (i - nlc))


def s5_scan(z, lc, wd, wr, tab):
    b, n, _ = z.shape
    tb = min(S5_TB, lc)
    nlc, ntot = lc // tb, n // tb
    blk = (b, tb, GROUP_W)
    full = lambda a: pl.BlockSpec(a.shape, lambda i: (0,) * a.ndim)
    return pl.pallas_call(
        _s5_kernel,
        grid=(ntot,),
        in_specs=[pl.BlockSpec(blk, lambda i: (0, i, 0)),
                  pl.BlockSpec(blk, lambda i: (0, _rev_block(i, nlc, ntot), 0)),
                  full(wd), full(wr), full(tab)],
        out_specs=[pl.BlockSpec(blk, lambda i: (0, i, 0)),
                   pl.BlockSpec(blk, lambda i: (0, _rev_block(i, nlc, ntot), 0))],
        out_shape=[jax.ShapeDtypeStruct((b, n, GROUP_W), F32)] * 2,
        scratch_shapes=[pltpu.VMEM((S5_TILES_ST, 2 * b * tb, 128), F32)] * 2
        + [pltpu.VMEM((S5_TILES_ST, 8, 128), F32)] * 2,
        compiler_params=_cparams(("arbitrary",)),
        name="s5_scan",
    )(z, z, wd, wr, tab)


def _cmul(ar, ai, br, bi):
    return ar * br - ai * bi, ar * bi + ai * br


def s5_operators(a_re, a_im, log_step, b_re, b_im, c_re, c_im, nb):
    step = jnp.exp(log_step)[..., None]
    mag = jnp.exp(a_re * step)
    lr, li = mag * jnp.cos(a_im * step), mag * jnp.sin(a_im * step)
    den = a_re * a_re + a_im * a_im
    nr, ni = lr - 1.0, li
    fr = (nr * a_re + ni * a_im) / den
    fi = (ni * a_re - nr * a_im) / den
    bbr, bbi = _cmul(fr[..., None], fi[..., None], b_re, b_im)
    eye = jnp.eye(S5_GPT, dtype=F32)

    def drive(w):
        w = w.reshape(2, S5_TILES_IN, S5_GPT, S5_STATE, S5_GROUP_CH)
        w = jnp.einsum('dtgph,gk->dtghkp', w, eye)
        return w.reshape(2, S5_TILES_IN, 128, S5_GPT * S5_STATE)

    def readout(w):
        w = w.reshape(2, S5_TILES_IN, S5_GPT, S5_GROUP_CH, S5_STATE)
        w = jnp.einsum('dtghp,gk->dtgpkh', w, eye)
        return w.reshape(2, S5_TILES_IN, S5_GPT * S5_STATE, 128)

    wd = jnp.concatenate([drive(bbr), drive(bbi)], axis=-1).astype(BF16)
    wr = jnp.concatenate([readout(c_re), readout(-c_im)], axis=-2).astype(BF16)

    log_mag = (a_re * step).reshape(2, S5_TILES_ST, 1, 128)
    ang = (a_im * step).reshape(2, S5_TILES_ST, 1, 128)
    r8 = jnp.arange(8, dtype=F32).reshape(1, 1, 8, 1)

    def power(e, keep):
        e = jnp.broadcast_to(jnp.asarray(e, F32), (2, S5_TILES_ST, 8, 128))
        m = jnp.exp(log_mag * e)
        k = jnp.broadcast_to(keep, e.shape)
        return jnp.where(k, m * jnp.cos(ang * e), 0.0), jnp.where(k, m * jnp.sin(ang * e), 0.0)

    is_fwd = jnp.arange(2).reshape(2, 1, 1, 1) == 0
    tabs = []
    for s in (1, 2, 4):
        tabs += power(float(s), jnp.where(is_fwd, r8 >= s, r8 < 8 - s))
    tabs += power(jnp.where(is_fwd, r8 + 1.0, 8.0 - r8), True)
    return wd, wr, jnp.stack(tabs, axis=1)


def _s5_post_kernel(yf_ref, yb_ref, u_ref, d_ref, w_ref, o_ref):
    y = jax.nn.gelu(yf_ref[...] + yb_ref[...] + d_ref[...] * u_ref[...])
    gv = jnp.dot(y.astype(BF16), w_ref[...], preferred_element_type=F32)
    o_ref[...] = (gv[:, :GROUP_W] * jax.nn.sigmoid(gv[:, GROUP_W:])).astype(o_ref.dtype)


def s5_post(yf, yb, z, d_skip, w_glu):
    t = yf.shape[0]
    tm = _pick(t, (512, 256, 128))
    row = pl.BlockSpec((tm, GROUP_W), lambda i: (i, 0))
    return pl.pallas_call(
        _s5_post_kernel,
        grid=(t // tm,),
        in_specs=[row, row, row, pl.BlockSpec((1, GROUP_W), lambda i: (0, 0)),
                  pl.BlockSpec((GROUP_W, 2 * GROUP_W), lambda i: (0, 0))],
        out_specs=row,
        out_shape=jax.ShapeDtypeStruct((t, GROUP_W), BF16),
        compiler_params=_cparams(("parallel",)),
        name="s5_post",
    )(yf, yb, z, d_skip.reshape(1, GROUP_W), w_glu)


RW_TB = 32
RW_ROWS = 8
RW_TILES = GROUP_W // 128


def _split_bf16(x, parts):
    out = []
    for _ in range(parts - 1):
        hi = x.astype(BF16)
        out.append(hi)
        x = x - hi.astype(F32)
    out.append(x.astype(BF16))
    return out


def _permute_rows(p, x, parts):
    return sum(jnp.dot(p, piece, preferred_element_type=F32) for piece in _split_bf16(x, parts))


def _rwkv_to_lanes(fwd_ref, bwd_ref, p_ref, xt_s, dst_ref, parts):
    nb, tb, _ = fwd_ref.shape
    x = jnp.concatenate([fwd_ref[...].reshape(nb * tb, GROUP_W), bwd_ref[...].reshape(nb * tb, GROUP_W)], axis=0)
    xt_s[...] = _permute_rows(p_ref[...], x, parts)
    low = lax.broadcasted_iota(jnp.int32, (RW_HEAD, 128), 1) < RW_HEAD

    def pair(tp, carry):
        z = xt_s[pl.ds(pl.multiple_of(tp * 2 * RW_ROWS, 2 * RW_ROWS), 2 * RW_ROWS), :]
        zz = jnp.concatenate([z[p * RW_ROWS:(p + 1) * RW_ROWS, jt * 128:(jt + 1) * 128]
                              for p in range(2) for jt in range(RW_TILES)], axis=0)
        tt = zz.T
        top, bot = tt[:RW_HEAD], tt[RW_HEAD:]
        dst_ref[2 * tp] = jnp.where(low, top, pltpu.roll(bot, RW_HEAD, axis=1))
        dst_ref[2 * tp + 1] = jnp.where(low, pltpu.roll(top, RW_HEAD, axis=1), bot)
        return carry

    lax.fori_loop(0, tb // 2, pair, 0, unroll=8)


def _rwkv_from_lanes(y_s, pt_ref, xt_s, of_ref, ob_ref):
    nb, tb, _ = of_ref.shape
    low = lax.broadcasted_iota(jnp.int32, (RW_HEAD, 128), 1) < RW_HEAD

    def pair(tp, carry):
        y0, y1 = y_s[2 * tp], y_s[2 * tp + 1]
        top = jnp.where(low, y0, pltpu.roll(y1, RW_HEAD, axis=1))
        bot = jnp.where(low, pltpu.roll(y0, RW_HEAD, axis=1), y1)
        zz = jnp.concatenate([top, bot], axis=0).T
        base = pl.multiple_of(tp * 2 * RW_ROWS, 2 * RW_ROWS)
        for p in range(2):
            for jt in range(RW_TILES):
                blk = zz[(p * RW_TILES + jt) * RW_ROWS:(p * RW_TILES + jt + 1) * RW_ROWS]
                xt_s[pl.ds(base + p * RW_ROWS, RW_ROWS), jt * 128:(jt + 1) * 128] = blk
        return carry

    lax.fori_loop(0, tb // 2, pair, 0, unroll=8)
    y = _permute_rows(pt_ref[...], xt_s[...], 2)
    of_ref[...] = y[:nb * tb].reshape(nb, tb, GROUP_W)
    ob_ref[...] = y[nb * tb:].reshape(nb, tb, GROUP_W)


def _rwkv_kernel(rf_ref, kf_ref, vf_ref, wf_ref, af_ref, rb_ref, kb_ref, vb_ref, wb_ref, ab_ref,
                 p_ref, pt_ref, c_ref, of_ref, ob_ref,
                 r_s, k_s, v_s, w_s, a_s, y_s, xt_s, kk_s, bb_s, s_ref):
    tb = rf_ref.shape[1]

    @pl.when(pl.program_id(0) == 0)
    def _():
        s_ref[...] = jnp.zeros_like(s_ref)

    _rwkv_to_lanes(rf_ref, rb_ref, p_ref, xt_s, r_s, 2)
    _rwkv_to_lanes(kf_ref, kb_ref, p_ref, xt_s, k_s, 2)
    _rwkv_to_lanes(vf_ref, vb_ref, p_ref, xt_s, v_s, 2)
    _rwkv_to_lanes(wf_ref, wb_ref, p_ref, xt_s, w_s, 3)
    _rwkv_to_lanes(af_ref, ab_ref, p_ref, xt_s, a_s, 2)

    def step(t, carry):
        k = k_s[t]
        kk = k * c_ref[2]
        kk = kk * lax.rsqrt(jnp.maximum(jnp.sum(kk * kk, axis=0, keepdims=True), 1e-12))
        z = -(w_s[t] + c_ref[0])
        w = -(jnp.maximum(z, 0.0) + jnp.log(1.0 + jnp.exp(-jnp.abs(z)))) - 0.5
        a = jax.nn.sigmoid(a_s[t] + c_ref[1])
        kd = k * (1.0 + (a - 1.0) * c_ref[3])
        bonus = jnp.sum(r_s[t] * kd * c_ref[4], axis=0, keepdims=True)
        w_s[t] = jnp.exp(-jnp.exp(w))
        k_s[t] = kd
        kk_s[...] = kk
        bb_s[...] = kk * a

        def vgroup(vg, carry2):
            off = pl.multiple_of(vg * 8, 8)
            vrows = v_s[t, pl.ds(off, 8), :]
            outs = []
            for j in range(8):
                sv = s_ref[off + j]
                sa = jnp.sum(sv * kk_s[...], axis=0, keepdims=True)
                sn = sv * w_s[t] - sa * bb_s[...] + vrows[j:j + 1] * k_s[t]
                s_ref[off + j] = sn
                outs.append(jnp.sum(sn * r_s[t], axis=0, keepdims=True))
            y_s[t, pl.ds(off, 8), :] = jnp.concatenate(outs, axis=0) + bonus * vrows
            return carry2

        return lax.fori_loop(0, RW_HEAD // 8, vgroup, carry)

    lax.fori_loop(0, tb, step, 0)
    _rwkv_from_lanes(y_s, pt_ref, xt_s, of_ref, ob_ref)


def rwkv_scan(zs, lr, consts, lc):
    b, n, _ = zs.shape
    assert 2 * b == RW_ROWS
    tb = RW_TB
    nlc, ntot = lc // tb, n // tb
    rows = RW_ROWS * tb
    r_out = jnp.arange(rows)
    step, db = r_out // RW_ROWS, r_out % RW_ROWS
    src = db * tb + jnp.where(db < b, step, tb - 1 - step)
    perm = (src[:, None] == jnp.arange(rows)[None, :]).astype(BF16)
    blk = (b, tb, GROUP_W)
    fwd = lambda col: pl.BlockSpec(blk, lambda i: (0, i, col))
    bwd = lambda col: pl.BlockSpec(blk, lambda i: (0, _rev_block(i, nlc, ntot), col))
    full = lambda a: pl.BlockSpec(a.shape, lambda i: (0,) * a.ndim)
    tile = pltpu.VMEM((tb, RW_HEAD, 128), F32)
    return pl.pallas_call(
        _rwkv_kernel,
        grid=(ntot,),
        in_specs=[fwd(0), fwd(1), fwd(2), fwd(0), fwd(2), bwd(0), bwd(1), bwd(2), bwd(1), bwd(3),
                  full(perm), full(perm), full(consts)],
        out_specs=[fwd(0), bwd(0)],
        out_shape=[jax.ShapeDtypeStruct((b, n, GROUP_W), F32)] * 2,
        scratch_shapes=[tile] * 6 + [pltpu.VMEM((rows, GROUP_W), F32),
                                     pltpu.VMEM((RW_HEAD, 128), F32), pltpu.VMEM((RW_HEAD, 128), F32),
                                     pltpu.VMEM((RW_HEAD, RW_HEAD, 128), F32)],
        compiler_params=_cparams(("arbitrary",)),
        name="rwkv_scan",
    )(zs, zs, zs, lr, lr, zs, zs, zs, lr, lr, perm, perm.T, consts)


def _moe_kernel(be_ref, nu_ref, x_ref, wg_ref, wu_ref, wd_ref, o_ref, g_s, u_s, d_s):
    i = pl.program_id(0)
    used = i < nu_ref[0]
    changed = jnp.logical_or(i == 0, be_ref[i] != be_ref[jnp.maximum(i - 1, 0)])

    @pl.when(jnp.logical_and(used, changed))
    def _():
        g_s[...] = wg_ref[...].astype(BF16)
        u_s[...] = wu_ref[...].astype(BF16)
        d_s[...] = wd_ref[...].astype(BF16)

    @pl.when(used)
    def _():
        x = x_ref[...]
        g = jnp.dot(x, g_s[...], preferred_element_type=F32)
        u = jnp.dot(x, u_s[...], preferred_element_type=F32)
        h = (g * jax.nn.sigmoid(g) * u).astype(BF16)
        o_ref[...] = jnp.dot(h, d_s[...], preferred_element_type=F32)

    @pl.when(jnp.logical_not(used))
    def _():
        o_ref[...] = jnp.zeros_like(o_ref)


def moe_ffn(block_exp, n_used, xb, wg, wu, wd, layer):
    cap, d = xb.shape
    hid = wg.shape[-1]
    nb = cap // MOE_BLOCK
    once = pl.Buffered(1)
    grid_spec = pltpu.PrefetchScalarGridSpec(
        num_scalar_prefetch=2,
        grid=(nb,),
        in_specs=[pl.BlockSpec((MOE_BLOCK, d), lambda i, be, nu: (i, 0)),
                  pl.BlockSpec((None, None, d, hid), lambda i, be, nu: (layer, be[i], 0, 0), pipeline_mode=once),
                  pl.BlockSpec((None, None, d, hid), lambda i, be, nu: (layer, be[i], 0, 0), pipeline_mode=once),
                  pl.BlockSpec((None, None, hid, d), lambda i, be, nu: (layer, be[i], 0, 0), pipeline_mode=once)],
        out_specs=pl.BlockSpec((MOE_BLOCK, d), lambda i, be, nu: (i, 0)),
        scratch_shapes=[pltpu.VMEM((d, hid), BF16), pltpu.VMEM((d, hid), BF16), pltpu.VMEM((hid, d), BF16)],
    )
    return pl.pallas_call(
        _moe_kernel,
        grid_spec=grid_spec,
        out_shape=jax.ShapeDtypeStruct((cap, d), F32),
        compiler_params=_cparams(("arbitrary",)),
        name="moe_ffn",
    )(block_exp, n_used, xb, wg, wu, wd)


def _rms(x, g):
    return x * lax.rsqrt(jnp.mean(x * x, axis=-1, keepdims=True) + NORM_EPS) * g


def _axial_rope(rows, dim):
    n_freq = dim // 4
    inv = ROPE_BASE ** (-jnp.arange(n_freq, dtype=F32) / n_freq)
    row = jnp.repeat(jnp.arange(rows, dtype=F32), GRID_W)
    col = jnp.tile(jnp.arange(GRID_W, dtype=F32), rows)
    ang = jnp.concatenate([row[:, None] * inv, col[:, None] * inv], axis=-1)
    return jnp.cos(ang), jnp.sin(ang)


def _rope(x, cos, sin):
    half = x.shape[-1] // 2
    x1, x2 = x[..., :half], x[..., half:]
    return jnp.concatenate([x1 * cos - x2 * sin, x1 * sin + x2 * cos], axis=-1)


def _rope_lat(x, cos, sin, lc):
    extra = (None,) * (x.ndim - 3)
    cs = cos[(slice(None),) + extra]
    sn = sin[(slice(None),) + extra]
    return jnp.concatenate([x[:, :lc], _rope(x[:, lc:], cs, sn)], axis=1)


def _seq_rev(t, lc):
    return jnp.concatenate([jnp.flip(t[:, :lc], 1), jnp.flip(t[:, lc:], 1)], axis=1)


def _mm3(x, w, **kw):
    b, n, k = x.shape
    return matmul(x.reshape(b * n, k), w, **kw).reshape(b, n, -1)


def _mla_mixer(cq, ckv, krope, lc, q_a_norm, kv_a_norm, w_uq, w_ukv, q_norm, k_norm, cos, sin):
    b, n, _ = cq.shape
    q = _mm3(_rms(cq, q_a_norm), w_uq).reshape(b, n, MLA_HEADS, MLA_NOPE + MLA_ROPE)
    kv = _mm3(_rms(ckv, kv_a_norm), w_ukv).reshape(b, n, MLA_HEADS, MLA_NOPE + MLA_V)
    q_nope = _rms(q[..., :MLA_NOPE], q_norm[:MLA_NOPE])
    q_rope = _rope_lat(_rms(q[..., MLA_NOPE:], q_norm[MLA_NOPE:]), cos, sin, lc)
    k_nope = _rms(kv[..., :MLA_NOPE], k_norm[:MLA_NOPE])
    k_rope = _rope_lat(_rms(krope, k_norm[MLA_NOPE:]), cos, sin, lc)
    qf = jnp.concatenate([q_nope, q_rope], axis=-1)
    kf = jnp.concatenate([k_nope, jnp.broadcast_to(k_rope[:, :, None], (b, n, MLA_HEADS, MLA_ROPE))], axis=-1)
    t = lambda x: x.transpose(0, 2, 1, 3).astype(BF16)
    return mla_attention(t(qf), t(kf), t(kv[..., MLA_NOPE:]), lc)


def _diff_mixer(dq, dk, dv, lc, q_norm, k_norm, lam_vecs, sub_norm, lam_init, cos, sin):
    b, n, _ = dq.shape
    q = _rope_lat(_rms(dq.reshape(b, n, DIFF_HEADS, 2, DIFF_HEAD), q_norm), cos, sin, lc)
    k = _rope_lat(_rms(dk.reshape(b, n, DIFF_HEADS, 2, DIFF_HEAD), k_norm), cos, sin, lc)
    lam = jnp.exp(jnp.sum(lam_vecs[0] * lam_vecs[1])) - jnp.exp(jnp.sum(lam_vecs[2] * lam_vecs[3])) + lam_init
    t = lambda x: x.transpose(0, 2, 1, 3).astype(BF16)
    return diff_attention(t(q[:, :, :, 0]), t(q[:, :, :, 1]), t(k[:, :, :, 0]), t(k[:, :, :, 1]),
                          t(dv.reshape(b, n, DIFF_HEADS, 2 * DIFF_HEAD)), lam, sub_norm, lc,
                          1.0 - lam_init)


def _centred_shift(z, mu_prev, mu_next):
    z_prev = jnp.pad(z, ((0, 0), (1, 0), (0, 0)))[:, :-1]
    z_next = jnp.pad(z, ((0, 0), (0, 1), (0, 0)))[:, 1:]
    return z + mu_prev * (z_prev - z) + mu_next * (z_next - z)


def _rwkv_mixer(z, lc, mu, w0, w2, a0, a2, g2, k_k, k_a, r_k, ln_g, ln_b):
    b, n, _ = z.shape
    heads = lambda t: t.reshape(b, n, RW_HEADS, RW_HEAD)
    z = jnp.concatenate([_centred_shift(z[:, :lc], mu[0], mu[1]),
                         _centred_shift(z[:, lc:], mu[0], mu[1])], axis=1)
    o = 3 * GROUP_W
    wl, al, gl = z[..., o:o + 64], z[..., o + 64:o + 128], z[..., o + 128:o + 256]
    lr_in = jnp.concatenate([jnp.tanh(wl), al, jax.nn.sigmoid(gl)], axis=-1)
    zw = jnp.zeros((RW_W_RANK, GROUP_W), F32)
    zg = jnp.zeros((RW_G_RANK, GROUP_W), F32)
    w_lr = jnp.concatenate([
        jnp.concatenate([w2[0], w2[1], zw, zw, zw], axis=1),
        jnp.concatenate([zw, zw, a2[0], a2[1], zw], axis=1),
        jnp.concatenate([zg, zg, zg, zg, g2], axis=1)], axis=0)
    lr = _mm3(lr_in, w_lr)
    g = lr[..., 4 * GROUP_W:]

    def chain_tile(p_fwd, p_bwd):
        per_dir = [p.reshape(RW_TILES, 2, RW_HEAD).transpose(2, 1, 0) for p in (p_fwd, p_bwd)]
        return jnp.stack([per_dir[0]] * b + [per_dir[1]] * b, axis=-1).reshape(RW_HEAD, 128)

    consts = jnp.stack([chain_tile(w0[0], w0[1]), chain_tile(a0[0], a0[1]), chain_tile(k_k, k_k),
                        chain_tile(k_a, k_a), chain_tile(r_k.reshape(-1), r_k.reshape(-1))])
    o_f, o_b = rwkv_scan(z, lr, consts, lc)
    o = heads(o_f + o_b)
    mean = jnp.mean(o, axis=-1, keepdims=True)
    var = jnp.mean(jnp.square(o - mean), axis=-1, keepdims=True)
    o = ((o - mean) * lax.rsqrt(var + RW_GN_EPS)).reshape(b, n, GROUP_W)
    return ((o * ln_g + ln_b) * g).astype(BF16)


def _hier_moe(h, wr, br_coarse, br_fine, wg, wu, wd, layer):
    t, d = h.shape
    logits = matmul(h, wr)
    p_group = jax.nn.softmax(logits[:, :MOE_GROUPS] + br_coarse, axis=-1)
    group = jnp.argmax(p_group, axis=-1).astype(jnp.int32)
    p_sel = jnp.take_along_axis(p_group, group[:, None], axis=-1)
    fine = (logits[:, MOE_GROUPS:MOE_GROUPS + MOE_EXPERTS] + br_fine).reshape(t, MOE_GROUPS, MOE_PER_GROUP)
    fine = jnp.take_along_axis(fine, group[:, None, None], axis=1)[:, 0]
    top_v, top_i = lax.top_k(fine, MOE_TOP_K)
    weight = p_sel * jax.nn.softmax(top_v, axis=-1)
    expert = group[:, None] * MOE_PER_GROUP + top_i.astype(jnp.int32)
    flat_e = expert.reshape(-1)
    n_assign = flat_e.shape[0]
    order = jnp.argsort(flat_e)
    sorted_e = flat_e[order]
    counts = jnp.bincount(flat_e, length=MOE_EXPERTS)
    padded = (counts + MOE_BLOCK - 1) // MOE_BLOCK * MOE_BLOCK
    pad_end = jnp.cumsum(padded)
    pad_start = pad_end - padded
    seg_start = jnp.cumsum(counts) - counts
    dest = (pad_start[sorted_e] + jnp.arange(n_assign) - seg_start[sorted_e]).astype(jnp.int32)
    n_blocks = -(-(n_assign + MOE_EXPERTS * (MOE_BLOCK - 1)) // MOE_BLOCK)
    cap = n_blocks * MOE_BLOCK
    slot_tok = jnp.full((cap,), t, jnp.int32).at[dest].set((order // MOE_TOP_K).astype(jnp.int32))
    block_exp = jnp.minimum(jnp.searchsorted(pad_end, jnp.arange(n_blocks) * MOE_BLOCK, side='right'),
                            MOE_EXPERTS - 1).astype(jnp.int32)
    n_used = (pad_end[-1:] // MOE_BLOCK).astype(jnp.int32)
    h_pad = jnp.concatenate([h, jnp.zeros((1, d), h.dtype)], axis=0)
    yb = moe_ffn(block_exp, n_used, h_pad[slot_tok], wg, wu, wd, layer)
    slot_of = jnp.zeros((n_assign,), jnp.int32).at[order].set(dest).reshape(t, MOE_TOP_K)
    return weight[:, 0:1] * yb[slot_of[:, 0]] + weight[:, 1:2] * yb[slot_of[:, 1]]


def kernel(x, c, ctx, c_ctx, mod_w, mod_b, norm1_g, norm2_g, w_in, w_out, s5_a_re, s5_a_im, s5_log_step, s5_b_re, s5_b_im, s5_c_re, s5_c_im, s5_d, s5_w_glu, mla_q_a_norm, mla_kv_a_norm, mla_w_uq, mla_w_ukv, mla_q_norm, mla_k_norm, rw_mu, rw_w0, rw_w2, rw_a0, rw_a2, rw_g2, rw_k_k, rw_k_a, rw_r_k, rw_ln_g, rw_ln_b, diff_q_norm, diff_k_norm, diff_lambda, diff_sub_norm, moe_wr_coarse, moe_br_coarse, moe_wr_fine, moe_br_fine, moe_w_gate, moe_w_up, moe_w_down):
    b, n, d = x.shape
    lc = ctx.shape[1]
    ntok = lc + n
    depth = mod_w.shape[0]
    rows = n // GRID_W
    cos_m, sin_m = _axial_rope(rows, MLA_ROPE)
    cos_d, sin_d = _axial_rope(rows, DIFF_HEAD)
    act = jnp.concatenate([jax.nn.silu(c), jax.nn.silu(c_ctx)[None],
                           jnp.zeros((8 - b - 1, d), F32)], axis=0)
    mod_all = mod_matmul(act, mod_w) + mod_b[:, None, :]
    xs = jnp.concatenate([ctx, x], axis=1)

    for l in range(depth):
        last = l == depth - 1
        lam_init = 0.8 - 0.6 * math.exp(-0.3 * l)
        mod_lat = mod_all[l, :b].reshape(b, N_MOD, d)
        mod_ctx = jnp.broadcast_to(mod_all[l, b].reshape(1, N_MOD, d), (b, N_MOD, d))
        mods = jnp.stack([mod_ctx, mod_lat], axis=2)
        seg = lambda m: jnp.concatenate([jnp.broadcast_to(m[:, 0:1], (b, lc, d)),
                                         jnp.broadcast_to(m[:, 1:2], (b, n, d))], axis=1)

        h = normmod(xs, norm1_g[l], mods[:, 0], mods[:, 1], lc).reshape(b * ntok, d)
        tm_in = _pick(b * ntok, (1024, 512, 256, 128))
        za = matmul(h, w_in[l][:, :ZA_WIDTH].astype(BF16), tn=ZA_WIDTH).reshape(b, ntok, ZA_WIDTH)
        zb = matmul(h, w_in[l][:, ZA_SPLIT:].astype(BF16), tm=tm_in,
                    tn=_pick(ZB_WIDTH, (1280, 640, 128))).reshape(b, ntok, ZB_WIDTH)

        wd, wr, lam = s5_operators(s5_a_re[l], s5_a_im[l], s5_log_step[l], s5_b_re[l], s5_b_im[l],
                                   s5_c_re[l], s5_c_im[l], b)
        yf, yr = s5_scan(za, lc, wd, wr, lam)
        ya = s5_post(yf.reshape(b * ntok, GROUP_W), yr.reshape(b * ntok, GROUP_W),
                     za.reshape(b * ntok, ZA_WIDTH), s5_d[l], s5_w_glu[l].astype(BF16))
        yb = _mla_mixer(za[..., ZA_CQ:ZA_CKV], za[..., ZA_CKV:ZA_KR], za[..., ZA_KR:ZA_SPLIT], lc,
                        mla_q_a_norm[l], mla_kv_a_norm[l], mla_w_uq[l], mla_w_ukv[l],
                        mla_q_norm[l], mla_k_norm[l], cos_m, sin_m)
        yc = _rwkv_mixer(zb[..., :ZB_DQ], lc, rw_mu[l], rw_w0[l], rw_w2[l], rw_a0[l], rw_a2[l],
                         rw_g2[l], rw_k_k[l], rw_k_a[l], rw_r_k[l], rw_ln_g[l], rw_ln_b[l])
        yd = _diff_mixer(zb[..., ZB_DQ:ZB_DK], zb[..., ZB_DK:ZB_DV], zb[..., ZB_DV:], lc,
                         diff_q_norm[l], diff_k_norm[l], diff_lambda[l], diff_sub_norm[l], lam_init,
                         cos_d, sin_d)
        flat = lambda y: y.reshape(b * ntok, GROUP_W)
        xs = out_proj((ya, flat(yb), flat(yc), flat(yd)), w_out[l].astype(BF16), xs, mods[:, 2], lc)

        h2 = normmod(xs, norm2_g[l], mods[:, 3], mods[:, 4], lc)
        wrt = jnp.concatenate([moe_wr_coarse[l], moe_wr_fine[l],
                               jnp.zeros((d, 128 - MOE_GROUPS - MOE_EXPERTS), F32)], axis=1)
        experts = (moe_w_gate, moe_w_up, moe_w_down, l)
        if last:
            y = _hier_moe(h2[:, lc:].reshape(b * n, d), wrt, moe_br_coarse[l], moe_br_fine[l], *experts)
            return xs[:, lc:] + mods[:, 5, 1:2] * y.reshape(b, n, d)
        y = _hier_moe(h2.reshape(b * ntok, d), wrt, moe_br_coarse[l], moe_br_fine[l], *experts)
        xs = xs + seg(mods[:, 5]) * y.reshape(b, ntok, d)
    return xs[:, lc:]
```

```python
import functools
import math

import jax
import jax.numpy as jnp
from jax import lax
from jax.experimental import pallas as pl
from jax.experimental.pallas import tpu as pltpu

F32 = jnp.float32
BF16 = jnp.bfloat16

D_MODEL = 4096
N_MOD = 6
GROUP_W = D_MODEL // 4
GRID_W = 64
ROPE_BASE = 10000.0
NORM_EPS = 1e-6

S5_GROUP_CH = 16
S5_GROUPS = GROUP_W // S5_GROUP_CH
S5_STATE = 64

MLA_HEADS = 8
MLA_NOPE = 128
MLA_ROPE = 64
MLA_V = GROUP_W // MLA_HEADS
MLA_Q_RANK = 512
MLA_KV_RANK = 256

RW_HEAD = 64
RW_HEADS = GROUP_W // RW_HEAD
RW_W_RANK = 64
RW_A_RANK = 64
RW_G_RANK = 128
RW_GN_EPS = 64e-5
RW_IN = 3 * GROUP_W + RW_W_RANK + RW_A_RANK + RW_G_RANK

DIFF_HEAD = 64
DIFF_HEADS = GROUP_W // (2 * DIFF_HEAD)

MOE_GROUPS = 4
MOE_PER_GROUP = 8
MOE_EXPERTS = MOE_GROUPS * MOE_PER_GROUP
MOE_TOP_K = 2
MOE_BLOCK = 128

ZA_CQ = GROUP_W
ZA_CKV = ZA_CQ + MLA_Q_RANK
ZA_KR = ZA_CKV + MLA_KV_RANK
ZA_SPLIT = ZA_KR + MLA_ROPE
ZA_WIDTH = ZA_KR + 128
ZB_DQ = RW_IN
ZB_DK = ZB_DQ + GROUP_W
ZB_DV = ZB_DK + GROUP_W
ZB_WIDTH = ZB_DV + GROUP_W

VMEM_LIMIT = 56 * 1024 * 1024


def _cparams(sem):
    return pltpu.CompilerParams(dimension_semantics=sem, vmem_limit_bytes=VMEM_LIMIT)


def _pick(n, prefs):
    for p in prefs:
        if n % p == 0:
            return p
    return n


def _mm_kernel(a_ref, b_ref, o_ref):
    a = a_ref[...].astype(BF16)
    b = b_ref[...].astype(BF16)
    o_ref[...] = jnp.dot(a, b, preferred_element_type=F32).astype(o_ref.dtype)


def matmul(a, b, out_dtype=F32, tm=None, tn=None, layer=None, n_cols=None):
    m, k = a.shape
    n = n_cols or b.shape[-1]
    tm = tm or _pick(m, (512, 256, 128))
    tn = tn or _pick(n, (1024, 768, 512, 256, 128))
    if layer is None:
        b_spec = pl.BlockSpec((k, tn), lambda i, j: (0, j))
    else:
        b_spec = pl.BlockSpec((None, k, tn), lambda i, j: (layer, 0, j))
    return pl.pallas_call(
        _mm_kernel,
        grid=(m // tm, n // tn),
        in_specs=[pl.BlockSpec((tm, k), lambda i, j: (i, 0)), b_spec],
        out_specs=pl.BlockSpec((tm, tn), lambda i, j: (i, j)),
        out_shape=jax.ShapeDtypeStruct((m, n), out_dtype),
        compiler_params=_cparams(("parallel", "parallel")),
        name="matmul",
    )(a, b)


def mod_matmul(act, mod_w, tn=1024):
    m, k = act.shape
    depth, _, n = mod_w.shape
    return pl.pallas_call(
        _mm_kernel,
        grid=(depth, n // tn),
        in_specs=[pl.BlockSpec((m, k), lambda l, j: (0, 0)),
                  pl.BlockSpec((None, k, tn), lambda l, j: (l, 0, j))],
        out_specs=pl.BlockSpec((None, m, tn), lambda l, j: (l, 0, j)),
        out_shape=jax.ShapeDtypeStruct((depth, m, n), F32),
        compiler_params=_cparams(("parallel", "parallel")),
        name="mod_matmul",
    )(act, mod_w)


def _normmod_kernel(x_ref, g_ref, sh_ref, sc_ref, o_ref):
    x = x_ref[0]
    ms = jnp.mean(x * x, axis=-1, keepdims=True)
    y = x * lax.rsqrt(ms + NORM_EPS) * g_ref[...]
    o_ref[0] = (y * (1.0 + sc_ref[0]) + sh_ref[0]).astype(o_ref.dtype)


def normmod(xs, g, shift, scale, lc, out_dtype=BF16):
    b, n, d = xs.shape
    tb = lc
    sh = shift.reshape(b * 2, 1, d)
    sc = scale.reshape(b * 2, 1, d)
    seg = lambda bi, i: (bi * 2 + jnp.minimum(i, 1), 0, 0)
    return pl.pallas_call(
        _normmod_kernel,
        grid=(b, n // tb),
        in_specs=[pl.BlockSpec((1, tb, d), lambda bi, i: (bi, i, 0)),
                  pl.BlockSpec((1, d), lambda bi, i: (0, 0)),
                  pl.BlockSpec((1, 1, d), seg),
                  pl.BlockSpec((1, 1, d), seg)],
        out_specs=pl.BlockSpec((1, tb, d), lambda bi, i: (bi, i, 0)),
        out_shape=jax.ShapeDtypeStruct((b, n, d), out_dtype),
        compiler_params=_cparams(("parallel", "parallel")),
        name="normmod",
    )(xs, g.reshape(1, d), sh, sc)


def _out_proj_kernel(ya_ref, yb_ref, yc_ref, yd_ref, w_ref, x_ref, g_ref, o_ref, *, lc, blocks_per_seq):
    acc = jnp.dot(ya_ref[...], w_ref[0:GROUP_W], preferred_element_type=F32)
    acc += jnp.dot(yb_ref[...], w_ref[GROUP_W:2 * GROUP_W], preferred_element_type=F32)
    acc += jnp.dot(yc_ref[...], w_ref[2 * GROUP_W:3 * GROUP_W], preferred_element_type=F32)
    acc += jnp.dot(yd_ref[...], w_ref[3 * GROUP_W:4 * GROUP_W], preferred_element_type=F32)
    tm = acc.shape[0]
    tok = lax.broadcasted_iota(jnp.int32, (tm, 1), 0) + (pl.program_id(0) % blocks_per_seq) * tm
    gate = jnp.where(tok < lc, g_ref[0:1], g_ref[1:2])
    o_ref[...] = x_ref[...] + gate * acc


def out_proj(groups, w, xs, gate, lc):
    b, n, d = xs.shape
    tm = _pick(n, (768, 384, 256, 128))
    tn = 1024
    bps = n // tm
    row = pl.BlockSpec((tm, GROUP_W), lambda i, j: (i, 0))
    kern = functools.partial(_out_proj_kernel, lc=lc, blocks_per_seq=bps)
    out = pl.pallas_call(
        kern,
        grid=(b * bps, d // tn),
        in_specs=[row, row, row, row,
                  pl.BlockSpec((4 * GROUP_W, tn), lambda i, j: (0, j)),
                  pl.BlockSpec((tm, tn), lambda i, j: (i, j)),
                  pl.BlockSpec((None, 2, tn), lambda i, j: (i // bps, 0, j))],
        out_specs=pl.BlockSpec((tm, tn), lambda i, j: (i, j)),
        out_shape=jax.ShapeDtypeStruct((b * n, d), F32),
        compiler_params=_cparams(("parallel", "parallel")),
        name="out_proj",
    )(*groups, w, xs.reshape(b * n, d), gate)
    return out.reshape(b, n, d)


HEAD_TILE = 128


def _softmax_pv(q, k, v):
    s = lax.dot_general(q, k, (((1,), (1,)), ((), ())), preferred_element_type=F32)
    p = jnp.exp(s - jnp.max(s, axis=-1, keepdims=True))
    o = jnp.dot(p.astype(BF16), v, preferred_element_type=F32)
    return o, jnp.sum(p, axis=-1, keepdims=True)


def _per_segment(attend, lc, n):
    i = pl.program_id(2)

    @pl.when(i == 0)
    def _():
        attend(lc)

    @pl.when(i > 0)
    def _():
        attend(n)


def _mla_attn_kernel(qn_ref, qr_ref, kn_ref, kr_ref, v_ref, o_ref, *, lc):
    def attend(nk):
        q = jnp.concatenate([qn_ref[0], qr_ref[0]], axis=1)
        k = jnp.concatenate([kn_ref[0, :nk], kr_ref[0, :nk]], axis=1)
        o, l = _softmax_pv(q, k, v_ref[0, :nk])
        o_ref[0] = (o / l).astype(o_ref.dtype)

    _per_segment(attend, lc, v_ref.shape[1])


def mla_attention(q, kv, kr, lc):
    b, n, w = q.shape
    h = w // (2 * HEAD_TILE)
    tq = lc
    kern = functools.partial(_mla_attn_kernel, lc=lc)
    qspec = lambda off: pl.BlockSpec((1, tq, HEAD_TILE), lambda bi, hi, i: (bi, i, off + hi))
    kspec = lambda mul, off: pl.BlockSpec((1, n, HEAD_TILE), lambda bi, hi, i: (bi, 0, mul * hi + off))
    return pl.pallas_call(
        kern,
        grid=(b, h, n // tq),
        in_specs=[qspec(0), qspec(h), kspec(2, 0), kspec(0, 0), kspec(2, 1)],
        out_specs=qspec(0),
        out_shape=jax.ShapeDtypeStruct((b, n, h * HEAD_TILE), BF16),
        compiler_params=_cparams(("parallel", "parallel", "arbitrary")),
        name="mla_attention",
    )(q, q, kv, kr, kv)


def _diff_attn_kernel(q_ref, k_ref, v_ref, lam_ref, g_ref, o_ref, *, lc, post):
    lam = lam_ref[...]
    first = lax.broadcasted_iota(jnp.int32, q_ref.shape[1:], 1) < DIFF_HEAD

    def attend(nk):
        q = q_ref[0]
        k, v = k_ref[0, :nk], v_ref[0, :nk]
        o0, l0 = _softmax_pv(jnp.where(first, q, jnp.zeros_like(q)), k, v)
        o1, l1 = _softmax_pv(jnp.where(first, jnp.zeros_like(q), q), k, v)
        o = o0 / l0 - o1 * (lam[:, :1] / l1)
        ms = jnp.mean(o * o, axis=-1, keepdims=True)
        o_ref[0] = (o * lax.rsqrt(ms + NORM_EPS) * g_ref[...] * post).astype(o_ref.dtype)

    _per_segment(attend, lc, v_ref.shape[1])


def diff_attention(q, k, v, lam, sub_norm, lc, post):
    b, n, w = q.shape
    h = w // HEAD_TILE
    tq = lc
    kern = functools.partial(_diff_attn_kernel, lc=lc, post=post)
    qspec = pl.BlockSpec((1, tq, HEAD_TILE), lambda bi, hi, i: (bi, i, hi))
    kspec = pl.BlockSpec((1, n, HEAD_TILE), lambda bi, hi, i: (bi, 0, hi))
    vec = pl.BlockSpec((1, HEAD_TILE), lambda bi, hi, i: (0, 0))
    return pl.pallas_call(
        kern,
        grid=(b, h, n // tq),
        in_specs=[qspec, kspec, kspec, vec, vec],
        out_specs=qspec,
        out_shape=jax.ShapeDtypeStruct((b, n, w), BF16),
        compiler_params=_cparams(("parallel", "parallel", "arbitrary")),
        name="diff_attention",
    )(q, k, v, jnp.broadcast_to(lam.reshape(1, 1), (1, HEAD_TILE)).astype(F32),
      sub_norm.reshape(1, HEAD_TILE))


S5_TB = 64
S5_GPT = 128 // S5_GROUP_CH
S5_TILES_IN = GROUP_W // 128
S5_TILES_ST = S5_GROUPS * S5_STATE // 128
S5_ST_PER_IN = S5_TILES_ST // S5_TILES_IN


def _s5_kernel(uf_ref, ub_ref, wd_ref, wr_ref, tab_ref, yf_ref, yb_ref, bu_re, bu_im, st_re, st_im):
    nb, tb, _ = uf_ref.shape
    rows = nb * tb
    half = S5_ST_PER_IN * 128

    @pl.when(pl.program_id(0) == 0)
    def _():
        st_re[...] = jnp.zeros_like(st_re)
        st_im[...] = jnp.zeros_like(st_im)

    for d, u_ref in enumerate((uf_ref, ub_ref)):
        x = u_ref[...].reshape(rows, GROUP_W).astype(BF16)
        for jt in range(S5_TILES_IN):
            bu = jnp.dot(x[:, jt * 128:(jt + 1) * 128], wd_ref[d, jt], preferred_element_type=F32)
            for c in range(S5_ST_PER_IN):
                bu_re[jt * S5_ST_PER_IN + c, d * rows:(d + 1) * rows, :] = bu[:, c * 128:(c + 1) * 128]
                bu_im[jt * S5_ST_PER_IN + c, d * rows:(d + 1) * rows, :] = bu[:, half + c * 128:half + (c + 1) * 128]

    def lane_tile(c, carry):
        for d in range(2):
            levels = [(tab_ref[d, 2 * k, c], tab_ref[d, 2 * k + 1, c], (1 << k) if d == 0 else 8 - (1 << k))
                      for k in range(3)]
            cr, ci = tab_ref[d, 6, c], tab_ref[d, 7, c]
            tiles = range(tb // 8) if d == 0 else range(tb // 8 - 1, -1, -1)
            for bi in range(nb):
                row = d * nb + bi
                car_r = st_re[c, row:row + 1, :]
                car_i = st_im[c, row:row + 1, :]
                for tt in tiles:
                    off = row * tb + tt * 8
                    hr = bu_re[c, off:off + 8, :]
                    hi = bu_im[c, off:off + 8, :]
                    for ar, ai, shift in levels:
                        pr = pltpu.roll(hr, shift, axis=0)
                        pi = pltpu.roll(hi, shift, axis=0)
                        hr, hi = hr + (ar * pr - ai * pi), hi + (ar * pi + ai * pr)
                    hr, hi = hr + (cr * car_r - ci * car_i), hi + (cr * car_i + ci * car_r)
                    bu_re[c, off:off + 8, :] = hr
                    bu_im[c, off:off + 8, :] = hi
                    last = 7 if d == 0 else 0
                    car_r, car_i = hr[last:last + 1], hi[last:last + 1]
                st_re[c, row:row + 1, :] = car_r
                st_im[c, row:row + 1, :] = car_i
        return carry

    lax.fori_loop(0, S5_TILES_ST, lane_tile, 0)

    for d, y_ref in enumerate((yf_ref, yb_ref)):
        for jt in range(S5_TILES_IN):
            tiles = [bu_re[jt * S5_ST_PER_IN + c, d * rows:(d + 1) * rows, :] for c in range(S5_ST_PER_IN)]
            tiles += [bu_im[jt * S5_ST_PER_IN + c, d * rows:(d + 1) * rows, :] for c in range(S5_ST_PER_IN)]
            h = jnp.concatenate(tiles, axis=1).astype(BF16)
            y = jnp.dot(h, wr_ref[d, jt], preferred_element_type=F32)
            y_ref[:, :, jt * 128:(jt + 1) * 128] = y.reshape(nb, tb, 128)


def _rev_block(i, nlc, ntot):
    return jnp.where(i < nlc, nlc - 1 - i, ntot - 1 - (i - nlc))


def s5_scan(z, lc, wd, wr, tab):
    b, n, _ = z.shape
    tb = min(S5_TB, lc)
    nlc, ntot = lc // tb, n // tb
    blk = (b, tb, GROUP_W)
    full = lambda a: pl.BlockSpec(a.shape, lambda i: (0,) * a.ndim)
    return pl.pallas_call(
        _s5_kernel,
        grid=(ntot,),
        in_specs=[pl.BlockSpec(blk, lambda i: (0, i, 0)),
                  pl.BlockSpec(blk, lambda i: (0, _rev_block(i, nlc, ntot), 0)),
                  full(wd), full(wr), full(tab)],
        out_specs=[pl.BlockSpec(blk, lambda i: (0, i, 0)),
                   pl.BlockSpec(blk, lambda i: (0, _rev_block(i, nlc, ntot), 0))],
        out_shape=[jax.ShapeDtypeStruct((b, n, GROUP_W), F32)] * 2,
        scratch_shapes=[pltpu.VMEM((S5_TILES_ST, 2 * b * tb, 128), F32)] * 2
        + [pltpu.VMEM((S5_TILES_ST, 8, 128), F32)] * 2,
        compiler_params=_cparams(("arbitrary",)),
        name="s5_scan",
    )(z, z, wd, wr, tab)


def _cmul(ar, ai, br, bi):
    return ar * br - ai * bi, ar * bi + ai * br


def s5_operators(a_re, a_im, log_step, b_re, b_im, c_re, c_im, nb):
    step = jnp.exp(log_step)[..., None]
    mag = jnp.exp(a_re * step)
    lr, li = mag * jnp.cos(a_im * step), mag * jnp.sin(a_im * step)
    den = a_re * a_re + a_im * a_im
    nr, ni = lr - 1.0, li
    fr = (nr * a_re + ni * a_im) / den
    fi = (ni * a_re - nr * a_im) / den
    bbr, bbi = _cmul(fr[..., None], fi[..., None], b_re, b_im)
    eye = jnp.eye(S5_GPT, dtype=F32)

    def drive(w):
        w = w.reshape(2, S5_TILES_IN, S5_GPT, S5_STATE, S5_GROUP_CH)
        w = jnp.einsum('dtgph,gk->dtghkp', w, eye)
        return w.reshape(2, S5_TILES_IN, 128, S5_GPT * S5_STATE)

    def readout(w):
        w = w.reshape(2, S5_TILES_IN, S5_GPT, S5_GROUP_CH, S5_STATE)
        w = jnp.einsum('dtghp,gk->dtgpkh', w, eye)
        return w.reshape(2, S5_TILES_IN, S5_GPT * S5_STATE, 128)

    wd = jnp.concatenate([drive(bbr), drive(bbi)], axis=-1).astype(BF16)
    wr = jnp.concatenate([readout(c_re), readout(-c_im)], axis=-2).astype(BF16)

    log_mag = (a_re * step).reshape(2, S5_TILES_ST, 1, 128)
    ang = (a_im * step).reshape(2, S5_TILES_ST, 1, 128)
    r8 = jnp.arange(8, dtype=F32).reshape(1, 1, 8, 1)

    def power(e, keep):
        e = jnp.broadcast_to(jnp.asarray(e, F32), (2, S5_TILES_ST, 8, 128))
        m = jnp.exp(log_mag * e)
        k = jnp.broadcast_to(keep, e.shape)
        return jnp.where(k, m * jnp.cos(ang * e), 0.0), jnp.where(k, m * jnp.sin(ang * e), 0.0)

    is_fwd = jnp.arange(2).reshape(2, 1, 1, 1) == 0
    tabs = []
    for s in (1, 2, 4):
        tabs += power(float(s), jnp.where(is_fwd, r8 >= s, r8 < 8 - s))
    tabs += power(jnp.where(is_fwd, r8 + 1.0, 8.0 - r8), True)
    return wd, wr, jnp.stack(tabs, axis=1)


def _s5_post_kernel(yf_ref, yb_ref, u_ref, d_ref, w_ref, o_ref):
    y = jax.nn.gelu(yf_ref[...] + yb_ref[...] + d_ref[...] * u_ref[...])
    gv = jnp.dot(y.astype(BF16), w_ref[...], preferred_element_type=F32)
    o_ref[...] = (gv[:, :GROUP_W] * jax.nn.sigmoid(gv[:, GROUP_W:])).astype(o_ref.dtype)


def s5_post(yf, yb, z, d_skip, w_glu):
    t = yf.shape[0]
    tm = _pick(t, (512, 256, 128))
    row = pl.BlockSpec((tm, GROUP_W), lambda i: (i, 0))
    return pl.pallas_call(
        _s5_post_kernel,
        grid=(t // tm,),
        in_specs=[row, row, row, pl.BlockSpec((1, GROUP_W), lambda i: (0, 0)),
                  pl.BlockSpec((GROUP_W, 2 * GROUP_W), lambda i: (0, 0))],
        out_specs=row,
        out_shape=jax.ShapeDtypeStruct((t, GROUP_W), BF16),
        compiler_params=_cparams(("parallel",)),
        name="s5_post",
    )(yf, yb, z, d_skip.reshape(1, GROUP_W), w_glu)


RW_TB = 32
RW_ROWS = 8
RW_TILES = GROUP_W // 128


def _split_bf16(x, parts):
    out = []
    for _ in range(parts - 1):
        hi = x.astype(BF16)
        out.append(hi)
        x = x - hi.astype(F32)
    out.append(x.astype(BF16))
    return out


def _permute_rows(p, x, parts):
    return sum(jnp.dot(p, piece, preferred_element_type=F32) for piece in _split_bf16(x, parts))


def _rwkv_to_lanes(fwd_ref, bwd_ref, p_ref, xt_s, dst_ref, parts):
    nb, tb, _ = fwd_ref.shape
    x = jnp.concatenate([fwd_ref[...].reshape(nb * tb, GROUP_W), bwd_ref[...].reshape(nb * tb, GROUP_W)], axis=0)
    xt_s[...] = _permute_rows(p_ref[...], x, parts)
    low = lax.broadcasted_iota(jnp.int32, (RW_HEAD, 128), 1) < RW_HEAD

    def pair(tp, carry):
        z = xt_s[pl.ds(pl.multiple_of(tp * 2 * RW_ROWS, 2 * RW_ROWS), 2 * RW_ROWS), :]
        zz = jnp.concatenate([z[p * RW_ROWS:(p + 1) * RW_ROWS, jt * 128:(jt + 1) * 128]
                              for p in range(2) for jt in range(RW_TILES)], axis=0)
        tt = zz.T
        top, bot = tt[:RW_HEAD], tt[RW_HEAD:]
        dst_ref[2 * tp] = jnp.where(low, top, pltpu.roll(bot, RW_HEAD, axis=1))
        dst_ref[2 * tp + 1] = jnp.where(low, pltpu.roll(top, RW_HEAD, axis=1), bot)
        return carry

    lax.fori_loop(0, tb // 2, pair, 0, unroll=8)


def _rwkv_from_lanes(y_s, pt_ref, xt_s, of_ref, ob_ref):
    nb, tb, _ = of_ref.shape
    low = lax.broadcasted_iota(jnp.int32, (RW_HEAD, 128), 1) < RW_HEAD

    def pair(tp, carry):
        y0, y1 = y_s[2 * tp], y_s[2 * tp + 1]
        top = jnp.where(low, y0, pltpu.roll(y1, RW_HEAD, axis=1))
        bot = jnp.where(low, pltpu.roll(y0, RW_HEAD, axis=1), y1)
        zz = jnp.concatenate([top, bot], axis=0).T
        base = pl.multiple_of(tp * 2 * RW_ROWS, 2 * RW_ROWS)
        for p in range(2):
            for jt in range(RW_TILES):
                blk = zz[(p * RW_TILES + jt) * RW_ROWS:(p * RW_TILES + jt + 1) * RW_ROWS]
                xt_s[pl.ds(base + p * RW_ROWS, RW_ROWS), jt * 128:(jt + 1) * 128] = blk
        return carry

    lax.fori_loop(0, tb // 2, pair, 0, unroll=8)
    y = _permute_rows(pt_ref[...], xt_s[...], 2)
    of_ref[...] = y[:nb * tb].reshape(nb, tb, GROUP_W)
    ob_ref[...] = y[nb * tb:].reshape(nb, tb, GROUP_W)


def _rwkv_kernel(rf_ref, kf_ref, vf_ref, wf_ref, af_ref, rb_ref, kb_ref, vb_ref, wb_ref, ab_ref,
                 p_ref, pt_ref, c_ref, of_ref, ob_ref,
                 r_s, k_s, v_s, w_s, a_s, y_s, xt_s, kk_s, bb_s, s_ref):
    tb = rf_ref.shape[1]

    @pl.when(pl.program_id(0) == 0)
    def _():
        s_ref[...] = jnp.zeros_like(s_ref)

    _rwkv_to_lanes(rf_ref, rb_ref, p_ref, xt_s, r_s, 2)
    _rwkv_to_lanes(kf_ref, kb_ref, p_ref, xt_s, k_s, 2)
    _rwkv_to_lanes(vf_ref, vb_ref, p_ref, xt_s, v_s, 2)
    _rwkv_to_lanes(wf_ref, wb_ref, p_ref, xt_s, w_s, 3)
    _rwkv_to_lanes(af_ref, ab_ref, p_ref, xt_s, a_s, 2)

    def step(t, carry):
        k = k_s[t]
        kk = k * c_ref[2]
        kk = kk * lax.rsqrt(jnp.maximum(jnp.sum(kk * kk, axis=0, keepdims=True), 1e-12))
        z = -(w_s[t] + c_ref[0])
        w = -(jnp.maximum(z, 0.0) + jnp.log(1.0 + jnp.exp(-jnp.abs(z)))) - 0.5
        a = jax.nn.sigmoid(a_s[t] + c_ref[1])
        kd = k * (1.0 + (a - 1.0) * c_ref[3])
        bonus = jnp.sum(r_s[t] * kd * c_ref[4], axis=0, keepdims=True)
        w_s[t] = jnp.exp(-jnp.exp(w))
        k_s[t] = kd
        kk_s[...] = kk
        bb_s[...] = kk * a

        def vgroup(vg, carry2):
            off = pl.multiple_of(vg * 8, 8)
            vrows = v_s[t, pl.ds(off, 8), :]
            outs = []
            for j in range(8):
                sv = s_ref[off + j]
                sa = jnp.sum(sv * kk_s[...], axis=0, keepdims=True)
                sn = sv * w_s[t] - sa * bb_s[...] + vrows[j:j + 1] * k_s[t]
                s_ref[off + j] = sn
                outs.append(jnp.sum(sn * r_s[t], axis=0, keepdims=True))
            y_s[t, pl.ds(off, 8), :] = jnp.concatenate(outs, axis=0) + bonus * vrows
            return carry2

        return lax.fori_loop(0, RW_HEAD // 8, vgroup, carry)

    lax.fori_loop(0, tb, step, 0)
    _rwkv_from_lanes(y_s, pt_ref, xt_s, of_ref, ob_ref)


def rwkv_scan(zs, lr, consts, lc):
    b, n, _ = zs.shape
    assert 2 * b == RW_ROWS
    tb = RW_TB
    nlc, ntot = lc // tb, n // tb
    rows = RW_ROWS * tb
    r_out = jnp.arange(rows)
    step, db = r_out // RW_ROWS, r_out % RW_ROWS
    src = db * tb + jnp.where(db < b, step, tb - 1 - step)
    perm = (src[:, None] == jnp.arange(rows)[None, :]).astype(BF16)
    blk = (b, tb, GROUP_W)
    fwd = lambda col: pl.BlockSpec(blk, lambda i: (0, i, col))
    bwd = lambda col: pl.BlockSpec(blk, lambda i: (0, _rev_block(i, nlc, ntot), col))
    full = lambda a: pl.BlockSpec(a.shape, lambda i: (0,) * a.ndim)
    tile = pltpu.VMEM((tb, RW_HEAD, 128), F32)
    return pl.pallas_call(
        _rwkv_kernel,
        grid=(ntot,),
        in_specs=[fwd(0), fwd(1), fwd(2), fwd(0), fwd(2), bwd(0), bwd(1), bwd(2), bwd(1), bwd(3),
                  full(perm), full(perm), full(consts)],
        out_specs=[fwd(0), bwd(0)],
        out_shape=[jax.ShapeDtypeStruct((b, n, GROUP_W), F32)] * 2,
        scratch_shapes=[tile] * 6 + [pltpu.VMEM((rows, GROUP_W), F32),
                                     pltpu.VMEM((RW_HEAD, 128), F32), pltpu.VMEM((RW_HEAD, 128), F32),
                                     pltpu.VMEM((RW_HEAD, RW_HEAD, 128), F32)],
        compiler_params=_cparams(("arbitrary",)),
        name="rwkv_scan",
    )(zs, zs, zs, lr, lr, zs, zs, zs, lr, lr, perm, perm.T, consts)


def _moe_kernel(be_ref, nu_ref, x_ref, wg_ref, wu_ref, wd_ref, o_ref, g_s, u_s, d_s):
    i = pl.program_id(0)
    used = i < nu_ref[0]
    changed = jnp.logical_or(i == 0, be_ref[i] != be_ref[jnp.maximum(i - 1, 0)])

    @pl.when(jnp.logical_and(used, changed))
    def _():
        g_s[...] = wg_ref[...].astype(BF16)
        u_s[...] = wu_ref[...].astype(BF16)
        d_s[...] = wd_ref[...].astype(BF16)

    @pl.when(used)
    def _():
        x = x_ref[...]
        g = jnp.dot(x, g_s[...], preferred_element_type=F32)
        u = jnp.dot(x, u_s[...], preferred_element_type=F32)
        h = (g * jax.nn.sigmoid(g) * u).astype(BF16)
        o_ref[...] = jnp.dot(h, d_s[...], preferred_element_type=F32)

    @pl.when(jnp.logical_not(used))
    def _():
        o_ref[...] = jnp.zeros_like(o_ref)


def moe_ffn(block_exp, n_used, xb, wg, wu, wd, layer):
    cap, d = xb.shape
    hid = wg.shape[-1]
    nb = cap // MOE_BLOCK
    once = pl.Buffered(1)
    grid_spec = pltpu.PrefetchScalarGridSpec(
        num_scalar_prefetch=2,
        grid=(nb,),
        in_specs=[pl.BlockSpec((MOE_BLOCK, d), lambda i, be, nu: (i, 0)),
                  pl.BlockSpec((None, None, d, hid), lambda i, be, nu: (layer, be[i], 0, 0), pipeline_mode=once),
                  pl.BlockSpec((None, None, d, hid), lambda i, be, nu: (layer, be[i], 0, 0), pipeline_mode=once),
                  pl.BlockSpec((None, None, hid, d), lambda i, be, nu: (layer, be[i], 0, 0), pipeline_mode=once)],
        out_specs=pl.BlockSpec((MOE_BLOCK, d), lambda i, be, nu: (i, 0)),
        scratch_shapes=[pltpu.VMEM((d, hid), BF16), pltpu.VMEM((d, hid), BF16), pltpu.VMEM((hid, d), BF16)],
    )
    return pl.pallas_call(
        _moe_kernel,
        grid_spec=grid_spec,
        out_shape=jax.ShapeDtypeStruct((cap, d), F32),
        compiler_params=_cparams(("arbitrary",)),
        name="moe_ffn",
    )(block_exp, n_used, xb, wg, wu, wd)


def _rms(x, g):
    return x * lax.rsqrt(jnp.mean(x * x, axis=-1, keepdims=True) + NORM_EPS) * g


def _axial_rope(rows, dim):
    n_freq = dim // 4
    inv = ROPE_BASE ** (-jnp.arange(n_freq, dtype=F32) / n_freq)
    row = jnp.repeat(jnp.arange(rows, dtype=F32), GRID_W)
    col = jnp.tile(jnp.arange(GRID_W, dtype=F32), rows)
    ang = jnp.concatenate([row[:, None] * inv, col[:, None] * inv], axis=-1)
    return jnp.cos(ang), jnp.sin(ang)


def _rope(x, cos, sin):
    half = x.shape[-1] // 2
    x1, x2 = x[..., :half], x[..., half:]
    return jnp.concatenate([x1 * cos - x2 * sin, x1 * sin + x2 * cos], axis=-1)


def _rope_lat(x, cos, sin, lc):
    extra = (None,) * (x.ndim - 3)
    cs = cos[(slice(None),) + extra]
    sn = sin[(slice(None),) + extra]
    return jnp.concatenate([x[:, :lc], _rope(x[:, lc:], cs, sn)], axis=1)


def _seq_rev(t, lc):
    return jnp.concatenate([jnp.flip(t[:, :lc], 1), jnp.flip(t[:, lc:], 1)], axis=1)


def _mm3(x, w, **kw):
    b, n, k = x.shape
    return matmul(x.reshape(b * n, k), w, **kw).reshape(b, n, -1)


def _mla_mixer(cq, ckv, krope, lc, q_a_norm, kv_a_norm, w_uq, w_ukv, q_norm, k_norm, cos, sin):
    b, n, _ = cq.shape
    pad = HEAD_TILE - MLA_ROPE
    wq = w_uq.reshape(MLA_Q_RANK, MLA_HEADS, MLA_NOPE + MLA_ROPE)
    wq = jnp.concatenate([wq[:, :, :MLA_NOPE].reshape(MLA_Q_RANK, -1),
                          jnp.pad(wq[:, :, MLA_NOPE:], ((0, 0), (0, 0), (0, pad))).reshape(MLA_Q_RANK, -1)], axis=1)
    q = _mm3(_rms(cq, q_a_norm), wq).reshape(b, n, 2, MLA_HEADS, HEAD_TILE)
    kv = _mm3(_rms(ckv, kv_a_norm), w_ukv).reshape(b, n, MLA_HEADS, MLA_NOPE + MLA_V)
    scale = float(MLA_NOPE + MLA_ROPE) ** -0.5
    q_nope = _rms(q[:, :, 0], q_norm[:MLA_NOPE]) * scale
    q_rope = _rope_lat(_rms(q[:, :, 1, :, :MLA_ROPE], q_norm[MLA_NOPE:]), cos, sin, lc) * scale
    q_rope = jnp.pad(q_rope, ((0, 0), (0, 0), (0, 0), (0, pad)))
    k_nope = _rms(kv[..., :MLA_NOPE], k_norm[:MLA_NOPE])
    k_rope = _rope_lat(_rms(krope, k_norm[MLA_NOPE:]), cos, sin, lc)
    qf = jnp.stack([q_nope, q_rope], axis=2).reshape(b, n, 2 * MLA_HEADS * HEAD_TILE).astype(BF16)
    kvf = jnp.concatenate([k_nope, kv[..., MLA_NOPE:]], axis=-1).reshape(b, n, -1).astype(BF16)
    return mla_attention(qf, kvf, jnp.pad(k_rope, ((0, 0), (0, 0), (0, pad))).astype(BF16), lc)


def _diff_mixer(dq, dk, dv, lc, q_norm, k_norm, lam_vecs, sub_norm, lam_init, cos, sin):
    b, n, _ = dq.shape
    q = _rope_lat(_rms(dq.reshape(b, n, DIFF_HEADS, 2, DIFF_HEAD), q_norm), cos, sin, lc) * float(DIFF_HEAD) ** -0.5
    k = _rope_lat(_rms(dk.reshape(b, n, DIFF_HEADS, 2, DIFF_HEAD), k_norm), cos, sin, lc)
    lam = jnp.exp(jnp.sum(lam_vecs[0] * lam_vecs[1])) - jnp.exp(jnp.sum(lam_vecs[2] * lam_vecs[3])) + lam_init
    flat = lambda x: x.reshape(b, n, GROUP_W).astype(BF16)
    return diff_attention(flat(q), flat(k), dv.astype(BF16), lam, sub_norm, lc, 1.0 - lam_init)


def _centred_shift(z, mu_prev, mu_next):
    z_prev = jnp.pad(z, ((0, 0), (1, 0), (0, 0)))[:, :-1]
    z_next = jnp.pad(z, ((0, 0), (0, 1), (0, 0)))[:, 1:]
    return z + mu_prev * (z_prev - z) + mu_next * (z_next - z)


def _rwkv_mixer(z, lc, mu, w0, w2, a0, a2, g2, k_k, k_a, r_k, ln_g, ln_b):
    b, n, _ = z.shape
    heads = lambda t: t.reshape(b, n, RW_HEADS, RW_HEAD)
    z = jnp.concatenate([_centred_shift(z[:, :lc], mu[0], mu[1]),
                         _centred_shift(z[:, lc:], mu[0], mu[1])], axis=1)
    o = 3 * GROUP_W
    wl, al, gl = z[..., o:o + 64], z[..., o + 64:o + 128], z[..., o + 128:o + 256]
    lr_in = jnp.concatenate([jnp.tanh(wl), al, jax.nn.sigmoid(gl)], axis=-1)
    zw = jnp.zeros((RW_W_RANK, GROUP_W), F32)
    zg = jnp.zeros((RW_G_RANK, GROUP_W), F32)
    w_lr = jnp.concatenate([
        jnp.concatenate([w2[0], w2[1], zw, zw, zw], axis=1),
        jnp.concatenate([zw, zw, a2[0], a2[1], zw], axis=1),
        jnp.concatenate([zg, zg, zg, zg, g2], axis=1)], axis=0)
    lr = _mm3(lr_in, w_lr)
    g = lr[..., 4 * GROUP_W:]

    def chain_tile(p_fwd, p_bwd):
        per_dir = [p.reshape(RW_TILES, 2, RW_HEAD).transpose(2, 1, 0) for p in (p_fwd, p_bwd)]
        return jnp.stack([per_dir[0]] * b + [per_dir[1]] * b, axis=-1).reshape(RW_HEAD, 128)

    consts = jnp.stack([chain_tile(w0[0], w0[1]), chain_tile(a0[0], a0[1]), chain_tile(k_k, k_k),
                        chain_tile(k_a, k_a), chain_tile(r_k.reshape(-1), r_k.reshape(-1))])
    o_f, o_b = rwkv_scan(z, lr, consts, lc)
    o = heads(o_f + o_b)
    mean = jnp.mean(o, axis=-1, keepdims=True)
    var = jnp.mean(jnp.square(o - mean), axis=-1, keepdims=True)
    o = ((o - mean) * lax.rsqrt(var + RW_GN_EPS)).reshape(b, n, GROUP_W)
    return ((o * ln_g + ln_b) * g).astype(BF16)


def _hier_moe(h, wr, br_coarse, br_fine, wg, wu, wd, layer):
    t, d = h.shape
    logits = matmul(h, wr)
    p_group = jax.nn.softmax(logits[:, :MOE_GROUPS] + br_coarse, axis=-1)
    group = jnp.argmax(p_group, axis=-1).astype(jnp.int32)
    p_sel = jnp.take_along_axis(p_group, group[:, None], axis=-1)
    fine = (logits[:, MOE_GROUPS:MOE_GROUPS + MOE_EXPERTS] + br_fine).reshape(t, MOE_GROUPS, MOE_PER_GROUP)
    fine = jnp.take_along_axis(fine, group[:, None, None], axis=1)[:, 0]
    top_v, top_i = lax.top_k(fine, MOE_TOP_K)
    weight = p_sel * jax.nn.softmax(top_v, axis=-1)
    expert = group[:, None] * MOE_PER_GROUP + top_i.astype(jnp.int32)
    flat_e = expert.reshape(-1)
    n_assign = flat_e.shape[0]
    onehot = (flat_e[:, None] == jnp.arange(MOE_EXPERTS, dtype=jnp.int32)[None, :]).astype(jnp.int32)
    running = jnp.cumsum(onehot, axis=0)
    counts = running[-1]
    padded = (counts + MOE_BLOCK - 1) // MOE_BLOCK * MOE_BLOCK
    pad_end = jnp.cumsum(padded)
    pad_start = pad_end - padded
    dest = jnp.sum(onehot * (running - 1 + pad_start[None, :]), axis=1).astype(jnp.int32)
    n_blocks = -(-(n_assign + MOE_EXPERTS * (MOE_BLOCK - 1)) // MOE_BLOCK)
    cap = n_blocks * MOE_BLOCK
    slot_tok = jnp.full((cap,), t, jnp.int32).at[dest].set(jnp.arange(n_assign, dtype=jnp.int32) // MOE_TOP_K)
    block_exp = jnp.minimum(jnp.searchsorted(pad_end, jnp.arange(n_blocks) * MOE_BLOCK, side='right'),
                            MOE_EXPERTS - 1).astype(jnp.int32)
    n_used = (pad_end[-1:] // MOE_BLOCK).astype(jnp.int32)
    h_pad = jnp.concatenate([h, jnp.zeros((1, d), h.dtype)], axis=0)
    yb = moe_ffn(block_exp, n_used, h_pad[slot_tok], wg, wu, wd, layer)
    slot_of = dest.reshape(t, MOE_TOP_K)
    return weight[:, 0:1] * yb[slot_of[:, 0]] + weight[:, 1:2] * yb[slot_of[:, 1]]


def kernel(x, c, ctx, c_ctx, mod_w, mod_b, norm1_g, norm2_g, w_in, w_out, s5_a_re, s5_a_im, s5_log_step, s5_b_re, s5_b_im, s5_c_re, s5_c_im, s5_d, s5_w_glu, mla_q_a_norm, mla_kv_a_norm, mla_w_uq, mla_w_ukv, mla_q_norm, mla_k_norm, rw_mu, rw_w0, rw_w2, rw_a0, rw_a2, rw_g2, rw_k_k, rw_k_a, rw_r_k, rw_ln_g, rw_ln_b, diff_q_norm, diff_k_norm, diff_lambda, diff_sub_norm, moe_wr_coarse, moe_br_coarse, moe_wr_fine, moe_br_fine, moe_w_gate, moe_w_up, moe_w_down):
    b, n, d = x.shape
    lc = ctx.shape[1]
    ntok = lc + n
    depth = mod_w.shape[0]
    rows = n // GRID_W
    cos_m, sin_m = _axial_rope(rows, MLA_ROPE)
    cos_d, sin_d = _axial_rope(rows, DIFF_HEAD)
    act = jnp.concatenate([jax.nn.silu(c), jax.nn.silu(c_ctx)[None],
                           jnp.zeros((8 - b - 1, d), F32)], axis=0)
    mod_all = mod_matmul(act, mod_w) + mod_b[:, None, :]
    xs = jnp.concatenate([ctx, x], axis=1)

    for l in range(depth):
        last = l == depth - 1
        lam_init = 0.8 - 0.6 * math.exp(-0.3 * l)
        mod_lat = mod_all[l, :b].reshape(b, N_MOD, d)
        mod_ctx = jnp.broadcast_to(mod_all[l, b].reshape(1, N_MOD, d), (b, N_MOD, d))
        mods = jnp.stack([mod_ctx, mod_lat], axis=2)
        seg = lambda m: jnp.concatenate([jnp.broadcast_to(m[:, 0:1], (b, lc, d)),
                                         jnp.broadcast_to(m[:, 1:2], (b, n, d))], axis=1)

        h = normmod(xs, norm1_g[l], mods[:, 0], mods[:, 1], lc).reshape(b * ntok, d)
        tm_in = _pick(b * ntok, (1024, 512, 256, 128))
        za = matmul(h, w_in[l][:, :ZA_WIDTH].astype(BF16), tn=ZA_WIDTH).reshape(b, ntok, ZA_WIDTH)
        zb = matmul(h, w_in[l][:, ZA_SPLIT:].astype(BF16), tm=tm_in,
                    tn=_pick(ZB_WIDTH, (1280, 640, 128))).reshape(b, ntok, ZB_WIDTH)

        wd, wr, lam = s5_operators(s5_a_re[l], s5_a_im[l], s5_log_step[l], s5_b_re[l], s5_b_im[l],
                                   s5_c_re[l], s5_c_im[l], b)
        yf, yr = s5_scan(za, lc, wd, wr, lam)
        ya = s5_post(yf.reshape(b * ntok, GROUP_W), yr.reshape(b * ntok, GROUP_W),
                     za.reshape(b * ntok, ZA_WIDTH), s5_d[l], s5_w_glu[l].astype(BF16))
        yb = _mla_mixer(za[..., ZA_CQ:ZA_CKV], za[..., ZA_CKV:ZA_KR], za[..., ZA_KR:ZA_SPLIT], lc,
                        mla_q_a_norm[l], mla_kv_a_norm[l], mla_w_uq[l], mla_w_ukv[l],
                        mla_q_norm[l], mla_k_norm[l], cos_m, sin_m)
        yc = _rwkv_mixer(zb[..., :ZB_DQ], lc, rw_mu[l], rw_w0[l], rw_w2[l], rw_a0[l], rw_a2[l],
                         rw_g2[l], rw_k_k[l], rw_k_a[l], rw_r_k[l], rw_ln_g[l], rw_ln_b[l])
        yd = _diff_mixer(zb[..., ZB_DQ:ZB_DK], zb[..., ZB_DK:ZB_DV], zb[..., ZB_DV:], lc,
                         diff_q_norm[l], diff_k_norm[l], diff_lambda[l], diff_sub_norm[l], lam_init,
                         cos_d, sin_d)
        flat = lambda y: y.reshape(b * ntok, GROUP_W)
        xs = out_proj((ya, flat(yb), flat(yc), flat(yd)), w_out[l].astype(BF16), xs, mods[:, 2], lc)

        h2 = normmod(xs, norm2_g[l], mods[:, 3], mods[:, 4], lc)
        wrt = jnp.concatenate([moe_wr_coarse[l], moe_wr_fine[l],
                               jnp.zeros((d, 128 - MOE_GROUPS - MOE_EXPERTS), F32)], axis=1)
        experts = (moe_w_gate, moe_w_up, moe_w_down, l)
        if last:
            y = _hier_moe(h2[:, lc:].reshape(b * n, d), wrt, moe_br_coarse[l], moe_br_fine[l], *experts)
            return xs[:, lc:] + mods[:, 5, 1:2] * y.reshape(b, n, d)
        y = _hier_moe(h2.reshape(b * ntok, d), wrt, moe_br_coarse[l], moe_br_fine[l], *experts)
        xs = xs + seg(mods[:, 5]) * y.reshape(b, ntok, d)
    return xs[:, lc:]
```

```python
import functools
import math

import jax
import jax.numpy as jnp
from jax import lax
from jax.experimental import pallas as pl
from jax.experimental.pallas import tpu as pltpu

F32 = jnp.float32
BF16 = jnp.bfloat16

D_MODEL = 4096
N_MOD = 6
GROUP_W = D_MODEL // 4
GRID_W = 64
ROPE_BASE = 10000.0
NORM_EPS = 1e-6

S5_GROUP_CH = 16
S5_GROUPS = GROUP_W // S5_GROUP_CH
S5_STATE = 64

MLA_HEADS = 8
MLA_NOPE = 128
MLA_ROPE = 64
MLA_V = GROUP_W // MLA_HEADS
MLA_Q_RANK = 512
MLA_KV_RANK = 256

RW_HEAD = 64
RW_HEADS = GROUP_W // RW_HEAD
RW_W_RANK = 64
RW_A_RANK = 64
RW_G_RANK = 128
RW_GN_EPS = 64e-5
RW_IN = 3 * GROUP_W + RW_W_RANK + RW_A_RANK + RW_G_RANK

DIFF_HEAD = 64
DIFF_HEADS = GROUP_W // (2 * DIFF_HEAD)

MOE_GROUPS = 4
MOE_PER_GROUP = 8
MOE_EXPERTS = MOE_GROUPS * MOE_PER_GROUP
MOE_TOP_K = 2
MOE_BLOCK = 128

ZA_CQ = GROUP_W
ZA_CKV = ZA_CQ + MLA_Q_RANK
ZA_KR = ZA_CKV + MLA_KV_RANK
ZA_SPLIT = ZA_KR + MLA_ROPE
ZA_WIDTH = ZA_KR + 128
ZB_DQ = RW_IN
ZB_DK = ZB_DQ + GROUP_W
ZB_DV = ZB_DK + GROUP_W
ZB_WIDTH = ZB_DV + GROUP_W

VMEM_LIMIT = 56 * 1024 * 1024


def _cparams(sem):
    return pltpu.CompilerParams(dimension_semantics=sem, vmem_limit_bytes=VMEM_LIMIT)


def _pick(n, prefs):
    for p in prefs:
        if n % p == 0:
            return p
    return n


def _mm_kernel(a_ref, b_ref, o_ref):
    a = a_ref[...].astype(BF16)
    b = b_ref[...].astype(BF16)
    o_ref[...] = jnp.dot(a, b, preferred_element_type=F32).astype(o_ref.dtype)


def matmul(a, b, out_dtype=F32, tm=None, tn=None, layer=None, n_cols=None):
    m, k = a.shape
    n = n_cols or b.shape[-1]
    tm = tm or _pick(m, (512, 256, 128))
    tn = tn or _pick(n, (1024, 768, 512, 256, 128))
    if layer is None:
        b_spec = pl.BlockSpec((k, tn), lambda i, j: (0, j))
    else:
        b_spec = pl.BlockSpec((None, k, tn), lambda i, j: (layer, 0, j))
    return pl.pallas_call(
        _mm_kernel,
        grid=(m // tm, n // tn),
        in_specs=[pl.BlockSpec((tm, k), lambda i, j: (i, 0)), b_spec],
        out_specs=pl.BlockSpec((tm, tn), lambda i, j: (i, j)),
        out_shape=jax.ShapeDtypeStruct((m, n), out_dtype),
        compiler_params=_cparams(("parallel", "parallel")),
        name="matmul",
    )(a, b)


def mod_matmul(act, mod_w, tn=1024):
    m, k = act.shape
    depth, _, n = mod_w.shape
    return pl.pallas_call(
        _mm_kernel,
        grid=(depth, n // tn),
        in_specs=[pl.BlockSpec((m, k), lambda l, j: (0, 0)),
                  pl.BlockSpec((None, k, tn), lambda l, j: (l, 0, j))],
        out_specs=pl.BlockSpec((None, m, tn), lambda l, j: (l, 0, j)),
        out_shape=jax.ShapeDtypeStruct((depth, m, n), F32),
        compiler_params=_cparams(("parallel", "parallel")),
        name="mod_matmul",
    )(act, mod_w)


def _normmod_kernel(x_ref, g_ref, sh_ref, sc_ref, o_ref):
    x = x_ref[0]
    ms = jnp.mean(x * x, axis=-1, keepdims=True)
    y = x * lax.rsqrt(ms + NORM_EPS) * g_ref[...]
    o_ref[0] = (y * (1.0 + sc_ref[0]) + sh_ref[0]).astype(o_ref.dtype)


def normmod(xs, g, shift, scale, lc, out_dtype=BF16):
    b, n, d = xs.shape
    tb = lc
    sh = shift.reshape(b * 2, 1, d)
    sc = scale.reshape(b * 2, 1, d)
    seg = lambda bi, i: (bi * 2 + jnp.minimum(i, 1), 0, 0)
    return pl.pallas_call(
        _normmod_kernel,
        grid=(b, n // tb),
        in_specs=[pl.BlockSpec((1, tb, d), lambda bi, i: (bi, i, 0)),
                  pl.BlockSpec((1, d), lambda bi, i: (0, 0)),
                  pl.BlockSpec((1, 1, d), seg),
                  pl.BlockSpec((1, 1, d), seg)],
        out_specs=pl.BlockSpec((1, tb, d), lambda bi, i: (bi, i, 0)),
        out_shape=jax.ShapeDtypeStruct((b, n, d), out_dtype),
        compiler_params=_cparams(("parallel", "parallel")),
        name="normmod",
    )(xs, g.reshape(1, d), sh, sc)


def _out_proj_kernel(ya_ref, yb_ref, yc_ref, yd_ref, w_ref, x_ref, g_ref, o_ref, *, lc, blocks_per_seq):
    acc = jnp.dot(ya_ref[...], w_ref[0:GROUP_W], preferred_element_type=F32)
    acc += jnp.dot(yb_ref[...], w_ref[GROUP_W:2 * GROUP_W], preferred_element_type=F32)
    acc += jnp.dot(yc_ref[...], w_ref[2 * GROUP_W:3 * GROUP_W], preferred_element_type=F32)
    acc += jnp.dot(yd_ref[...], w_ref[3 * GROUP_W:4 * GROUP_W], preferred_element_type=F32)
    tm = acc.shape[0]
    tok = lax.broadcasted_iota(jnp.int32, (tm, 1), 0) + (pl.program_id(0) % blocks_per_seq) * tm
    gate = jnp.where(tok < lc, g_ref[0:1], g_ref[1:2])
    o_ref[...] = x_ref[...] + gate * acc


def out_proj(groups, w, xs, gate, lc):
    b, n, d = xs.shape
    tm = _pick(n, (768, 384, 256, 128))
    tn = 1024
    bps = n // tm
    row = pl.BlockSpec((tm, GROUP_W), lambda i, j: (i, 0))
    kern = functools.partial(_out_proj_kernel, lc=lc, blocks_per_seq=bps)
    out = pl.pallas_call(
        kern,
        grid=(b * bps, d // tn),
        in_specs=[row, row, row, row,
                  pl.BlockSpec((4 * GROUP_W, tn), lambda i, j: (0, j)),
                  pl.BlockSpec((tm, tn), lambda i, j: (i, j)),
                  pl.BlockSpec((None, 2, tn), lambda i, j: (i // bps, 0, j))],
        out_specs=pl.BlockSpec((tm, tn), lambda i, j: (i, j)),
        out_shape=jax.ShapeDtypeStruct((b * n, d), F32),
        compiler_params=_cparams(("parallel", "parallel")),
        name="out_proj",
    )(*groups, w, xs.reshape(b * n, d), gate)
    return out.reshape(b, n, d)


HEAD_TILE = 128


def _softmax_pv(q, k, v):
    s = lax.dot_general(q, k, (((1,), (1,)), ((), ())), preferred_element_type=F32)
    p = jnp.exp(s - jnp.max(s, axis=-1, keepdims=True))
    o = jnp.dot(p.astype(BF16), v, preferred_element_type=F32)
    return o, jnp.sum(p, axis=-1, keepdims=True)


def _per_segment(attend, lc, n):
    i = pl.program_id(2)

    @pl.when(i == 0)
    def _():
        attend(lc)

    @pl.when(i > 0)
    def _():
        attend(n)


ATT_HEADS = 2
ATT_W = ATT_HEADS * HEAD_TILE


def _head(x, j, width=HEAD_TILE, stride=HEAD_TILE):
    return x[:, j * stride:j * stride + width]


def _mla_attn_kernel(qn_ref, qr_ref, kv_ref, kr_ref, o_ref, *, lc):
    def attend(nk):
        kv, kr = kv_ref[0, :nk], kr_ref[0, :nk]
        for j in range(ATT_HEADS):
            q = jnp.concatenate([_head(qn_ref[0], j), _head(qr_ref[0], j)], axis=1)
            k = jnp.concatenate([_head(kv, 2 * j), kr], axis=1)
            o, l = _softmax_pv(q, k, _head(kv, 2 * j + 1))
            o_ref[0, :, j * HEAD_TILE:(j + 1) * HEAD_TILE] = (o / l).astype(o_ref.dtype)

    _per_segment(attend, lc, kv_ref.shape[1])


def mla_attention(q, kv, kr, lc):
    b, n, w = q.shape
    groups = w // (2 * ATT_W)
    tq = lc
    kern = functools.partial(_mla_attn_kernel, lc=lc)
    qspec = lambda off: pl.BlockSpec((1, tq, ATT_W), lambda bi, gi, i: (bi, i, off + gi))
    return pl.pallas_call(
        kern,
        grid=(b, groups, n // tq),
        in_specs=[qspec(0), qspec(groups),
                  pl.BlockSpec((1, n, 2 * ATT_W), lambda bi, gi, i: (bi, 0, gi)),
                  pl.BlockSpec((1, n, HEAD_TILE), lambda bi, gi, i: (bi, 0, 0))],
        out_specs=qspec(0),
        out_shape=jax.ShapeDtypeStruct((b, n, w // 2), BF16),
        compiler_params=_cparams(("parallel", "parallel", "arbitrary")),
        name="mla_attention",
    )(q, q, kv, kr)


def _diff_attn_kernel(q_ref, k_ref, v_ref, lam_ref, g_ref, o_ref, *, lc, post):
    lam = lam_ref[...]
    first = lax.broadcasted_iota(jnp.int32, (q_ref.shape[1], HEAD_TILE), 1) < DIFF_HEAD

    def attend(nk):
        for j in range(ATT_HEADS):
            q = _head(q_ref[0], j)
            k, v = _head(k_ref[0, :nk], j), _head(v_ref[0, :nk], j)
            o0, l0 = _softmax_pv(jnp.where(first, q, jnp.zeros_like(q)), k, v)
            o1, l1 = _softmax_pv(jnp.where(first, jnp.zeros_like(q), q), k, v)
            o = o0 / l0 - o1 * (lam[:, :1] / l1)
            ms = jnp.mean(o * o, axis=-1, keepdims=True)
            o_ref[0, :, j * HEAD_TILE:(j + 1) * HEAD_TILE] = (
                o * lax.rsqrt(ms + NORM_EPS) * g_ref[...] * post).astype(o_ref.dtype)

    _per_segment(attend, lc, v_ref.shape[1])


def diff_attention(q, k, v, lam, sub_norm, lc, post):
    b, n, w = q.shape
    tq = lc
    kern = functools.partial(_diff_attn_kernel, lc=lc, post=post)
    qspec = pl.BlockSpec((1, tq, ATT_W), lambda bi, gi, i: (bi, i, gi))
    kspec = pl.BlockSpec((1, n, ATT_W), lambda bi, gi, i: (bi, 0, gi))
    vec = pl.BlockSpec((1, HEAD_TILE), lambda bi, gi, i: (0, 0))
    return pl.pallas_call(
        kern,
        grid=(b, w // ATT_W, n // tq),
        in_specs=[qspec, kspec, kspec, vec, vec],
        out_specs=qspec,
        out_shape=jax.ShapeDtypeStruct((b, n, w), BF16),
        compiler_params=_cparams(("parallel", "parallel", "arbitrary")),
        name="diff_attention",
    )(q, k, v, jnp.broadcast_to(lam.reshape(1, 1), (1, HEAD_TILE)).astype(F32),
      sub_norm.reshape(1, HEAD_TILE))


S5_TB = 64
S5_GPT = 128 // S5_GROUP_CH
S5_TILES_IN = GROUP_W // 128
S5_TILES_ST = S5_GROUPS * S5_STATE // 128
S5_ST_PER_IN = S5_TILES_ST // S5_TILES_IN


def _s5_kernel(uf_ref, ub_ref, wd_ref, wr_ref, tab_ref, yf_ref, yb_ref, bu_re, bu_im, st_re, st_im):
    nb, tb, _ = uf_ref.shape
    rows = nb * tb
    half = S5_ST_PER_IN * 128

    @pl.when(pl.program_id(0) == 0)
    def _():
        st_re[...] = jnp.zeros_like(st_re)
        st_im[...] = jnp.zeros_like(st_im)

    for d, u_ref in enumerate((uf_ref, ub_ref)):
        x = u_ref[...].reshape(rows, GROUP_W).astype(BF16)
        for jt in range(S5_TILES_IN):
            bu = jnp.dot(x[:, jt * 128:(jt + 1) * 128], wd_ref[d, jt], preferred_element_type=F32)
            for c in range(S5_ST_PER_IN):
                bu_re[jt * S5_ST_PER_IN + c, d * rows:(d + 1) * rows, :] = bu[:, c * 128:(c + 1) * 128]
                bu_im[jt * S5_ST_PER_IN + c, d * rows:(d + 1) * rows, :] = bu[:, half + c * 128:half + (c + 1) * 128]

    def lane_tile(c, carry):
        for d in range(2):
            levels = [(tab_ref[d, 2 * k, c], tab_ref[d, 2 * k + 1, c], (1 << k) if d == 0 else 8 - (1 << k))
                      for k in range(3)]
            cr, ci = tab_ref[d, 6, c], tab_ref[d, 7, c]
            tiles = range(tb // 8) if d == 0 else range(tb // 8 - 1, -1, -1)
            for bi in range(nb):
                row = d * nb + bi
                car_r = st_re[c, row:row + 1, :]
                car_i = st_im[c, row:row + 1, :]
                for tt in tiles:
                    off = row * tb + tt * 8
                    hr = bu_re[c, off:off + 8, :]
                    hi = bu_im[c, off:off + 8, :]
                    for ar, ai, shift in levels:
                        pr = pltpu.roll(hr, shift, axis=0)
                        pi = pltpu.roll(hi, shift, axis=0)
                        hr, hi = hr + (ar * pr - ai * pi), hi + (ar * pi + ai * pr)
                    hr, hi = hr + (cr * car_r - ci * car_i), hi + (cr * car_i + ci * car_r)
                    bu_re[c, off:off + 8, :] = hr
                    bu_im[c, off:off + 8, :] = hi
                    last = 7 if d == 0 else 0
                    car_r, car_i = hr[last:last + 1], hi[last:last + 1]
                st_re[c, row:row + 1, :] = car_r
                st_im[c, row:row + 1, :] = car_i
        return carry

    lax.fori_loop(0, S5_TILES_ST, lane_tile, 0)

    for d, y_ref in enumerate((yf_ref, yb_ref)):
        for jt in range(S5_TILES_IN):
            tiles = [bu_re[jt * S5_ST_PER_IN + c, d * rows:(d + 1) * rows, :] for c in range(S5_ST_PER_IN)]
            tiles += [bu_im[jt * S5_ST_PER_IN + c, d * rows:(d + 1) * rows, :] for c in range(S5_ST_PER_IN)]
            h = jnp.concatenate(tiles, axis=1).astype(BF16)
            y = jnp.dot(h, wr_ref[d, jt], preferred_element_type=F32)
            y_ref[:, :, jt * 128:(jt + 1) * 128] = y.reshape(nb, tb, 128)


def _rev_block(i, nlc, ntot):
    return jnp.where(i < nlc, nlc - 1 - i, ntot - 1 - (i - nlc))


def s5_scan(z, lc, wd, wr, tab):
    b, n, _ = z.shape
    tb = min(S5_TB, lc)
    nlc, ntot = lc // tb, n // tb
    blk = (b, tb, GROUP_W)
    full = lambda a: pl.BlockSpec(a.shape, lambda i: (0,) * a.ndim)
    return pl.pallas_call(
        _s5_kernel,
        grid=(ntot,),
        in_specs=[pl.BlockSpec(blk, lambda i: (0, i, 0)),
                  pl.BlockSpec(blk, lambda i: (0, _rev_block(i, nlc, ntot), 0)),
                  full(wd), full(wr), full(tab)],
        out_specs=[pl.BlockSpec(blk, lambda i: (0, i, 0)),
                   pl.BlockSpec(blk, lambda i: (0, _rev_block(i, nlc, ntot), 0))],
        out_shape=[jax.ShapeDtypeStruct((b, n, GROUP_W), F32)] * 2,
        scratch_shapes=[pltpu.VMEM((S5_TILES_ST, 2 * b * tb, 128), F32)] * 2
        + [pltpu.VMEM((S5_TILES_ST, 8, 128), F32)] * 2,
        compiler_params=_cparams(("arbitrary",)),
        name="s5_scan",
    )(z, z, wd, wr, tab)


def _cmul(ar, ai, br, bi):
    return ar * br - ai * bi, ar * bi + ai * br


def s5_operators(a_re, a_im, log_step, b_re, b_im, c_re, c_im, nb):
    step = jnp.exp(log_step)[..., None]
    mag = jnp.exp(a_re * step)
    lr, li = mag * jnp.cos(a_im * step), mag * jnp.sin(a_im * step)
    den = a_re * a_re + a_im * a_im
    nr, ni = lr - 1.0, li
    fr = (nr * a_re + ni * a_im) / den
    fi = (ni * a_re - nr * a_im) / den
    bbr, bbi = _cmul(fr[..., None], fi[..., None], b_re, b_im)
    eye = jnp.eye(S5_GPT, dtype=F32)

    def drive(w):
        w = w.reshape(2, S5_TILES_IN, S5_GPT, S5_STATE, S5_GROUP_CH)
        w = jnp.einsum('dtgph,gk->dtghkp', w, eye)
        return w.reshape(2, S5_TILES_IN, 128, S5_GPT * S5_STATE)

    def readout(w):
        w = w.reshape(2, S5_TILES_IN, S5_GPT, S5_GROUP_CH, S5_STATE)
        w = jnp.einsum('dtghp,gk->dtgpkh', w, eye)
        return w.reshape(2, S5_TILES_IN, S5_GPT * S5_STATE, 128)

    wd = jnp.concatenate([drive(bbr), drive(bbi)], axis=-1).astype(BF16)
    wr = jnp.concatenate([readout(c_re), readout(-c_im)], axis=-2).astype(BF16)

    log_mag = (a_re * step).reshape(2, S5_TILES_ST, 1, 128)
    ang = (a_im * step).reshape(2, S5_TILES_ST, 1, 128)
    r8 = jnp.arange(8, dtype=F32).reshape(1, 1, 8, 1)

    def power(e, keep):
        e = jnp.broadcast_to(jnp.asarray(e, F32), (2, S5_TILES_ST, 8, 128))
        m = jnp.exp(log_mag * e)
        k = jnp.broadcast_to(keep, e.shape)
        return jnp.where(k, m * jnp.cos(ang * e), 0.0), jnp.where(k, m * jnp.sin(ang * e), 0.0)

    is_fwd = jnp.arange(2).reshape(2, 1, 1, 1) == 0
    tabs = []
    for s in (1, 2, 4):
        tabs += power(float(s), jnp.where(is_fwd, r8 >= s, r8 < 8 - s))
    tabs += power(jnp.where(is_fwd, r8 + 1.0, 8.0 - r8), True)
    return wd, wr, jnp.stack(tabs, axis=1)


def _s5_post_kernel(yf_ref, yb_ref, u_ref, d_ref, w_ref, o_ref):
    y = jax.nn.gelu(yf_ref[...] + yb_ref[...] + d_ref[...] * u_ref[...])
    gv = jnp.dot(y.astype(BF16), w_ref[...], preferred_element_type=F32)
    o_ref[...] = (gv[:, :GROUP_W] * jax.nn.sigmoid(gv[:, GROUP_W:])).astype(o_ref.dtype)


def s5_post(yf, yb, z, d_skip, w_glu):
    t = yf.shape[0]
    tm = _pick(t, (512, 256, 128))
    row = pl.BlockSpec((tm, GROUP_W), lambda i: (i, 0))
    return pl.pallas_call(
        _s5_post_kernel,
        grid=(t // tm,),
        in_specs=[row, row, row, pl.BlockSpec((1, GROUP_W), lambda i: (0, 0)),
                  pl.BlockSpec((GROUP_W, 2 * GROUP_W), lambda i: (0, 0))],
        out_specs=row,
        out_shape=jax.ShapeDtypeStruct((t, GROUP_W), BF16),
        compiler_params=_cparams(("parallel",)),
        name="s5_post",
    )(yf, yb, z, d_skip.reshape(1, GROUP_W), w_glu)


RW_TB = 32
RW_ROWS = 8
RW_TILES = GROUP_W // 128


def _split_bf16(x, parts):
    out = []
    for _ in range(parts - 1):
        hi = x.astype(BF16)
        out.append(hi)
        x = x - hi.astype(F32)
    out.append(x.astype(BF16))
    return out


def _permute_rows(p, x, parts):
    return sum(jnp.dot(p, piece, preferred_element_type=F32) for piece in _split_bf16(x, parts))


def _rwkv_to_lanes(fwd_ref, bwd_ref, p_ref, xt_s, dst_ref, parts):
    nb, tb, _ = fwd_ref.shape
    x = jnp.concatenate([fwd_ref[...].reshape(nb * tb, GROUP_W), bwd_ref[...].reshape(nb * tb, GROUP_W)], axis=0)
    xt_s[...] = _permute_rows(p_ref[...], x, parts)
    low = lax.broadcasted_iota(jnp.int32, (RW_HEAD, 128), 1) < RW_HEAD

    def pair(tp, carry):
        z = xt_s[pl.ds(pl.multiple_of(tp * 2 * RW_ROWS, 2 * RW_ROWS), 2 * RW_ROWS), :]
        zz = jnp.concatenate([z[p * RW_ROWS:(p + 1) * RW_ROWS, jt * 128:(jt + 1) * 128]
                              for p in range(2) for jt in range(RW_TILES)], axis=0)
        tt = zz.T
        top, bot = tt[:RW_HEAD], tt[RW_HEAD:]
        dst_ref[2 * tp] = jnp.where(low, top, pltpu.roll(bot, RW_HEAD, axis=1))
        dst_ref[2 * tp + 1] = jnp.where(low, pltpu.roll(top, RW_HEAD, axis=1), bot)
        return carry

    lax.fori_loop(0, tb // 2, pair, 0, unroll=8)


def _rwkv_from_lanes(y_s, pt_ref, xt_s, of_ref, ob_ref):
    nb, tb, _ = of_ref.shape
    low = lax.broadcasted_iota(jnp.int32, (RW_HEAD, 128), 1) < RW_HEAD

    def pair(tp, carry):
        y0, y1 = y_s[2 * tp], y_s[2 * tp + 1]
        top = jnp.where(low, y0, pltpu.roll(y1, RW_HEAD, axis=1))
        bot = jnp.where(low, pltpu.roll(y0, RW_HEAD, axis=1), y1)
        zz = jnp.concatenate([top, bot], axis=0).T
        base = pl.multiple_of(tp * 2 * RW_ROWS, 2 * RW_ROWS)
        for p in range(2):
            for jt in range(RW_TILES):
                blk = zz[(p * RW_TILES + jt) * RW_ROWS:(p * RW_TILES + jt + 1) * RW_ROWS]
                xt_s[pl.ds(base + p * RW_ROWS, RW_ROWS), jt * 128:(jt + 1) * 128] = blk
        return carry

    lax.fori_loop(0, tb // 2, pair, 0, unroll=8)
    y = _permute_rows(pt_ref[...], xt_s[...], 2)
    of_ref[...] = y[:nb * tb].reshape(nb, tb, GROUP_W)
    ob_ref[...] = y[nb * tb:].reshape(nb, tb, GROUP_W)


def _rwkv_kernel(rf_ref, kf_ref, vf_ref, wf_ref, af_ref, rb_ref, kb_ref, vb_ref, wb_ref, ab_ref,
                 p_ref, pt_ref, c_ref, of_ref, ob_ref,
                 r_s, k_s, v_s, w_s, a_s, y_s, xt_s, kk_s, bb_s, s_ref):
    tb = rf_ref.shape[1]

    @pl.when(pl.program_id(0) == 0)
    def _():
        s_ref[...] = jnp.zeros_like(s_ref)

    _rwkv_to_lanes(rf_ref, rb_ref, p_ref, xt_s, r_s, 2)
    _rwkv_to_lanes(kf_ref, kb_ref, p_ref, xt_s, k_s, 2)
    _rwkv_to_lanes(vf_ref, vb_ref, p_ref, xt_s, v_s, 2)
    _rwkv_to_lanes(wf_ref, wb_ref, p_ref, xt_s, w_s, 3)
    _rwkv_to_lanes(af_ref, ab_ref, p_ref, xt_s, a_s, 2)

    def step(t, carry):
        k = k_s[t]
        kk = k * c_ref[2]
        kk = kk * lax.rsqrt(jnp.maximum(jnp.sum(kk * kk, axis=0, keepdims=True), 1e-12))
        z = -(w_s[t] + c_ref[0])
        w = -(jnp.maximum(z, 0.0) + jnp.log(1.0 + jnp.exp(-jnp.abs(z)))) - 0.5
        a = jax.nn.sigmoid(a_s[t] + c_ref[1])
        kd = k * (1.0 + (a - 1.0) * c_ref[3])
        bonus = jnp.sum(r_s[t] * kd * c_ref[4], axis=0, keepdims=True)
        w_s[t] = jnp.exp(-jnp.exp(w))
        k_s[t] = kd
        kk_s[...] = kk
        bb_s[...] = kk * a

        def vgroup(vg, carry2):
            off = pl.multiple_of(vg * 8, 8)
            vrows = v_s[t, pl.ds(off, 8), :]
            outs = []
            for j in range(8):
                sv = s_ref[off + j]
                sa = jnp.sum(sv * kk_s[...], axis=0, keepdims=True)
                sn = sv * w_s[t] - sa * bb_s[...] + vrows[j:j + 1] * k_s[t]
                s_ref[off + j] = sn
                outs.append(jnp.sum(sn * r_s[t], axis=0, keepdims=True))
            y_s[t, pl.ds(off, 8), :] = jnp.concatenate(outs, axis=0) + bonus * vrows
            return carry2

        return lax.fori_loop(0, RW_HEAD // 8, vgroup, carry)

    lax.fori_loop(0, tb, step, 0)
    _rwkv_from_lanes(y_s, pt_ref, xt_s, of_ref, ob_ref)


def rwkv_scan(zs, lr, consts, lc):
    b, n, _ = zs.shape
    assert 2 * b == RW_ROWS
    tb = RW_TB
    nlc, ntot = lc // tb, n // tb
    rows = RW_ROWS * tb
    r_out = jnp.arange(rows)
    step, db = r_out // RW_ROWS, r_out % RW_ROWS
    src = db * tb + jnp.where(db < b, step, tb - 1 - step)
    perm = (src[:, None] == jnp.arange(rows)[None, :]).astype(BF16)
    blk = (b, tb, GROUP_W)
    fwd = lambda col: pl.BlockSpec(blk, lambda i: (0, i, col))
    bwd = lambda col: pl.BlockSpec(blk, lambda i: (0, _rev_block(i, nlc, ntot), col))
    full = lambda a: pl.BlockSpec(a.shape, lambda i: (0,) * a.ndim)
    tile = pltpu.VMEM((tb, RW_HEAD, 128), F32)
    return pl.pallas_call(
        _rwkv_kernel,
        grid=(ntot,),
        in_specs=[fwd(0), fwd(1), fwd(2), fwd(0), fwd(2), bwd(0), bwd(1), bwd(2), bwd(1), bwd(3),
                  full(perm), full(perm), full(consts)],
        out_specs=[fwd(0), bwd(0)],
        out_shape=[jax.ShapeDtypeStruct((b, n, GROUP_W), F32)] * 2,
        scratch_shapes=[tile] * 6 + [pltpu.VMEM((rows, GROUP_W), F32),
                                     pltpu.VMEM((RW_HEAD, 128), F32), pltpu.VMEM((RW_HEAD, 128), F32),
                                     pltpu.VMEM((RW_HEAD, RW_HEAD, 128), F32)],
        compiler_params=_cparams(("arbitrary",)),
        name="rwkv_scan",
    )(zs, zs, zs, lr, lr, zs, zs, zs, lr, lr, perm, perm.T, consts)


def _moe_kernel(be_ref, nu_ref, x_ref, wg_ref, wu_ref, wd_ref, o_ref, g_s, u_s, d_s):
    i = pl.program_id(0)
    used = i < nu_ref[0]
    changed = jnp.logical_or(i == 0, be_ref[i] != be_ref[jnp.maximum(i - 1, 0)])

    @pl.when(jnp.logical_and(used, changed))
    def _():
        g_s[...] = wg_ref[...].astype(BF16)
        u_s[...] = wu_ref[...].astype(BF16)
        d_s[...] = wd_ref[...].astype(BF16)

    @pl.when(used)
    def _():
        x = x_ref[...]
        g = jnp.dot(x, g_s[...], preferred_element_type=F32)
        u = jnp.dot(x, u_s[...], preferred_element_type=F32)
        h = (g * jax.nn.sigmoid(g) * u).astype(BF16)
        o_ref[...] = jnp.dot(h, d_s[...], preferred_element_type=F32)

    @pl.when(jnp.logical_not(used))
    def _():
        o_ref[...] = jnp.zeros_like(o_ref)


def moe_ffn(block_exp, n_used, xb, wg, wu, wd, layer):
    cap, d = xb.shape
    hid = wg.shape[-1]
    nb = cap // MOE_BLOCK
    once = pl.Buffered(1)
    grid_spec = pltpu.PrefetchScalarGridSpec(
        num_scalar_prefetch=2,
        grid=(nb,),
        in_specs=[pl.BlockSpec((MOE_BLOCK, d), lambda i, be, nu: (i, 0)),
                  pl.BlockSpec((None, None, d, hid), lambda i, be, nu: (layer, be[i], 0, 0), pipeline_mode=once),
                  pl.BlockSpec((None, None, d, hid), lambda i, be, nu: (layer, be[i], 0, 0), pipeline_mode=once),
                  pl.BlockSpec((None, None, hid, d), lambda i, be, nu: (layer, be[i], 0, 0), pipeline_mode=once)],
        out_specs=pl.BlockSpec((MOE_BLOCK, d), lambda i, be, nu: (i, 0)),
        scratch_shapes=[pltpu.VMEM((d, hid), BF16), pltpu.VMEM((d, hid), BF16), pltpu.VMEM((hid, d), BF16)],
    )
    return pl.pallas_call(
        _moe_kernel,
        grid_spec=grid_spec,
        out_shape=jax.ShapeDtypeStruct((cap, d), F32),
        compiler_params=_cparams(("arbitrary",)),
        name="moe_ffn",
    )(block_exp, n_used, xb, wg, wu, wd)


def _rms(x, g):
    return x * lax.rsqrt(jnp.mean(x * x, axis=-1, keepdims=True) + NORM_EPS) * g


def _axial_rope(rows, dim):
    n_freq = dim // 4
    inv = ROPE_BASE ** (-jnp.arange(n_freq, dtype=F32) / n_freq)
    row = jnp.repeat(jnp.arange(rows, dtype=F32), GRID_W)
    col = jnp.tile(jnp.arange(GRID_W, dtype=F32), rows)
    ang = jnp.concatenate([row[:, None] * inv, col[:, None] * inv], axis=-1)
    return jnp.cos(ang), jnp.sin(ang)


def _rope(x, cos, sin):
    half = x.shape[-1] // 2
    x1, x2 = x[..., :half], x[..., half:]
    return jnp.concatenate([x1 * cos - x2 * sin, x1 * sin + x2 * cos], axis=-1)


def _rope_lat(x, cos, sin, lc):
    extra = (None,) * (x.ndim - 3)
    cs = cos[(slice(None),) + extra]
    sn = sin[(slice(None),) + extra]
    return jnp.concatenate([x[:, :lc], _rope(x[:, lc:], cs, sn)], axis=1)


def _seq_rev(t, lc):
    return jnp.concatenate([jnp.flip(t[:, :lc], 1), jnp.flip(t[:, lc:], 1)], axis=1)


def _mm3(x, w, **kw):
    b, n, k = x.shape
    return matmul(x.reshape(b * n, k), w, **kw).reshape(b, n, -1)


def _mla_mixer(cq, ckv, krope, lc, q_a_norm, kv_a_norm, w_uq, w_ukv, q_norm, k_norm, cos, sin):
    b, n, _ = cq.shape
    pad = HEAD_TILE - MLA_ROPE
    wq = w_uq.reshape(MLA_Q_RANK, MLA_HEADS, MLA_NOPE + MLA_ROPE)
    wq = jnp.concatenate([wq[:, :, :MLA_NOPE].reshape(MLA_Q_RANK, -1),
                          jnp.pad(wq[:, :, MLA_NOPE:], ((0, 0), (0, 0), (0, pad))).reshape(MLA_Q_RANK, -1)], axis=1)
    q = _mm3(_rms(cq, q_a_norm), wq).reshape(b, n, 2, MLA_HEADS, HEAD_TILE)
    kv = _mm3(_rms(ckv, kv_a_norm), w_ukv).reshape(b, n, MLA_HEADS, MLA_NOPE + MLA_V)
    scale = float(MLA_NOPE + MLA_ROPE) ** -0.5
    q_nope = _rms(q[:, :, 0], q_norm[:MLA_NOPE]) * scale
    q_rope = _rope_lat(_rms(q[:, :, 1, :, :MLA_ROPE], q_norm[MLA_NOPE:]), cos, sin, lc) * scale
    q_rope = jnp.pad(q_rope, ((0, 0), (0, 0), (0, 0), (0, pad)))
    k_nope = _rms(kv[..., :MLA_NOPE], k_norm[:MLA_NOPE])
    k_rope = _rope_lat(_rms(krope, k_norm[MLA_NOPE:]), cos, sin, lc)
    qf = jnp.stack([q_nope, q_rope], axis=2).reshape(b, n, 2 * MLA_HEADS * HEAD_TILE).astype(BF16)
    kvf = jnp.concatenate([k_nope, kv[..., MLA_NOPE:]], axis=-1).reshape(b, n, -1).astype(BF16)
    return mla_attention(qf, kvf, jnp.pad(k_rope, ((0, 0), (0, 0), (0, pad))).astype(BF16), lc)


def _diff_mixer(dq, dk, dv, lc, q_norm, k_norm, lam_vecs, sub_norm, lam_init, cos, sin):
    b, n, _ = dq.shape
    q = _rope_lat(_rms(dq.reshape(b, n, DIFF_HEADS, 2, DIFF_HEAD), q_norm), cos, sin, lc) * float(DIFF_HEAD) ** -0.5
    k = _rope_lat(_rms(dk.reshape(b, n, DIFF_HEADS, 2, DIFF_HEAD), k_norm), cos, sin, lc)
    lam = jnp.exp(jnp.sum(lam_vecs[0] * lam_vecs[1])) - jnp.exp(jnp.sum(lam_vecs[2] * lam_vecs[3])) + lam_init
    flat = lambda x: x.reshape(b, n, GROUP_W).astype(BF16)
    return diff_attention(flat(q), flat(k), dv.astype(BF16), lam, sub_norm, lc, 1.0 - lam_init)


def _centred_shift(z, mu_prev, mu_next):
    z_prev = jnp.pad(z, ((0, 0), (1, 0), (0, 0)))[:, :-1]
    z_next = jnp.pad(z, ((0, 0), (0, 1), (0, 0)))[:, 1:]
    return z + mu_prev * (z_prev - z) + mu_next * (z_next - z)


def _rwkv_mixer(z, lc, mu, w0, w2, a0, a2, g2, k_k, k_a, r_k, ln_g, ln_b):
    b, n, _ = z.shape
    heads = lambda t: t.reshape(b, n, RW_HEADS, RW_HEAD)
    z = jnp.concatenate([_centred_shift(z[:, :lc], mu[0], mu[1]),
                         _centred_shift(z[:, lc:], mu[0], mu[1])], axis=1)
    o = 3 * GROUP_W
    wl, al, gl = z[..., o:o + 64], z[..., o + 64:o + 128], z[..., o + 128:o + 256]
    lr_in = jnp.concatenate([jnp.tanh(wl), al, jax.nn.sigmoid(gl)], axis=-1)
    zw = jnp.zeros((RW_W_RANK, GROUP_W), F32)
    zg = jnp.zeros((RW_G_RANK, GROUP_W), F32)
    w_lr = jnp.concatenate([
        jnp.concatenate([w2[0], w2[1], zw, zw, zw], axis=1),
        jnp.concatenate([zw, zw, a2[0], a2[1], zw], axis=1),
        jnp.concatenate([zg, zg, zg, zg, g2], axis=1)], axis=0)
    lr = _mm3(lr_in, w_lr)
    g = lr[..., 4 * GROUP_W:]

    def chain_tile(p_fwd, p_bwd):
        per_dir = [p.reshape(RW_TILES, 2, RW_HEAD).transpose(2, 1, 0) for p in (p_fwd, p_bwd)]
        return jnp.stack([per_dir[0]] * b + [per_dir[1]] * b, axis=-1).reshape(RW_HEAD, 128)

    consts = jnp.stack([chain_tile(w0[0], w0[1]), chain_tile(a0[0], a0[1]), chain_tile(k_k, k_k),
                        chain_tile(k_a, k_a), chain_tile(r_k.reshape(-1), r_k.reshape(-1))])
    o_f, o_b = rwkv_scan(z, lr, consts, lc)
    o = heads(o_f + o_b)
    mean = jnp.mean(o, axis=-1, keepdims=True)
    var = jnp.mean(jnp.square(o - mean), axis=-1, keepdims=True)
    o = ((o - mean) * lax.rsqrt(var + RW_GN_EPS)).reshape(b, n, GROUP_W)
    return ((o * ln_g + ln_b) * g).astype(BF16)


def _hier_moe(h, wr, br_coarse, br_fine, wg, wu, wd, layer):
    t, d = h.shape
    logits = matmul(h, wr)
    p_group = jax.nn.softmax(logits[:, :MOE_GROUPS] + br_coarse, axis=-1)
    group = jnp.argmax(p_group, axis=-1).astype(jnp.int32)
    p_sel = jnp.take_along_axis(p_group, group[:, None], axis=-1)
    fine = (logits[:, MOE_GROUPS:MOE_GROUPS + MOE_EXPERTS] + br_fine).reshape(t, MOE_GROUPS, MOE_PER_GROUP)
    fine = jnp.take_along_axis(fine, group[:, None, None], axis=1)[:, 0]
    top_v, top_i = lax.top_k(fine, MOE_TOP_K)
    weight = p_sel * jax.nn.softmax(top_v, axis=-1)
    expert = group[:, None] * MOE_PER_GROUP + top_i.astype(jnp.int32)
    flat_e = expert.reshape(-1)
    n_assign = flat_e.shape[0]
    onehot = (flat_e[:, None] == jnp.arange(MOE_EXPERTS, dtype=jnp.int32)[None, :]).astype(jnp.int32)
    running = jnp.cumsum(onehot, axis=0)
    counts = running[-1]
    padded = (counts + MOE_BLOCK - 1) // MOE_BLOCK * MOE_BLOCK
    pad_end = jnp.cumsum(padded)
    pad_start = pad_end - padded
    dest = jnp.sum(onehot * (running - 1 + pad_start[None, :]), axis=1).astype(jnp.int32)
    n_blocks = -(-(n_assign + MOE_EXPERTS * (MOE_BLOCK - 1)) // MOE_BLOCK)
    cap = n_blocks * MOE_BLOCK
    slot_tok = jnp.full((cap,), t, jnp.int32).at[dest].set(jnp.arange(n_assign, dtype=jnp.int32) // MOE_TOP_K)
    block_start = jnp.arange(n_blocks, dtype=jnp.int32) * MOE_BLOCK
    block_exp = jnp.minimum(jnp.sum((pad_end[None, :] <= block_start[:, None]).astype(jnp.int32), axis=1),
                            MOE_EXPERTS - 1)
    n_used = (pad_end[-1:] // MOE_BLOCK).astype(jnp.int32)
    h_pad = jnp.concatenate([h, jnp.zeros((1, d), h.dtype)], axis=0)
    yb = moe_ffn(block_exp, n_used, h_pad[slot_tok], wg, wu, wd, layer)
    slot_of = dest.reshape(t, MOE_TOP_K)
    return weight, yb[slot_of[:, 0]], yb[slot_of[:, 1]]


def _combine_kernel(x_ref, y0_ref, y1_ref, w_ref, g_ref, o_ref):
    w = w_ref[0]
    o_ref[0] = x_ref[0] + g_ref[0] * (w[:, 0:1] * y0_ref[0] + w[:, 1:2] * y1_ref[0])


def moe_combine(xs, weight, y0, y1, gate, lc, skip):
    b, ntok, d = xs.shape
    r = ntok - skip
    tm, tn = lc, 1024
    off = skip // tm
    blk = lambda idx: pl.BlockSpec((1, tm, tn), idx)
    rows = lambda bi, i, j: (bi, i, j)
    return pl.pallas_call(
        _combine_kernel,
        grid=(b, r // tm, d // tn),
        in_specs=[blk(lambda bi, i, j: (bi, off + i, j)), blk(rows), blk(rows),
                  pl.BlockSpec((1, tm, MOE_TOP_K), lambda bi, i, j: (bi, i, 0)),
                  pl.BlockSpec((1, 1, tn), lambda bi, i, j: (bi * 2 + jnp.minimum(off + i, 1), 0, j))],
        out_specs=blk(rows),
        out_shape=jax.ShapeDtypeStruct((b, r, d), F32),
        compiler_params=_cparams(("parallel", "parallel", "parallel")),
        name="moe_combine",
    )(xs, y0.reshape(b, r, d), y1.reshape(b, r, d), weight.reshape(b, r, MOE_TOP_K), gate.reshape(b * 2, 1, d))


def kernel(x, c, ctx, c_ctx, mod_w, mod_b, norm1_g, norm2_g, w_in, w_out, s5_a_re, s5_a_im, s5_log_step, s5_b_re, s5_b_im, s5_c_re, s5_c_im, s5_d, s5_w_glu, mla_q_a_norm, mla_kv_a_norm, mla_w_uq, mla_w_ukv, mla_q_norm, mla_k_norm, rw_mu, rw_w0, rw_w2, rw_a0, rw_a2, rw_g2, rw_k_k, rw_k_a, rw_r_k, rw_ln_g, rw_ln_b, diff_q_norm, diff_k_norm, diff_lambda, diff_sub_norm, moe_wr_coarse, moe_br_coarse, moe_wr_fine, moe_br_fine, moe_w_gate, moe_w_up, moe_w_down):
    b, n, d = x.shape
    lc = ctx.shape[1]
    ntok = lc + n
    depth = mod_w.shape[0]
    rows = n // GRID_W
    cos_m, sin_m = _axial_rope(rows, MLA_ROPE)
    cos_d, sin_d = _axial_rope(rows, DIFF_HEAD)
    act = jnp.concatenate([jax.nn.silu(c), jax.nn.silu(c_ctx)[None],
                           jnp.zeros((8 - b - 1, d), F32)], axis=0)
    mod_all = mod_matmul(act, mod_w) + mod_b[:, None, :]
    xs = jnp.concatenate([ctx, x], axis=1)

    for l in range(depth):
        last = l == depth - 1
        lam_init = 0.8 - 0.6 * math.exp(-0.3 * l)
        mod_lat = mod_all[l, :b].reshape(b, N_MOD, d)
        mod_ctx = jnp.broadcast_to(mod_all[l, b].reshape(1, N_MOD, d), (b, N_MOD, d))
        mods = jnp.stack([mod_ctx, mod_lat], axis=2)

        h = normmod(xs, norm1_g[l], mods[:, 0], mods[:, 1], lc).reshape(b * ntok, d)
        tm_in = _pick(b * ntok, (1024, 512, 256, 128))
        za = matmul(h, w_in[l][:, :ZA_WIDTH].astype(BF16), tn=ZA_WIDTH).reshape(b, ntok, ZA_WIDTH)
        zb = matmul(h, w_in[l][:, ZA_SPLIT:].astype(BF16), tm=tm_in,
                    tn=_pick(ZB_WIDTH, (1280, 640, 128))).reshape(b, ntok, ZB_WIDTH)

        wd, wr, lam = s5_operators(s5_a_re[l], s5_a_im[l], s5_log_step[l], s5_b_re[l], s5_b_im[l],
                                   s5_c_re[l], s5_c_im[l], b)
        yf, yr = s5_scan(za, lc, wd, wr, lam)
        ya = s5_post(yf.reshape(b * ntok, GROUP_W), yr.reshape(b * ntok, GROUP_W),
                     za.reshape(b * ntok, ZA_WIDTH), s5_d[l], s5_w_glu[l].astype(BF16))
        yb = _mla_mixer(za[..., ZA_CQ:ZA_CKV], za[..., ZA_CKV:ZA_KR], za[..., ZA_KR:ZA_SPLIT], lc,
                        mla_q_a_norm[l], mla_kv_a_norm[l], mla_w_uq[l], mla_w_ukv[l],
                        mla_q_norm[l], mla_k_norm[l], cos_m, sin_m)
        yc = _rwkv_mixer(zb[..., :ZB_DQ], lc, rw_mu[l], rw_w0[l], rw_w2[l], rw_a0[l], rw_a2[l],
                         rw_g2[l], rw_k_k[l], rw_k_a[l], rw_r_k[l], rw_ln_g[l], rw_ln_b[l])
        yd = _diff_mixer(zb[..., ZB_DQ:ZB_DK], zb[..., ZB_DK:ZB_DV], zb[..., ZB_DV:], lc,
                         diff_q_norm[l], diff_k_norm[l], diff_lambda[l], diff_sub_norm[l], lam_init,
                         cos_d, sin_d)
        flat = lambda y: y.reshape(b * ntok, GROUP_W)
        xs = out_proj((ya, flat(yb), flat(yc), flat(yd)), w_out[l].astype(BF16), xs, mods[:, 2], lc)

        h2 = normmod(xs, norm2_g[l], mods[:, 3], mods[:, 4], lc)
        wrt = jnp.concatenate([moe_wr_coarse[l], moe_wr_fine[l],
                               jnp.zeros((d, 128 - MOE_GROUPS - MOE_EXPERTS), F32)], axis=1)
        experts = (moe_w_gate, moe_w_up, moe_w_down, l)
        skip = lc if last else 0
        routed = _hier_moe(h2[:, skip:].reshape(b * (ntok - skip), d), wrt, moe_br_coarse[l], moe_br_fine[l],
                           *experts)
        xs = moe_combine(xs, *routed, mods[:, 5], lc, skip)
    return xs
```

```python
import functools
import math

import jax
import jax.numpy as jnp
from jax import lax
from jax.experimental import pallas as pl
from jax.experimental.pallas import tpu as pltpu

F32 = jnp.float32
BF16 = jnp.bfloat16

D_MODEL = 4096
N_MOD = 6
GROUP_W = D_MODEL // 4
GRID_W = 64
ROPE_BASE = 10000.0
NORM_EPS = 1e-6

S5_GROUP_CH = 16
S5_GROUPS = GROUP_W // S5_GROUP_CH
S5_STATE = 64

MLA_HEADS = 8
MLA_NOPE = 128
MLA_ROPE = 64
MLA_V = GROUP_W // MLA_HEADS
MLA_Q_RANK = 512
MLA_KV_RANK = 256

RW_HEAD = 64
RW_HEADS = GROUP_W // RW_HEAD
RW_W_RANK = 64
RW_A_RANK = 64
RW_G_RANK = 128
RW_GN_EPS = 64e-5
RW_IN = 3 * GROUP_W + RW_W_RANK + RW_A_RANK + RW_G_RANK

DIFF_HEAD = 64
DIFF_HEADS = GROUP_W // (2 * DIFF_HEAD)

MOE_GROUPS = 4
MOE_PER_GROUP = 8
MOE_EXPERTS = MOE_GROUPS * MOE_PER_GROUP
MOE_TOP_K = 2
MOE_BLOCK = 128

ZA_CQ = GROUP_W
ZA_CKV = ZA_CQ + MLA_Q_RANK
ZA_KR = ZA_CKV + MLA_KV_RANK
ZA_SPLIT = ZA_KR + MLA_ROPE
ZA_WIDTH = ZA_KR + 128
ZB_DQ = RW_IN
ZB_DK = ZB_DQ + GROUP_W
ZB_DV = ZB_DK + GROUP_W
ZB_WIDTH = ZB_DV + GROUP_W

VMEM_LIMIT = 56 * 1024 * 1024


def _cparams(sem):
    return pltpu.CompilerParams(dimension_semantics=sem, vmem_limit_bytes=VMEM_LIMIT)


def _pick(n, prefs):
    for p in prefs:
        if n % p == 0:
            return p
    return n


def _mm_kernel(a_ref, b_ref, o_ref):
    a = a_ref[...].astype(BF16)
    b = b_ref[...].astype(BF16)
    o_ref[...] = jnp.dot(a, b, preferred_element_type=F32).astype(o_ref.dtype)


def matmul(a, b, out_dtype=F32, tm=None, tn=None, layer=None, n_cols=None):
    m, k = a.shape
    n = n_cols or b.shape[-1]
    tm = tm or _pick(m, (512, 256, 128))
    tn = tn or _pick(n, (1024, 768, 512, 256, 128))
    if layer is None:
        b_spec = pl.BlockSpec((k, tn), lambda i, j: (0, j))
    else:
        b_spec = pl.BlockSpec((None, k, tn), lambda i, j: (layer, 0, j))
    return pl.pallas_call(
        _mm_kernel,
        grid=(m // tm, n // tn),
        in_specs=[pl.BlockSpec((tm, k), lambda i, j: (i, 0)), b_spec],
        out_specs=pl.BlockSpec((tm, tn), lambda i, j: (i, j)),
        out_shape=jax.ShapeDtypeStruct((m, n), out_dtype),
        compiler_params=_cparams(("parallel", "parallel")),
        name="matmul",
    )(a, b)


def mod_matmul(act, mod_w, tn=1024):
    m, k = act.shape
    depth, _, n = mod_w.shape
    return pl.pallas_call(
        _mm_kernel,
        grid=(depth, n // tn),
        in_specs=[pl.BlockSpec((m, k), lambda l, j: (0, 0)),
                  pl.BlockSpec((None, k, tn), lambda l, j: (l, 0, j))],
        out_specs=pl.BlockSpec((None, m, tn), lambda l, j: (l, 0, j)),
        out_shape=jax.ShapeDtypeStruct((depth, m, n), F32),
        compiler_params=_cparams(("parallel", "parallel")),
        name="mod_matmul",
    )(act, mod_w)


def _normmod_kernel(x_ref, g_ref, sh_ref, sc_ref, o_ref):
    x = x_ref[0]
    ms = jnp.mean(x * x, axis=-1, keepdims=True)
    y = x * lax.rsqrt(ms + NORM_EPS) * g_ref[...]
    o_ref[0] = (y * (1.0 + sc_ref[0]) + sh_ref[0]).astype(o_ref.dtype)


def normmod(xs, g, shift, scale, lc, out_dtype=BF16):
    b, n, d = xs.shape
    tb = lc
    sh = shift.reshape(b * 2, 1, d)
    sc = scale.reshape(b * 2, 1, d)
    seg = lambda bi, i: (bi * 2 + jnp.minimum(i, 1), 0, 0)
    return pl.pallas_call(
        _normmod_kernel,
        grid=(b, n // tb),
        in_specs=[pl.BlockSpec((1, tb, d), lambda bi, i: (bi, i, 0)),
                  pl.BlockSpec((1, d), lambda bi, i: (0, 0)),
                  pl.BlockSpec((1, 1, d), seg),
                  pl.BlockSpec((1, 1, d), seg)],
        out_specs=pl.BlockSpec((1, tb, d), lambda bi, i: (bi, i, 0)),
        out_shape=jax.ShapeDtypeStruct((b, n, d), out_dtype),
        compiler_params=_cparams(("parallel", "parallel")),
        name="normmod",
    )(xs, g.reshape(1, d), sh, sc)


def _out_proj_kernel(ya_ref, yb_ref, yc_ref, yd_ref, w_ref, x_ref, g_ref, o_ref, *, lc, blocks_per_seq):
    acc = jnp.dot(ya_ref[...], w_ref[0:GROUP_W], preferred_element_type=F32)
    acc += jnp.dot(yb_ref[...], w_ref[GROUP_W:2 * GROUP_W], preferred_element_type=F32)
    acc += jnp.dot(yc_ref[...], w_ref[2 * GROUP_W:3 * GROUP_W], preferred_element_type=F32)
    acc += jnp.dot(yd_ref[...], w_ref[3 * GROUP_W:4 * GROUP_W], preferred_element_type=F32)
    tm = acc.shape[0]
    tok = lax.broadcasted_iota(jnp.int32, (tm, 1), 0) + (pl.program_id(0) % blocks_per_seq) * tm
    gate = jnp.where(tok < lc, g_ref[0:1], g_ref[1:2])
    o_ref[...] = x_ref[...] + gate * acc


def out_proj(groups, w, xs, gate, lc):
    b, n, d = xs.shape
    tm = _pick(n, (768, 384, 256, 128))
    tn = 1024
    bps = n // tm
    row = pl.BlockSpec((tm, GROUP_W), lambda i, j: (i, 0))
    kern = functools.partial(_out_proj_kernel, lc=lc, blocks_per_seq=bps)
    out = pl.pallas_call(
        kern,
        grid=(b * bps, d // tn),
        in_specs=[row, row, row, row,
                  pl.BlockSpec((4 * GROUP_W, tn), lambda i, j: (0, j)),
                  pl.BlockSpec((tm, tn), lambda i, j: (i, j)),
                  pl.BlockSpec((None, 2, tn), lambda i, j: (i // bps, 0, j))],
        out_specs=pl.BlockSpec((tm, tn), lambda i, j: (i, j)),
        out_shape=jax.ShapeDtypeStruct((b * n, d), F32),
        compiler_params=_cparams(("parallel", "parallel")),
        name="out_proj",
    )(*groups, w, xs.reshape(b * n, d), gate)
    return out.reshape(b, n, d)


HEAD_TILE = 128


def _softmax_pv(q, k, v):
    s = lax.dot_general(q, k, (((1,), (1,)), ((), ())), preferred_element_type=F32)
    p = jnp.exp(s - jnp.max(s, axis=-1, keepdims=True))
    o = jnp.dot(p.astype(BF16), v, preferred_element_type=F32)
    return o, jnp.sum(p, axis=-1, keepdims=True)


def _per_segment(attend, lc, n):
    i = pl.program_id(2)

    @pl.when(i == 0)
    def _():
        attend(lc)

    @pl.when(i > 0)
    def _():
        attend(n)


ATT_HEADS = 2
ATT_W = ATT_HEADS * HEAD_TILE


def _head(x, j, width=HEAD_TILE, stride=HEAD_TILE):
    return x[:, j * stride:j * stride + width]


def _mla_attn_kernel(qn_ref, qr_ref, kv_ref, kr_ref, o_ref, *, lc):
    def attend(nk):
        kv, kr = kv_ref[0, :nk], kr_ref[0, :nk]
        for j in range(ATT_HEADS):
            q = jnp.concatenate([_head(qn_ref[0], j), _head(qr_ref[0], j)], axis=1)
            k = jnp.concatenate([_head(kv, 2 * j), kr], axis=1)
            o, l = _softmax_pv(q, k, _head(kv, 2 * j + 1))
            o_ref[0, :, j * HEAD_TILE:(j + 1) * HEAD_TILE] = (o / l).astype(o_ref.dtype)

    _per_segment(attend, lc, kv_ref.shape[1])


def mla_attention(q, kv, kr, lc):
    b, n, w = q.shape
    groups = w // (2 * ATT_W)
    tq = lc
    kern = functools.partial(_mla_attn_kernel, lc=lc)
    qspec = lambda off: pl.BlockSpec((1, tq, ATT_W), lambda bi, gi, i: (bi, i, off + gi))
    return pl.pallas_call(
        kern,
        grid=(b, groups, n // tq),
        in_specs=[qspec(0), qspec(groups),
                  pl.BlockSpec((1, n, 2 * ATT_W), lambda bi, gi, i: (bi, 0, gi)),
                  pl.BlockSpec((1, n, HEAD_TILE), lambda bi, gi, i: (bi, 0, 0))],
        out_specs=qspec(0),
        out_shape=jax.ShapeDtypeStruct((b, n, w // 2), BF16),
        compiler_params=_cparams(("parallel", "parallel", "arbitrary")),
        name="mla_attention",
    )(q, q, kv, kr)


def _diff_attn_kernel(q_ref, k_ref, v_ref, lam_ref, g_ref, o_ref, *, lc, post):
    lam = lam_ref[...]
    first = lax.broadcasted_iota(jnp.int32, (q_ref.shape[1], HEAD_TILE), 1) < DIFF_HEAD

    def attend(nk):
        for j in range(ATT_HEADS):
            q = _head(q_ref[0], j)
            k, v = _head(k_ref[0, :nk], j), _head(v_ref[0, :nk], j)
            o0, l0 = _softmax_pv(jnp.where(first, q, jnp.zeros_like(q)), k, v)
            o1, l1 = _softmax_pv(jnp.where(first, jnp.zeros_like(q), q), k, v)
            o = o0 / l0 - o1 * (lam[:, :1] / l1)
            ms = jnp.mean(o * o, axis=-1, keepdims=True)
            o_ref[0, :, j * HEAD_TILE:(j + 1) * HEAD_TILE] = (
                o * lax.rsqrt(ms + NORM_EPS) * g_ref[...] * post).astype(o_ref.dtype)

    _per_segment(attend, lc, v_ref.shape[1])


def diff_attention(q, k, v, lam, sub_norm, lc, post):
    b, n, w = q.shape
    tq = lc
    kern = functools.partial(_diff_attn_kernel, lc=lc, post=post)
    qspec = pl.BlockSpec((1, tq, ATT_W), lambda bi, gi, i: (bi, i, gi))
    kspec = pl.BlockSpec((1, n, ATT_W), lambda bi, gi, i: (bi, 0, gi))
    vec = pl.BlockSpec((1, HEAD_TILE), lambda bi, gi, i: (0, 0))
    return pl.pallas_call(
        kern,
        grid=(b, w // ATT_W, n // tq),
        in_specs=[qspec, kspec, kspec, vec, vec],
        out_specs=qspec,
        out_shape=jax.ShapeDtypeStruct((b, n, w), BF16),
        compiler_params=_cparams(("parallel", "parallel", "arbitrary")),
        name="diff_attention",
    )(q, k, v, jnp.broadcast_to(lam.reshape(1, 1), (1, HEAD_TILE)).astype(F32),
      sub_norm.reshape(1, HEAD_TILE))


S5_TB = 64
S5_GPT = 128 // S5_GROUP_CH
S5_TILES_IN = GROUP_W // 128
S5_TILES_ST = S5_GROUPS * S5_STATE // 128
S5_ST_PER_IN = S5_TILES_ST // S5_TILES_IN


def _s5_kernel(uf_ref, ub_ref, wd_ref, wr_ref, tab_ref, yf_ref, yb_ref, bu_re, bu_im, st_re, st_im):
    nb, tb, _ = uf_ref.shape
    rows = nb * tb
    half = S5_ST_PER_IN * 128

    @pl.when(pl.program_id(0) == 0)
    def _():
        st_re[...] = jnp.zeros_like(st_re)
        st_im[...] = jnp.zeros_like(st_im)

    for d, u_ref in enumerate((uf_ref, ub_ref)):
        x = u_ref[...].reshape(rows, GROUP_W).astype(BF16)
        for jt in range(S5_TILES_IN):
            bu = jnp.dot(x[:, jt * 128:(jt + 1) * 128], wd_ref[d, jt], preferred_element_type=F32)
            for c in range(S5_ST_PER_IN):
                bu_re[jt * S5_ST_PER_IN + c, d * rows:(d + 1) * rows, :] = bu[:, c * 128:(c + 1) * 128]
                bu_im[jt * S5_ST_PER_IN + c, d * rows:(d + 1) * rows, :] = bu[:, half + c * 128:half + (c + 1) * 128]

    def lane_tile(c, carry):
        for d in range(2):
            levels = [(tab_ref[d, 2 * k, c], tab_ref[d, 2 * k + 1, c], (1 << k) if d == 0 else 8 - (1 << k))
                      for k in range(3)]
            cr, ci = tab_ref[d, 6, c], tab_ref[d, 7, c]
            tiles = range(tb // 8) if d == 0 else range(tb // 8 - 1, -1, -1)
            for bi in range(nb):
                row = d * nb + bi
                car_r = st_re[c, row:row + 1, :]
                car_i = st_im[c, row:row + 1, :]
                for tt in tiles:
                    off = row * tb + tt * 8
                    hr = bu_re[c, off:off + 8, :]
                    hi = bu_im[c, off:off + 8, :]
                    for ar, ai, shift in levels:
                        pr = pltpu.roll(hr, shift, axis=0)
                        pi = pltpu.roll(hi, shift, axis=0)
                        hr, hi = hr + (ar * pr - ai * pi), hi + (ar * pi + ai * pr)
                    hr, hi = hr + (cr * car_r - ci * car_i), hi + (cr * car_i + ci * car_r)
                    bu_re[c, off:off + 8, :] = hr
                    bu_im[c, off:off + 8, :] = hi
                    last = 7 if d == 0 else 0
                    car_r, car_i = hr[last:last + 1], hi[last:last + 1]
                st_re[c, row:row + 1, :] = car_r
                st_im[c, row:row + 1, :] = car_i
        return carry

    lax.fori_loop(0, S5_TILES_ST, lane_tile, 0)

    for d, y_ref in enumerate((yf_ref, yb_ref)):
        for jt in range(S5_TILES_IN):
            tiles = [bu_re[jt * S5_ST_PER_IN + c, d * rows:(d + 1) * rows, :] for c in range(S5_ST_PER_IN)]
            tiles += [bu_im[jt * S5_ST_PER_IN + c, d * rows:(d + 1) * rows, :] for c in range(S5_ST_PER_IN)]
            h = jnp.concatenate(tiles, axis=1).astype(BF16)
            y = jnp.dot(h, wr_ref[d, jt], preferred_element_type=F32)
            y_ref[:, :, jt * 128:(jt + 1) * 128] = y.reshape(nb, tb, 128)


def _rev_block(i, nlc, ntot):
    return jnp.where(i < nlc, nlc - 1 - i, ntot - 1 - (i - nlc))


def s5_scan(z, lc, wd, wr, tab):
    b, n, _ = z.shape
    tb = min(S5_TB, lc)
    nlc, ntot = lc // tb, n // tb
    blk = (b, tb, GROUP_W)
    full = lambda a: pl.BlockSpec(a.shape, lambda i: (0,) * a.ndim)
    return pl.pallas_call(
        _s5_kernel,
        grid=(ntot,),
        in_specs=[pl.BlockSpec(blk, lambda i: (0, i, 0)),
                  pl.BlockSpec(blk, lambda i: (0, _rev_block(i, nlc, ntot), 0)),
                  full(wd), full(wr), full(tab)],
        out_specs=[pl.BlockSpec(blk, lambda i: (0, i, 0)),
                   pl.BlockSpec(blk, lambda i: (0, _rev_block(i, nlc, ntot), 0))],
        out_shape=[jax.ShapeDtypeStruct((b, n, GROUP_W), F32)] * 2,
        scratch_shapes=[pltpu.VMEM((S5_TILES_ST, 2 * b * tb, 128), F32)] * 2
        + [pltpu.VMEM((S5_TILES_ST, 8, 128), F32)] * 2,
        compiler_params=_cparams(("arbitrary",)),
        name="s5_scan",
    )(z, z, wd, wr, tab)


def _cmul(ar, ai, br, bi):
    return ar * br - ai * bi, ar * bi + ai * br


def s5_operators(a_re, a_im, log_step, b_re, b_im, c_re, c_im, nb):
    step = jnp.exp(log_step)[..., None]
    mag = jnp.exp(a_re * step)
    lr, li = mag * jnp.cos(a_im * step), mag * jnp.sin(a_im * step)
    den = a_re * a_re + a_im * a_im
    nr, ni = lr - 1.0, li
    fr = (nr * a_re + ni * a_im) / den
    fi = (ni * a_re - nr * a_im) / den
    bbr, bbi = _cmul(fr[..., None], fi[..., None], b_re, b_im)
    eye = jnp.eye(S5_GPT, dtype=F32)

    def drive(w):
        w = w.reshape(2, S5_TILES_IN, S5_GPT, S5_STATE, S5_GROUP_CH)
        w = jnp.einsum('dtgph,gk->dtghkp', w, eye)
        return w.reshape(2, S5_TILES_IN, 128, S5_GPT * S5_STATE)

    def readout(w):
        w = w.reshape(2, S5_TILES_IN, S5_GPT, S5_GROUP_CH, S5_STATE)
        w = jnp.einsum('dtghp,gk->dtgpkh', w, eye)
        return w.reshape(2, S5_TILES_IN, S5_GPT * S5_STATE, 128)

    wd = jnp.concatenate([drive(bbr), drive(bbi)], axis=-1).astype(BF16)
    wr = jnp.concatenate([readout(c_re), readout(-c_im)], axis=-2).astype(BF16)

    log_mag = (a_re * step).reshape(2, S5_TILES_ST, 1, 128)
    ang = (a_im * step).reshape(2, S5_TILES_ST, 1, 128)
    r8 = jnp.arange(8, dtype=F32).reshape(1, 1, 8, 1)

    def power(e, keep):
        e = jnp.broadcast_to(jnp.asarray(e, F32), (2, S5_TILES_ST, 8, 128))
        m = jnp.exp(log_mag * e)
        k = jnp.broadcast_to(keep, e.shape)
        return jnp.where(k, m * jnp.cos(ang * e), 0.0), jnp.where(k, m * jnp.sin(ang * e), 0.0)

    is_fwd = jnp.arange(2).reshape(2, 1, 1, 1) == 0
    tabs = []
    for s in (1, 2, 4):
        tabs += power(float(s), jnp.where(is_fwd, r8 >= s, r8 < 8 - s))
    tabs += power(jnp.where(is_fwd, r8 + 1.0, 8.0 - r8), True)
    return wd, wr, jnp.stack(tabs, axis=1)


def _s5_post_kernel(yf_ref, yb_ref, u_ref, d_ref, w_ref, o_ref):
    y = jax.nn.gelu(yf_ref[...] + yb_ref[...] + d_ref[...] * u_ref[...])
    gv = jnp.dot(y.astype(BF16), w_ref[...], preferred_element_type=F32)
    o_ref[...] = (gv[:, :GROUP_W] * jax.nn.sigmoid(gv[:, GROUP_W:])).astype(o_ref.dtype)


def s5_post(yf, yb, z, d_skip, w_glu):
    t = yf.shape[0]
    tm = _pick(t, (512, 256, 128))
    row = pl.BlockSpec((tm, GROUP_W), lambda i: (i, 0))
    return pl.pallas_call(
        _s5_post_kernel,
        grid=(t // tm,),
        in_specs=[row, row, row, pl.BlockSpec((1, GROUP_W), lambda i: (0, 0)),
                  pl.BlockSpec((GROUP_W, 2 * GROUP_W), lambda i: (0, 0))],
        out_specs=row,
        out_shape=jax.ShapeDtypeStruct((t, GROUP_W), BF16),
        compiler_params=_cparams(("parallel",)),
        name="s5_post",
    )(yf, yb, z, d_skip.reshape(1, GROUP_W), w_glu)


RW_TB = 32
RW_ROWS = 8
RW_TILES = GROUP_W // 128


def _split_bf16(x, parts):
    out = []
    for _ in range(parts - 1):
        hi = x.astype(BF16)
        out.append(hi)
        x = x - hi.astype(F32)
    out.append(x.astype(BF16))
    return out


def _permute_rows(p, x, parts):
    return sum(jnp.dot(p, piece, preferred_element_type=F32) for piece in _split_bf16(x, parts))


def _rwkv_to_lanes(fwd_ref, bwd_ref, p_ref, xt_s, dst_ref, parts):
    nb, tb, _ = fwd_ref.shape
    x = jnp.concatenate([fwd_ref[...].reshape(nb * tb, GROUP_W), bwd_ref[...].reshape(nb * tb, GROUP_W)], axis=0)
    xt_s[...] = _permute_rows(p_ref[...], x, parts)
    low = lax.broadcasted_iota(jnp.int32, (RW_HEAD, 128), 1) < RW_HEAD

    def pair(tp, carry):
        z = xt_s[pl.ds(pl.multiple_of(tp * 2 * RW_ROWS, 2 * RW_ROWS), 2 * RW_ROWS), :]
        zz = jnp.concatenate([z[p * RW_ROWS:(p + 1) * RW_ROWS, jt * 128:(jt + 1) * 128]
                              for p in range(2) for jt in range(RW_TILES)], axis=0)
        tt = zz.T
        top, bot = tt[:RW_HEAD], tt[RW_HEAD:]
        dst_ref[2 * tp] = jnp.where(low, top, pltpu.roll(bot, RW_HEAD, axis=1))
        dst_ref[2 * tp + 1] = jnp.where(low, pltpu.roll(top, RW_HEAD, axis=1), bot)
        return carry

    lax.fori_loop(0, tb // 2, pair, 0, unroll=8)


def _rwkv_from_lanes(y_s, pt_ref, xt_s, of_ref, ob_ref):
    nb, tb, _ = of_ref.shape
    low = lax.broadcasted_iota(jnp.int32, (RW_HEAD, 128), 1) < RW_HEAD

    def pair(tp, carry):
        y0, y1 = y_s[2 * tp], y_s[2 * tp + 1]
        top = jnp.where(low, y0, pltpu.roll(y1, RW_HEAD, axis=1))
        bot = jnp.where(low, pltpu.roll(y0, RW_HEAD, axis=1), y1)
        zz = jnp.concatenate([top, bot], axis=0).T
        base = pl.multiple_of(tp * 2 * RW_ROWS, 2 * RW_ROWS)
        for p in range(2):
            for jt in range(RW_TILES):
                blk = zz[(p * RW_TILES + jt) * RW_ROWS:(p * RW_TILES + jt + 1) * RW_ROWS]
                xt_s[pl.ds(base + p * RW_ROWS, RW_ROWS), jt * 128:(jt + 1) * 128] = blk
        return carry

    lax.fori_loop(0, tb // 2, pair, 0, unroll=8)
    y = _permute_rows(pt_ref[...], xt_s[...], 2)
    of_ref[...] = y[:nb * tb].reshape(nb, tb, GROUP_W)
    ob_ref[...] = y[nb * tb:].reshape(nb, tb, GROUP_W)


def _rwkv_kernel(rf_ref, kf_ref, vf_ref, wf_ref, af_ref, rb_ref, kb_ref, vb_ref, wb_ref, ab_ref,
                 p_ref, pt_ref, c_ref, of_ref, ob_ref,
                 r_s, k_s, v_s, w_s, a_s, y_s, xt_s, kk_s, bb_s, s_ref):
    tb = rf_ref.shape[1]

    @pl.when(pl.program_id(0) == 0)
    def _():
        s_ref[...] = jnp.zeros_like(s_ref)

    _rwkv_to_lanes(rf_ref, rb_ref, p_ref, xt_s, r_s, 2)
    _rwkv_to_lanes(kf_ref, kb_ref, p_ref, xt_s, k_s, 2)
    _rwkv_to_lanes(vf_ref, vb_ref, p_ref, xt_s, v_s, 2)
    _rwkv_to_lanes(wf_ref, wb_ref, p_ref, xt_s, w_s, 3)
    _rwkv_to_lanes(af_ref, ab_ref, p_ref, xt_s, a_s, 2)

    def step(t, carry):
        k = k_s[t]
        kk = k * c_ref[2]
        kk = kk * lax.rsqrt(jnp.maximum(jnp.sum(kk * kk, axis=0, keepdims=True), 1e-12))
        z = -(w_s[t] + c_ref[0])
        w = -(jnp.maximum(z, 0.0) + jnp.log(1.0 + jnp.exp(-jnp.abs(z)))) - 0.5
        a = jax.nn.sigmoid(a_s[t] + c_ref[1])
        kd = k * (1.0 + (a - 1.0) * c_ref[3])
        bonus = jnp.sum(r_s[t] * kd * c_ref[4], axis=0, keepdims=True)
        w_s[t] = jnp.exp(-jnp.exp(w))
        k_s[t] = kd
        kk_s[...] = kk
        bb_s[...] = kk * a

        def vgroup(vg, carry2):
            off = pl.multiple_of(vg * 8, 8)
            vrows = v_s[t, pl.ds(off, 8), :]
            outs = []
            for j in range(8):
                sv = s_ref[off + j]
                sa = jnp.sum(sv * kk_s[...], axis=0, keepdims=True)
                sn = sv * w_s[t] - sa * bb_s[...] + vrows[j:j + 1] * k_s[t]
                s_ref[off + j] = sn
                outs.append(jnp.sum(sn * r_s[t], axis=0, keepdims=True))
            y_s[t, pl.ds(off, 8), :] = jnp.concatenate(outs, axis=0) + bonus * vrows
            return carry2

        return lax.fori_loop(0, RW_HEAD // 8, vgroup, carry)

    lax.fori_loop(0, tb, step, 0)
    _rwkv_from_lanes(y_s, pt_ref, xt_s, of_ref, ob_ref)


def rwkv_scan(zs, lr, consts, lc):
    b, n, _ = zs.shape
    assert 2 * b == RW_ROWS
    tb = RW_TB
    nlc, ntot = lc // tb, n // tb
    rows = RW_ROWS * tb
    r_out = jnp.arange(rows)
    step, db = r_out // RW_ROWS, r_out % RW_ROWS
    src = db * tb + jnp.where(db < b, step, tb - 1 - step)
    perm = (src[:, None] == jnp.arange(rows)[None, :]).astype(BF16)
    blk = (b, tb, GROUP_W)
    fwd = lambda col: pl.BlockSpec(blk, lambda i: (0, i, col))
    bwd = lambda col: pl.BlockSpec(blk, lambda i: (0, _rev_block(i, nlc, ntot), col))
    full = lambda a: pl.BlockSpec(a.shape, lambda i: (0,) * a.ndim)
    tile = pltpu.VMEM((tb, RW_HEAD, 128), F32)
    return pl.pallas_call(
        _rwkv_kernel,
        grid=(ntot,),
        in_specs=[fwd(0), fwd(1), fwd(2), fwd(0), fwd(2), bwd(0), bwd(1), bwd(2), bwd(1), bwd(3),
                  full(perm), full(perm), full(consts)],
        out_specs=[fwd(0), bwd(0)],
        out_shape=[jax.ShapeDtypeStruct((b, n, GROUP_W), F32)] * 2,
        scratch_shapes=[tile] * 6 + [pltpu.VMEM((rows, GROUP_W), F32),
                                     pltpu.VMEM((RW_HEAD, 128), F32), pltpu.VMEM((RW_HEAD, 128), F32),
                                     pltpu.VMEM((RW_HEAD, RW_HEAD, 128), F32)],
        compiler_params=_cparams(("arbitrary",)),
        name="rwkv_scan",
    )(zs, zs, zs, lr, lr, zs, zs, zs, lr, lr, perm, perm.T, consts)


def _moe_row_copy(tok_ref, x_hbm, xbuf, sem, blk, slot, r):
    tok = tok_ref[blk * MOE_BLOCK + r]
    return pltpu.make_async_copy(x_hbm.at[pl.ds(tok, 1)], xbuf.at[slot, pl.ds(r, 1)], sem.at[slot])


def _moe_kernel(be_ref, nu_ref, tok_ref, x_hbm, wg_ref, wu_ref, wd_ref, o_ref, xbuf, sem, g_s, u_s, d_s):
    i = pl.program_id(0)
    used = i < nu_ref[0]
    changed = jnp.logical_or(i == 0, be_ref[i] != be_ref[jnp.maximum(i - 1, 0)])
    slot = i % 2

    def gather(blk, into):
        def start(r, carry):
            _moe_row_copy(tok_ref, x_hbm, xbuf, sem, blk, into, r).start()
            return carry
        lax.fori_loop(0, MOE_BLOCK, start, 0)

    @pl.when(i == 0)
    def _():
        gather(0, 0)

    @pl.when(i + 1 < nu_ref[0])
    def _():
        gather(i + 1, 1 - slot)

    @pl.when(jnp.logical_and(used, changed))
    def _():
        g_s[...] = wg_ref[...].astype(BF16)
        u_s[...] = wu_ref[...].astype(BF16)
        d_s[...] = wd_ref[...].astype(BF16)

    @pl.when(used)
    def _():
        def wait(r, carry):
            _moe_row_copy(tok_ref, x_hbm, xbuf, sem, i, slot, r).wait()
            return carry
        lax.fori_loop(0, MOE_BLOCK, wait, 0)
        x = xbuf[slot].astype(BF16)
        g = jnp.dot(x, g_s[...], preferred_element_type=F32)
        u = jnp.dot(x, u_s[...], preferred_element_type=F32)
        h = (g * jax.nn.sigmoid(g) * u).astype(BF16)
        o_ref[...] = jnp.dot(h, d_s[...], preferred_element_type=F32)

    @pl.when(jnp.logical_not(used))
    def _():
        o_ref[...] = jnp.zeros_like(o_ref)


def moe_ffn(block_exp, n_used, slot_tok, x, wg, wu, wd, layer):
    cap = slot_tok.shape[0]
    d = x.shape[1]
    hid = wg.shape[-1]
    nb = cap // MOE_BLOCK
    once = pl.Buffered(1)
    wspec = lambda shape: pl.BlockSpec((None, None) + shape, lambda i, be, nu, tok: (layer, be[i], 0, 0),
                                       pipeline_mode=once)
    grid_spec = pltpu.PrefetchScalarGridSpec(
        num_scalar_prefetch=3,
        grid=(nb,),
        in_specs=[pl.BlockSpec(memory_space=pl.ANY), wspec((d, hid)), wspec((d, hid)), wspec((hid, d))],
        out_specs=pl.BlockSpec((MOE_BLOCK, d), lambda i, be, nu, tok: (i, 0)),
        scratch_shapes=[pltpu.VMEM((2, MOE_BLOCK, d), F32), pltpu.SemaphoreType.DMA((2,)),
                        pltpu.VMEM((d, hid), BF16), pltpu.VMEM((d, hid), BF16), pltpu.VMEM((hid, d), BF16)],
    )
    return pl.pallas_call(
        _moe_kernel,
        grid_spec=grid_spec,
        out_shape=jax.ShapeDtypeStruct((cap, d), F32),
        compiler_params=_cparams(("arbitrary",)),
        name="moe_ffn",
    )(block_exp, n_used, slot_tok, x, wg, wu, wd)


def _rms(x, g):
    return x * lax.rsqrt(jnp.mean(x * x, axis=-1, keepdims=True) + NORM_EPS) * g


def _axial_rope(rows, dim):
    n_freq = dim // 4
    inv = ROPE_BASE ** (-jnp.arange(n_freq, dtype=F32) / n_freq)
    row = jnp.repeat(jnp.arange(rows, dtype=F32), GRID_W)
    col = jnp.tile(jnp.arange(GRID_W, dtype=F32), rows)
    ang = jnp.concatenate([row[:, None] * inv, col[:, None] * inv], axis=-1)
    return jnp.cos(ang), jnp.sin(ang)


def _rope(x, cos, sin):
    half = x.shape[-1] // 2
    x1, x2 = x[..., :half], x[..., half:]
    return jnp.concatenate([x1 * cos - x2 * sin, x1 * sin + x2 * cos], axis=-1)


def _rope_lat(x, cos, sin, lc):
    extra = (None,) * (x.ndim - 3)
    cs = cos[(slice(None),) + extra]
    sn = sin[(slice(None),) + extra]
    return jnp.concatenate([x[:, :lc], _rope(x[:, lc:], cs, sn)], axis=1)


def _seq_rev(t, lc):
    return jnp.concatenate([jnp.flip(t[:, :lc], 1), jnp.flip(t[:, lc:], 1)], axis=1)


def _mm3(x, w, **kw):
    b, n, k = x.shape
    return matmul(x.reshape(b * n, k), w, **kw).reshape(b, n, -1)


def _mla_mixer(cq, ckv, krope, lc, q_a_norm, kv_a_norm, w_uq, w_ukv, q_norm, k_norm, cos, sin):
    b, n, _ = cq.shape
    pad = HEAD_TILE - MLA_ROPE
    wq = w_uq.reshape(MLA_Q_RANK, MLA_HEADS, MLA_NOPE + MLA_ROPE)
    wq = jnp.concatenate([wq[:, :, :MLA_NOPE].reshape(MLA_Q_RANK, -1),
                          jnp.pad(wq[:, :, MLA_NOPE:], ((0, 0), (0, 0), (0, pad))).reshape(MLA_Q_RANK, -1)], axis=1)
    q = _mm3(_rms(cq, q_a_norm), wq).reshape(b, n, 2, MLA_HEADS, HEAD_TILE)
    kv = _mm3(_rms(ckv, kv_a_norm), w_ukv).reshape(b, n, MLA_HEADS, MLA_NOPE + MLA_V)
    scale = float(MLA_NOPE + MLA_ROPE) ** -0.5
    q_nope = _rms(q[:, :, 0], q_norm[:MLA_NOPE]) * scale
    q_rope = _rope_lat(_rms(q[:, :, 1, :, :MLA_ROPE], q_norm[MLA_NOPE:]), cos, sin, lc) * scale
    q_rope = jnp.pad(q_rope, ((0, 0), (0, 0), (0, 0), (0, pad)))
    k_nope = _rms(kv[..., :MLA_NOPE], k_norm[:MLA_NOPE])
    k_rope = _rope_lat(_rms(krope, k_norm[MLA_NOPE:]), cos, sin, lc)
    qf = jnp.stack([q_nope, q_rope], axis=2).reshape(b, n, 2 * MLA_HEADS * HEAD_TILE).astype(BF16)
    kvf = jnp.concatenate([k_nope, kv[..., MLA_NOPE:]], axis=-1).reshape(b, n, -1).astype(BF16)
    return mla_attention(qf, kvf, jnp.pad(k_rope, ((0, 0), (0, 0), (0, pad))).astype(BF16), lc)


def _diff_mixer(dq, dk, dv, lc, q_norm, k_norm, lam_vecs, sub_norm, lam_init, cos, sin):
    b, n, _ = dq.shape
    q = _rope_lat(_rms(dq.reshape(b, n, DIFF_HEADS, 2, DIFF_HEAD), q_norm), cos, sin, lc) * float(DIFF_HEAD) ** -0.5
    k = _rope_lat(_rms(dk.reshape(b, n, DIFF_HEADS, 2, DIFF_HEAD), k_norm), cos, sin, lc)
    lam = jnp.exp(jnp.sum(lam_vecs[0] * lam_vecs[1])) - jnp.exp(jnp.sum(lam_vecs[2] * lam_vecs[3])) + lam_init
    flat = lambda x: x.reshape(b, n, GROUP_W).astype(BF16)
    return diff_attention(flat(q), flat(k), dv.astype(BF16), lam, sub_norm, lc, 1.0 - lam_init)


def _centred_shift(z, mu_prev, mu_next, lc):
    n = z.shape[1]
    t = jnp.arange(n)[None, :, None]
    z_prev = jnp.where((t == 0) | (t == lc), 0.0, jnp.roll(z, 1, axis=1))
    z_next = jnp.where((t == lc - 1) | (t == n - 1), 0.0, jnp.roll(z, -1, axis=1))
    return z + mu_prev * (z_prev - z) + mu_next * (z_next - z)


def _rwkv_mixer(z, lc, mu, w0, w2, a0, a2, g2, k_k, k_a, r_k, ln_g, ln_b):
    b, n, _ = z.shape
    heads = lambda t: t.reshape(b, n, RW_HEADS, RW_HEAD)
    z = _centred_shift(z, mu[0], mu[1], lc)
    o = 3 * GROUP_W
    wl, al, gl = z[..., o:o + 64], z[..., o + 64:o + 128], z[..., o + 128:o + 256]
    lr_in = jnp.concatenate([jnp.tanh(wl), al, jax.nn.sigmoid(gl)], axis=-1)
    zw = jnp.zeros((RW_W_RANK, GROUP_W), F32)
    zg = jnp.zeros((RW_G_RANK, GROUP_W), F32)
    w_lr = jnp.concatenate([
        jnp.concatenate([w2[0], w2[1], zw, zw, zw], axis=1),
        jnp.concatenate([zw, zw, a2[0], a2[1], zw], axis=1),
        jnp.concatenate([zg, zg, zg, zg, g2], axis=1)], axis=0)
    lr = _mm3(lr_in, w_lr)
    g = lr[..., 4 * GROUP_W:]

    def chain_tile(p_fwd, p_bwd):
        per_dir = [p.reshape(RW_TILES, 2, RW_HEAD).transpose(2, 1, 0) for p in (p_fwd, p_bwd)]
        return jnp.stack([per_dir[0]] * b + [per_dir[1]] * b, axis=-1).reshape(RW_HEAD, 128)

    consts = jnp.stack([chain_tile(w0[0], w0[1]), chain_tile(a0[0], a0[1]), chain_tile(k_k, k_k),
                        chain_tile(k_a, k_a), chain_tile(r_k.reshape(-1), r_k.reshape(-1))])
    o_f, o_b = rwkv_scan(z, lr, consts, lc)
    o = heads(o_f + o_b)
    mean = jnp.mean(o, axis=-1, keepdims=True)
    var = jnp.mean(jnp.square(o - mean), axis=-1, keepdims=True)
    o = ((o - mean) * lax.rsqrt(var + RW_GN_EPS)).reshape(b, n, GROUP_W)
    return ((o * ln_g + ln_b) * g).astype(BF16)


def _hier_moe(h, wr, br_coarse, br_fine, wg, wu, wd, layer):
    t, d = h.shape
    logits = matmul(h, wr)
    p_group = jax.nn.softmax(logits[:, :MOE_GROUPS] + br_coarse, axis=-1)
    group = jnp.argmax(p_group, axis=-1).astype(jnp.int32)
    p_sel = jnp.take_along_axis(p_group, group[:, None], axis=-1)
    fine = (logits[:, MOE_GROUPS:MOE_GROUPS + MOE_EXPERTS] + br_fine).reshape(t, MOE_GROUPS, MOE_PER_GROUP)
    fine = jnp.take_along_axis(fine, group[:, None, None], axis=1)[:, 0]
    top_v, top_i = lax.top_k(fine, MOE_TOP_K)
    weight = p_sel * jax.nn.softmax(top_v, axis=-1)
    expert = group[:, None] * MOE_PER_GROUP + top_i.astype(jnp.int32)
    flat_e = expert.reshape(-1)
    n_assign = flat_e.shape[0]
    onehot = (flat_e[:, None] == jnp.arange(MOE_EXPERTS, dtype=jnp.int32)[None, :]).astype(jnp.int32)
    running = jnp.cumsum(onehot, axis=0)
    counts = running[-1]
    padded = (counts + MOE_BLOCK - 1) // MOE_BLOCK * MOE_BLOCK
    pad_end = jnp.cumsum(padded)
    pad_start = pad_end - padded
    dest = jnp.sum(onehot * (running - 1 + pad_start[None, :]), axis=1).astype(jnp.int32)
    n_blocks = -(-(n_assign + MOE_EXPERTS * (MOE_BLOCK - 1)) // MOE_BLOCK)
    cap = n_blocks * MOE_BLOCK
    slot_tok = jnp.zeros((cap,), jnp.int32).at[dest].set(jnp.arange(n_assign, dtype=jnp.int32) // MOE_TOP_K)
    block_start = jnp.arange(n_blocks, dtype=jnp.int32) * MOE_BLOCK
    block_exp = jnp.minimum(jnp.sum((pad_end[None, :] <= block_start[:, None]).astype(jnp.int32), axis=1),
                            MOE_EXPERTS - 1)
    n_used = (pad_end[-1:] // MOE_BLOCK).astype(jnp.int32)
    yb = moe_ffn(block_exp, n_used, slot_tok, h, wg, wu, wd, layer)
    slot_of = dest.reshape(t, MOE_TOP_K)
    return weight, yb[slot_of[:, 0]], yb[slot_of[:, 1]]


def _combine_kernel(x_ref, y0_ref, y1_ref, w_ref, g_ref, o_ref):
    w = w_ref[0]
    o_ref[0] = x_ref[0] + g_ref[0] * (w[:, 0:1] * y0_ref[0] + w[:, 1:2] * y1_ref[0])


def moe_combine(xs, weight, y0, y1, gate, lc, skip):
    b, ntok, d = xs.shape
    r = ntok - skip
    tm, tn = lc, 1024
    off = skip // tm
    blk = lambda idx: pl.BlockSpec((1, tm, tn), idx)
    rows = lambda bi, i, j: (bi, i, j)
    return pl.pallas_call(
        _combine_kernel,
        grid=(b, r // tm, d // tn),
        in_specs=[blk(lambda bi, i, j: (bi, off + i, j)), blk(rows), blk(rows),
                  pl.BlockSpec((1, tm, MOE_TOP_K), lambda bi, i, j: (bi, i, 0)),
                  pl.BlockSpec((1, 1, tn), lambda bi, i, j: (bi * 2 + jnp.minimum(off + i, 1), 0, j))],
        out_specs=blk(rows),
        out_shape=jax.ShapeDtypeStruct((b, r, d), F32),
        compiler_params=_cparams(("parallel", "parallel", "parallel")),
        name="moe_combine",
    )(xs, y0.reshape(b, r, d), y1.reshape(b, r, d), weight.reshape(b, r, MOE_TOP_K), gate.reshape(b * 2, 1, d))


def kernel(x, c, ctx, c_ctx, mod_w, mod_b, norm1_g, norm2_g, w_in, w_out, s5_a_re, s5_a_im, s5_log_step, s5_b_re, s5_b_im, s5_c_re, s5_c_im, s5_d, s5_w_glu, mla_q_a_norm, mla_kv_a_norm, mla_w_uq, mla_w_ukv, mla_q_norm, mla_k_norm, rw_mu, rw_w0, rw_w2, rw_a0, rw_a2, rw_g2, rw_k_k, rw_k_a, rw_r_k, rw_ln_g, rw_ln_b, diff_q_norm, diff_k_norm, diff_lambda, diff_sub_norm, moe_wr_coarse, moe_br_coarse, moe_wr_fine, moe_br_fine, moe_w_gate, moe_w_up, moe_w_down):
    b, n, d = x.shape
    lc = ctx.shape[1]
    ntok = lc + n
    depth = mod_w.shape[0]
    rows = n // GRID_W
    cos_m, sin_m = _axial_rope(rows, MLA_ROPE)
    cos_d, sin_d = _axial_rope(rows, DIFF_HEAD)
    act = jnp.concatenate([jax.nn.silu(c), jax.nn.silu(c_ctx)[None],
                           jnp.zeros((8 - b - 1, d), F32)], axis=0)
    mod_all = mod_matmul(act, mod_w) + mod_b[:, None, :]
    xs = jnp.concatenate([ctx, x], axis=1)

    for l in range(depth):
        last = l == depth - 1
        lam_init = 0.8 - 0.6 * math.exp(-0.3 * l)
        mod_lat = mod_all[l, :b].reshape(b, N_MOD, d)
        mod_ctx = jnp.broadcast_to(mod_all[l, b].reshape(1, N_MOD, d), (b, N_MOD, d))
        mods = jnp.stack([mod_ctx, mod_lat], axis=2)

        h = normmod(xs, norm1_g[l], mods[:, 0], mods[:, 1], lc).reshape(b * ntok, d)
        tm_in = _pick(b * ntok, (1024, 512, 256, 128))
        za = matmul(h, w_in[l][:, :ZA_WIDTH].astype(BF16), tn=ZA_WIDTH).reshape(b, ntok, ZA_WIDTH)
        zb = matmul(h, w_in[l][:, ZA_SPLIT:].astype(BF16), tm=tm_in,
                    tn=_pick(ZB_WIDTH, (1280, 640, 128))).reshape(b, ntok, ZB_WIDTH)

        wd, wr, lam = s5_operators(s5_a_re[l], s5_a_im[l], s5_log_step[l], s5_b_re[l], s5_b_im[l],
                                   s5_c_re[l], s5_c_im[l], b)
        yf, yr = s5_scan(za, lc, wd, wr, lam)
        ya = s5_post(yf.reshape(b * ntok, GROUP_W), yr.reshape(b * ntok, GROUP_W),
                     za.reshape(b * ntok, ZA_WIDTH), s5_d[l], s5_w_glu[l].astype(BF16))
        yb = _mla_mixer(za[..., ZA_CQ:ZA_CKV], za[..., ZA_CKV:ZA_KR], za[..., ZA_KR:ZA_SPLIT], lc,
                        mla_q_a_norm[l], mla_kv_a_norm[l], mla_w_uq[l], mla_w_ukv[l],
                        mla_q_norm[l], mla_k_norm[l], cos_m, sin_m)
        yc = _rwkv_mixer(zb[..., :ZB_DQ], lc, rw_mu[l], rw_w0[l], rw_w2[l], rw_a0[l], rw_a2[l],
                         rw_g2[l], rw_k_k[l], rw_k_a[l], rw_r_k[l], rw_ln_g[l], rw_ln_b[l])
        yd = _diff_mixer(zb[..., ZB_DQ:ZB_DK], zb[..., ZB_DK:ZB_DV], zb[..., ZB_DV:], lc,
                         diff_q_norm[l], diff_k_norm[l], diff_lambda[l], diff_sub_norm[l], lam_init,
                         cos_d, sin_d)
        flat = lambda y: y.reshape(b * ntok, GROUP_W)
        xs = out_proj((ya, flat(yb), flat(yc), flat(yd)), w_out[l].astype(BF16), xs, mods[:, 2], lc)

        h2 = normmod(xs, norm2_g[l], mods[:, 3], mods[:, 4], lc, out_dtype=F32)
        wrt = jnp.concatenate([moe_wr_coarse[l], moe_wr_fine[l],
                               jnp.zeros((d, 128 - MOE_GROUPS - MOE_EXPERTS), F32)], axis=1)
        experts = (moe_w_gate, moe_w_up, moe_w_down, l)
        skip = lc if last else 0
        routed = _hier_moe(h2[:, skip:].reshape(b * (ntok - skip), d), wrt, moe_br_coarse[l], moe_br_fine[l],
                           *experts)
        xs = moe_combine(xs, *routed, mods[:, 5], lc, skip)
    return xs
```

```python
import functools
import math

import jax
import jax.numpy as jnp
from jax import lax
from jax.experimental import pallas as pl
from jax.experimental.pallas import tpu as pltpu

F32 = jnp.float32
BF16 = jnp.bfloat16

D_MODEL = 4096
N_MOD = 6
GROUP_W = D_MODEL // 4
GRID_W = 64
ROPE_BASE = 10000.0
NORM_EPS = 1e-6

S5_GROUP_CH = 16
S5_GROUPS = GROUP_W // S5_GROUP_CH
S5_STATE = 64

MLA_HEADS = 8
MLA_NOPE = 128
MLA_ROPE = 64
MLA_V = GROUP_W // MLA_HEADS
MLA_Q_RANK = 512
MLA_KV_RANK = 256

RW_HEAD = 64
RW_HEADS = GROUP_W // RW_HEAD
RW_W_RANK = 64
RW_A_RANK = 64
RW_G_RANK = 128
RW_GN_EPS = 64e-5
RW_IN = 3 * GROUP_W + RW_W_RANK + RW_A_RANK + RW_G_RANK

DIFF_HEAD = 64
DIFF_HEADS = GROUP_W // (2 * DIFF_HEAD)

MOE_GROUPS = 4
MOE_PER_GROUP = 8
MOE_EXPERTS = MOE_GROUPS * MOE_PER_GROUP
MOE_TOP_K = 2
MOE_BLOCK = 128

ZA_CQ = GROUP_W
ZA_CKV = ZA_CQ + MLA_Q_RANK
ZA_KR = ZA_CKV + MLA_KV_RANK
ZA_SPLIT = ZA_KR + MLA_ROPE
ZA_WIDTH = ZA_KR + 128
ZB_DQ = RW_IN
ZB_DK = ZB_DQ + GROUP_W
ZB_DV = ZB_DK + GROUP_W
ZB_WIDTH = ZB_DV + GROUP_W

VMEM_LIMIT = 56 * 1024 * 1024


def _cparams(sem):
    return pltpu.CompilerParams(dimension_semantics=sem, vmem_limit_bytes=VMEM_LIMIT)


def _pick(n, prefs):
    for p in prefs:
        if n % p == 0:
            return p
    return n


def _mm_kernel(a_ref, b_ref, o_ref):
    a = a_ref[...].astype(BF16)
    b = b_ref[...].astype(BF16)
    o_ref[...] = jnp.dot(a, b, preferred_element_type=F32).astype(o_ref.dtype)


def matmul(a, b, out_dtype=F32, tm=None, tn=None, layer=None, n_cols=None):
    m, k = a.shape
    n = n_cols or b.shape[-1]
    tm = tm or _pick(m, (512, 256, 128))
    tn = tn or _pick(n, (1024, 768, 512, 256, 128))
    if layer is None:
        b_spec = pl.BlockSpec((k, tn), lambda i, j: (0, j))
    else:
        b_spec = pl.BlockSpec((None, k, tn), lambda i, j: (layer, 0, j))
    return pl.pallas_call(
        _mm_kernel,
        grid=(m // tm, n // tn),
        in_specs=[pl.BlockSpec((tm, k), lambda i, j: (i, 0)), b_spec],
        out_specs=pl.BlockSpec((tm, tn), lambda i, j: (i, j)),
        out_shape=jax.ShapeDtypeStruct((m, n), out_dtype),
        compiler_params=_cparams(("parallel", "parallel")),
        name="matmul",
    )(a, b)


def mod_matmul(act, mod_w, tn=1024):
    m, k = act.shape
    depth, _, n = mod_w.shape
    return pl.pallas_call(
        _mm_kernel,
        grid=(depth, n // tn),
        in_specs=[pl.BlockSpec((m, k), lambda l, j: (0, 0)),
                  pl.BlockSpec((None, k, tn), lambda l, j: (l, 0, j))],
        out_specs=pl.BlockSpec((None, m, tn), lambda l, j: (l, 0, j)),
        out_shape=jax.ShapeDtypeStruct((depth, m, n), F32),
        compiler_params=_cparams(("parallel", "parallel")),
        name="mod_matmul",
    )(act, mod_w)


def _normmod_kernel(x_ref, g_ref, sh_ref, sc_ref, o_ref):
    x = x_ref[0]
    ms = jnp.mean(x * x, axis=-1, keepdims=True)
    y = x * lax.rsqrt(ms + NORM_EPS) * g_ref[...]
    o_ref[0] = (y * (1.0 + sc_ref[0]) + sh_ref[0]).astype(o_ref.dtype)


def normmod(xs, g, shift, scale, lc, out_dtype=BF16):
    b, n, d = xs.shape
    tb = lc
    sh = shift.reshape(b * 2, 1, d)
    sc = scale.reshape(b * 2, 1, d)
    seg = lambda bi, i: (bi * 2 + jnp.minimum(i, 1), 0, 0)
    return pl.pallas_call(
        _normmod_kernel,
        grid=(b, n // tb),
        in_specs=[pl.BlockSpec((1, tb, d), lambda bi, i: (bi, i, 0)),
                  pl.BlockSpec((1, d), lambda bi, i: (0, 0)),
                  pl.BlockSpec((1, 1, d), seg),
                  pl.BlockSpec((1, 1, d), seg)],
        out_specs=pl.BlockSpec((1, tb, d), lambda bi, i: (bi, i, 0)),
        out_shape=jax.ShapeDtypeStruct((b, n, d), out_dtype),
        compiler_params=_cparams(("parallel", "parallel")),
        name="normmod",
    )(xs, g.reshape(1, d), sh, sc)


def _out_proj_kernel(ya_ref, yb_ref, yc_ref, yd_ref, w_ref, x_ref, g_ref, o_ref, *, lc, blocks_per_seq):
    acc = jnp.dot(ya_ref[...], w_ref[0:GROUP_W], preferred_element_type=F32)
    acc += jnp.dot(yb_ref[...], w_ref[GROUP_W:2 * GROUP_W], preferred_element_type=F32)
    acc += jnp.dot(yc_ref[...], w_ref[2 * GROUP_W:3 * GROUP_W], preferred_element_type=F32)
    acc += jnp.dot(yd_ref[...], w_ref[3 * GROUP_W:4 * GROUP_W], preferred_element_type=F32)
    tm = acc.shape[0]
    tok = lax.broadcasted_iota(jnp.int32, (tm, 1), 0) + (pl.program_id(0) % blocks_per_seq) * tm
    gate = jnp.where(tok < lc, g_ref[0:1], g_ref[1:2])
    o_ref[...] = x_ref[...] + gate * acc


def out_proj(groups, w, xs, gate, lc):
    b, n, d = xs.shape
    tm = _pick(n, (768, 384, 256, 128))
    tn = 1024
    bps = n // tm
    row = pl.BlockSpec((tm, GROUP_W), lambda i, j: (i, 0))
    kern = functools.partial(_out_proj_kernel, lc=lc, blocks_per_seq=bps)
    out = pl.pallas_call(
        kern,
        grid=(b * bps, d // tn),
        in_specs=[row, row, row, row,
                  pl.BlockSpec((4 * GROUP_W, tn), lambda i, j: (0, j)),
                  pl.BlockSpec((tm, tn), lambda i, j: (i, j)),
                  pl.BlockSpec((None, 2, tn), lambda i, j: (i // bps, 0, j))],
        out_specs=pl.BlockSpec((tm, tn), lambda i, j: (i, j)),
        out_shape=jax.ShapeDtypeStruct((b * n, d), F32),
        compiler_params=_cparams(("parallel", "parallel")),
        name="out_proj",
    )(*groups, w, xs.reshape(b * n, d), gate)
    return out.reshape(b, n, d)


HEAD_TILE = 128


def _softmax_pv(q, k, v):
    s = lax.dot_general(q, k, (((1,), (1,)), ((), ())), preferred_element_type=F32)
    p = jnp.exp(s - jnp.max(s, axis=-1, keepdims=True))
    o = jnp.dot(p.astype(BF16), v, preferred_element_type=F32)
    return o, jnp.sum(p, axis=-1, keepdims=True)


def _per_segment(attend, lc, n):
    i = pl.program_id(2)

    @pl.when(i == 0)
    def _():
        attend(lc)

    @pl.when(i > 0)
    def _():
        attend(n)


ATT_HEADS = 2
ATT_W = ATT_HEADS * HEAD_TILE


def _head(x, j, width=HEAD_TILE, stride=HEAD_TILE):
    return x[:, j * stride:j * stride + width]


def _mla_attn_kernel(qn_ref, qr_ref, kv_ref, kr_ref, o_ref, *, lc):
    def attend(nk):
        kv, kr = kv_ref[0, :nk], kr_ref[0, :nk]
        for j in range(ATT_HEADS):
            q = jnp.concatenate([_head(qn_ref[0], j), _head(qr_ref[0], j)], axis=1)
            k = jnp.concatenate([_head(kv, 2 * j), kr], axis=1)
            o, l = _softmax_pv(q, k, _head(kv, 2 * j + 1))
            o_ref[0, :, j * HEAD_TILE:(j + 1) * HEAD_TILE] = (o / l).astype(o_ref.dtype)

    _per_segment(attend, lc, kv_ref.shape[1])


def mla_attention(q, kv, kr, lc):
    b, n, w = q.shape
    groups = w // (2 * ATT_W)
    tq = lc
    kern = functools.partial(_mla_attn_kernel, lc=lc)
    qspec = lambda off: pl.BlockSpec((1, tq, ATT_W), lambda bi, gi, i: (bi, i, off + gi))
    return pl.pallas_call(
        kern,
        grid=(b, groups, n // tq),
        in_specs=[qspec(0), qspec(groups),
                  pl.BlockSpec((1, n, 2 * ATT_W), lambda bi, gi, i: (bi, 0, gi)),
                  pl.BlockSpec((1, n, HEAD_TILE), lambda bi, gi, i: (bi, 0, 0))],
        out_specs=qspec(0),
        out_shape=jax.ShapeDtypeStruct((b, n, w // 2), BF16),
        compiler_params=_cparams(("parallel", "parallel", "arbitrary")),
        name="mla_attention",
    )(q, q, kv, kr)


def _diff_attn_kernel(q_ref, k_ref, v_ref, lam_ref, g_ref, o_ref, *, lc, post):
    lam = lam_ref[...]
    first = lax.broadcasted_iota(jnp.int32, (q_ref.shape[1], HEAD_TILE), 1) < DIFF_HEAD

    def attend(nk):
        for j in range(ATT_HEADS):
            q = _head(q_ref[0], j)
            k, v = _head(k_ref[0, :nk], j), _head(v_ref[0, :nk], j)
            o0, l0 = _softmax_pv(jnp.where(first, q, jnp.zeros_like(q)), k, v)
            o1, l1 = _softmax_pv(jnp.where(first, jnp.zeros_like(q), q), k, v)
            o = o0 / l0 - o1 * (lam[:, :1] / l1)
            ms = jnp.mean(o * o, axis=-1, keepdims=True)
            o_ref[0, :, j * HEAD_TILE:(j + 1) * HEAD_TILE] = (
                o * lax.rsqrt(ms + NORM_EPS) * g_ref[...] * post).astype(o_ref.dtype)

    _per_segment(attend, lc, v_ref.shape[1])


def diff_attention(q, k, v, lam, sub_norm, lc, post):
    b, n, w = q.shape
    tq = lc
    kern = functools.partial(_diff_attn_kernel, lc=lc, post=post)
    qspec = pl.BlockSpec((1, tq, ATT_W), lambda bi, gi, i: (bi, i, gi))
    kspec = pl.BlockSpec((1, n, ATT_W), lambda bi, gi, i: (bi, 0, gi))
    vec = pl.BlockSpec((1, HEAD_TILE), lambda bi, gi, i: (0, 0))
    return pl.pallas_call(
        kern,
        grid=(b, w // ATT_W, n // tq),
        in_specs=[qspec, kspec, kspec, vec, vec],
        out_specs=qspec,
        out_shape=jax.ShapeDtypeStruct((b, n, w), BF16),
        compiler_params=_cparams(("parallel", "parallel", "arbitrary")),
        name="diff_attention",
    )(q, k, v, jnp.broadcast_to(lam.reshape(1, 1), (1, HEAD_TILE)).astype(F32),
      sub_norm.reshape(1, HEAD_TILE))


S5_TB = 64
S5_GPT = 128 // S5_GROUP_CH
S5_TILES_IN = GROUP_W // 128
S5_TILES_ST = S5_GROUPS * S5_STATE // 128
S5_ST_PER_IN = S5_TILES_ST // S5_TILES_IN


def _s5_kernel(uf_ref, ub_ref, wd_ref, wr_ref, tab_ref, yf_ref, yb_ref, bu_re, bu_im, st_re, st_im):
    nb, tb, _ = uf_ref.shape
    rows = nb * tb
    half = S5_ST_PER_IN * 128

    @pl.when(pl.program_id(0) == 0)
    def _():
        st_re[...] = jnp.zeros_like(st_re)
        st_im[...] = jnp.zeros_like(st_im)

    for d, u_ref in enumerate((uf_ref, ub_ref)):
        x = u_ref[...].reshape(rows, GROUP_W).astype(BF16)
        for jt in range(S5_TILES_IN):
            bu = jnp.dot(x[:, jt * 128:(jt + 1) * 128], wd_ref[d, jt], preferred_element_type=F32)
            for c in range(S5_ST_PER_IN):
                bu_re[jt * S5_ST_PER_IN + c, d * rows:(d + 1) * rows, :] = bu[:, c * 128:(c + 1) * 128]
                bu_im[jt * S5_ST_PER_IN + c, d * rows:(d + 1) * rows, :] = bu[:, half + c * 128:half + (c + 1) * 128]

    def lane_tile(c, carry):
        for d in range(2):
            levels = [(tab_ref[d, 2 * k, c], tab_ref[d, 2 * k + 1, c], (1 << k) if d == 0 else 8 - (1 << k))
                      for k in range(3)]
            cr, ci = tab_ref[d, 6, c], tab_ref[d, 7, c]
            tiles = range(tb // 8) if d == 0 else range(tb // 8 - 1, -1, -1)
            for bi in range(nb):
                row = d * nb + bi
                car_r = st_re[c, row:row + 1, :]
                car_i = st_im[c, row:row + 1, :]
                for tt in tiles:
                    off = row * tb + tt * 8
                    hr = bu_re[c, off:off + 8, :]
                    hi = bu_im[c, off:off + 8, :]
                    for ar, ai, shift in levels:
                        pr = pltpu.roll(hr, shift, axis=0)
                        pi = pltpu.roll(hi, shift, axis=0)
                        hr, hi = hr + (ar * pr - ai * pi), hi + (ar * pi + ai * pr)
                    hr, hi = hr + (cr * car_r - ci * car_i), hi + (cr * car_i + ci * car_r)
                    bu_re[c, off:off + 8, :] = hr
                    bu_im[c, off:off + 8, :] = hi
                    last = 7 if d == 0 else 0
                    car_r, car_i = hr[last:last + 1], hi[last:last + 1]
                st_re[c, row:row + 1, :] = car_r
                st_im[c, row:row + 1, :] = car_i
        return carry

    lax.fori_loop(0, S5_TILES_ST, lane_tile, 0)

    for d, y_ref in enumerate((yf_ref, yb_ref)):
        for jt in range(S5_TILES_IN):
            tiles = [bu_re[jt * S5_ST_PER_IN + c, d * rows:(d + 1) * rows, :] for c in range(S5_ST_PER_IN)]
            tiles += [bu_im[jt * S5_ST_PER_IN + c, d * rows:(d + 1) * rows, :] for c in range(S5_ST_PER_IN)]
            h = jnp.concatenate(tiles, axis=1).astype(BF16)
            y = jnp.dot(h, wr_ref[d, jt], preferred_element_type=F32)
            y_ref[:, :, jt * 128:(jt + 1) * 128] = y.reshape(nb, tb, 128)


def _rev_block(i, nlc, ntot):
    return jnp.where(i < nlc, nlc - 1 - i, ntot - 1 - (i - nlc))


def s5_scan(z, lc, wd, wr, tab):
    b, n, _ = z.shape
    tb = min(S5_TB, lc)
    nlc, ntot = lc // tb, n // tb
    blk = (b, tb, GROUP_W)
    full = lambda a: pl.BlockSpec(a.shape, lambda i: (0,) * a.ndim)
    return pl.pallas_call(
        _s5_kernel,
        grid=(ntot,),
        in_specs=[pl.BlockSpec(blk, lambda i: (0, i, 0)),
                  pl.BlockSpec(blk, lambda i: (0, _rev_block(i, nlc, ntot), 0)),
                  full(wd), full(wr), full(tab)],
        out_specs=[pl.BlockSpec(blk, lambda i: (0, i, 0)),
                   pl.BlockSpec(blk, lambda i: (0, _rev_block(i, nlc, ntot), 0))],
        out_shape=[jax.ShapeDtypeStruct((b, n, GROUP_W), F32)] * 2,
        scratch_shapes=[pltpu.VMEM((S5_TILES_ST, 2 * b * tb, 128), F32)] * 2
        + [pltpu.VMEM((S5_TILES_ST, 8, 128), F32)] * 2,
        compiler_params=_cparams(("arbitrary",)),
        name="s5_scan",
    )(z, z, wd, wr, tab)


def _cmul(ar, ai, br, bi):
    return ar * br - ai * bi, ar * bi + ai * br


def s5_operators(a_re, a_im, log_step, b_re, b_im, c_re, c_im, nb):
    step = jnp.exp(log_step)[..., None]
    mag = jnp.exp(a_re * step)
    lr, li = mag * jnp.cos(a_im * step), mag * jnp.sin(a_im * step)
    den = a_re * a_re + a_im * a_im
    nr, ni = lr - 1.0, li
    fr = (nr * a_re + ni * a_im) / den
    fi = (ni * a_re - nr * a_im) / den
    bbr, bbi = _cmul(fr[..., None], fi[..., None], b_re, b_im)
    eye = jnp.eye(S5_GPT, dtype=F32)

    def drive(w):
        w = w.reshape(2, S5_TILES_IN, S5_GPT, S5_STATE, S5_GROUP_CH)
        w = jnp.einsum('dtgph,gk->dtghkp', w, eye)
        return w.reshape(2, S5_TILES_IN, 128, S5_GPT * S5_STATE)

    def readout(w):
        w = w.reshape(2, S5_TILES_IN, S5_GPT, S5_GROUP_CH, S5_STATE)
        w = jnp.einsum('dtghp,gk->dtgpkh', w, eye)
        return w.reshape(2, S5_TILES_IN, S5_GPT * S5_STATE, 128)

    wd = jnp.concatenate([drive(bbr), drive(bbi)], axis=-1).astype(BF16)
    wr = jnp.concatenate([readout(c_re), readout(-c_im)], axis=-2).astype(BF16)

    log_mag = (a_re * step).reshape(2, S5_TILES_ST, 1, 128)
    ang = (a_im * step).reshape(2, S5_TILES_ST, 1, 128)
    r8 = jnp.arange(8, dtype=F32).reshape(1, 1, 8, 1)

    def power(e, keep):
        e = jnp.broadcast_to(jnp.asarray(e, F32), (2, S5_TILES_ST, 8, 128))
        m = jnp.exp(log_mag * e)
        k = jnp.broadcast_to(keep, e.shape)
        return jnp.where(k, m * jnp.cos(ang * e), 0.0), jnp.where(k, m * jnp.sin(ang * e), 0.0)

    is_fwd = jnp.arange(2).reshape(2, 1, 1, 1) == 0
    tabs = []
    for s in (1, 2, 4):
        tabs += power(float(s), jnp.where(is_fwd, r8 >= s, r8 < 8 - s))
    tabs += power(jnp.where(is_fwd, r8 + 1.0, 8.0 - r8), True)
    return wd, wr, jnp.stack(tabs, axis=1)


def _s5_post_kernel(yf_ref, yb_ref, u_ref, d_ref, w_ref, o_ref):
    y = jax.nn.gelu(yf_ref[...] + yb_ref[...] + d_ref[...] * u_ref[...])
    gv = jnp.dot(y.astype(BF16), w_ref[...], preferred_element_type=F32)
    o_ref[...] = (gv[:, :GROUP_W] * jax.nn.sigmoid(gv[:, GROUP_W:])).astype(o_ref.dtype)


def s5_post(yf, yb, z, d_skip, w_glu):
    t = yf.shape[0]
    tm = _pick(t, (512, 256, 128))
    row = pl.BlockSpec((tm, GROUP_W), lambda i: (i, 0))
    return pl.pallas_call(
        _s5_post_kernel,
        grid=(t // tm,),
        in_specs=[row, row, row, pl.BlockSpec((1, GROUP_W), lambda i: (0, 0)),
                  pl.BlockSpec((GROUP_W, 2 * GROUP_W), lambda i: (0, 0))],
        out_specs=row,
        out_shape=jax.ShapeDtypeStruct((t, GROUP_W), BF16),
        compiler_params=_cparams(("parallel",)),
        name="s5_post",
    )(yf, yb, z, d_skip.reshape(1, GROUP_W), w_glu)


RW_TB = 32
RW_ROWS = 8
RW_TILES = GROUP_W // 128


def _split_bf16(x, parts):
    out = []
    for _ in range(parts - 1):
        hi = x.astype(BF16)
        out.append(hi)
        x = x - hi.astype(F32)
    out.append(x.astype(BF16))
    return out


def _permute_rows(p, x, parts):
    return sum(jnp.dot(p, piece, preferred_element_type=F32) for piece in _split_bf16(x, parts))


def _rwkv_to_lanes(fwd_ref, bwd_ref, p_ref, xt_s, dst_ref, parts):
    nb, tb, _ = fwd_ref.shape
    x = jnp.concatenate([fwd_ref[...].reshape(nb * tb, GROUP_W), bwd_ref[...].reshape(nb * tb, GROUP_W)], axis=0)
    xt_s[...] = _permute_rows(p_ref[...], x, parts)
    low = lax.broadcasted_iota(jnp.int32, (RW_HEAD, 128), 1) < RW_HEAD

    def pair(tp, carry):
        z = xt_s[pl.ds(pl.multiple_of(tp * 2 * RW_ROWS, 2 * RW_ROWS), 2 * RW_ROWS), :]
        zz = jnp.concatenate([z[p * RW_ROWS:(p + 1) * RW_ROWS, jt * 128:(jt + 1) * 128]
                              for p in range(2) for jt in range(RW_TILES)], axis=0)
        tt = zz.T
        top, bot = tt[:RW_HEAD], tt[RW_HEAD:]
        dst_ref[2 * tp] = jnp.where(low, top, pltpu.roll(bot, RW_HEAD, axis=1))
        dst_ref[2 * tp + 1] = jnp.where(low, pltpu.roll(top, RW_HEAD, axis=1), bot)
        return carry

    lax.fori_loop(0, tb // 2, pair, 0, unroll=8)


def _rwkv_from_lanes(y_s, pt_ref, xt_s, of_ref, ob_ref):
    nb, tb, _ = of_ref.shape
    low = lax.broadcasted_iota(jnp.int32, (RW_HEAD, 128), 1) < RW_HEAD

    def pair(tp, carry):
        y0, y1 = y_s[2 * tp], y_s[2 * tp + 1]
        top = jnp.where(low, y0, pltpu.roll(y1, RW_HEAD, axis=1))
        bot = jnp.where(low, pltpu.roll(y0, RW_HEAD, axis=1), y1)
        zz = jnp.concatenate([top, bot], axis=0).T
        base = pl.multiple_of(tp * 2 * RW_ROWS, 2 * RW_ROWS)
        for p in range(2):
            for jt in range(RW_TILES):
                blk = zz[(p * RW_TILES + jt) * RW_ROWS:(p * RW_TILES + jt + 1) * RW_ROWS]
                xt_s[pl.ds(base + p * RW_ROWS, RW_ROWS), jt * 128:(jt + 1) * 128] = blk
        return carry

    lax.fori_loop(0, tb // 2, pair, 0, unroll=8)
    y = _permute_rows(pt_ref[...], xt_s[...], 2)
    of_ref[...] = y[:nb * tb].reshape(nb, tb, GROUP_W)
    ob_ref[...] = y[nb * tb:].reshape(nb, tb, GROUP_W)


def _rwkv_kernel(rf_ref, kf_ref, vf_ref, wf_ref, af_ref, rb_ref, kb_ref, vb_ref, wb_ref, ab_ref,
                 p_ref, pt_ref, c_ref, of_ref, ob_ref,
                 r_s, k_s, v_s, w_s, a_s, y_s, xt_s, kk_s, bb_s, s_ref):
    tb = rf_ref.shape[1]

    @pl.when(pl.program_id(0) == 0)
    def _():
        s_ref[...] = jnp.zeros_like(s_ref)

    _rwkv_to_lanes(rf_ref, rb_ref, p_ref, xt_s, r_s, 2)
    _rwkv_to_lanes(kf_ref, kb_ref, p_ref, xt_s, k_s, 2)
    _rwkv_to_lanes(vf_ref, vb_ref, p_ref, xt_s, v_s, 2)
    _rwkv_to_lanes(wf_ref, wb_ref, p_ref, xt_s, w_s, 3)
    _rwkv_to_lanes(af_ref, ab_ref, p_ref, xt_s, a_s, 2)

    def step(t, carry):
        k = k_s[t]
        kk = k * c_ref[2]
        kk = kk * lax.rsqrt(jnp.maximum(jnp.sum(kk * kk, axis=0, keepdims=True), 1e-12))
        z = -(w_s[t] + c_ref[0])
        w = -(jnp.maximum(z, 0.0) + jnp.log(1.0 + jnp.exp(-jnp.abs(z)))) - 0.5
        a = jax.nn.sigmoid(a_s[t] + c_ref[1])
        kd = k * (1.0 + (a - 1.0) * c_ref[3])
        bonus = jnp.sum(r_s[t] * kd * c_ref[4], axis=0, keepdims=True)
        w_s[t] = jnp.exp(-jnp.exp(w))
        k_s[t] = kd
        kk_s[...] = kk
        bb_s[...] = kk * a

        def vgroup(vg, carry2):
            off = pl.multiple_of(vg * 8, 8)
            vrows = v_s[t, pl.ds(off, 8), :]
            outs = []
            for j in range(8):
                sv = s_ref[off + j]
                sa = jnp.sum(sv * kk_s[...], axis=0, keepdims=True)
                sn = sv * w_s[t] - sa * bb_s[...] + vrows[j:j + 1] * k_s[t]
                s_ref[off + j] = sn
                outs.append(jnp.sum(sn * r_s[t], axis=0, keepdims=True))
            y_s[t, pl.ds(off, 8), :] = jnp.concatenate(outs, axis=0) + bonus * vrows
            return carry2

        return lax.fori_loop(0, RW_HEAD // 8, vgroup, carry)

    lax.fori_loop(0, tb, step, 0)
    _rwkv_from_lanes(y_s, pt_ref, xt_s, of_ref, ob_ref)


def rwkv_scan(zs, lr, consts, lc):
    b, n, _ = zs.shape
    assert 2 * b == RW_ROWS
    tb = RW_TB
    nlc, ntot = lc // tb, n // tb
    rows = RW_ROWS * tb
    r_out = jnp.arange(rows)
    step, db = r_out // RW_ROWS, r_out % RW_ROWS
    src = db * tb + jnp.where(db < b, step, tb - 1 - step)
    perm = (src[:, None] == jnp.arange(rows)[None, :]).astype(BF16)
    blk = (b, tb, GROUP_W)
    fwd = lambda col: pl.BlockSpec(blk, lambda i: (0, i, col))
    bwd = lambda col: pl.BlockSpec(blk, lambda i: (0, _rev_block(i, nlc, ntot), col))
    full = lambda a: pl.BlockSpec(a.shape, lambda i: (0,) * a.ndim)
    tile = pltpu.VMEM((tb, RW_HEAD, 128), F32)
    return pl.pallas_call(
        _rwkv_kernel,
        grid=(ntot,),
        in_specs=[fwd(0), fwd(1), fwd(2), fwd(0), fwd(2), bwd(0), bwd(1), bwd(2), bwd(1), bwd(3),
                  full(perm), full(perm), full(consts)],
        out_specs=[fwd(0), bwd(0)],
        out_shape=[jax.ShapeDtypeStruct((b, n, GROUP_W), F32)] * 2,
        scratch_shapes=[tile] * 6 + [pltpu.VMEM((rows, GROUP_W), F32),
                                     pltpu.VMEM((RW_HEAD, 128), F32), pltpu.VMEM((RW_HEAD, 128), F32),
                                     pltpu.VMEM((RW_HEAD, RW_HEAD, 128), F32)],
        compiler_params=_cparams(("arbitrary",)),
        name="rwkv_scan",
    )(zs, zs, zs, lr, lr, zs, zs, zs, lr, lr, perm, perm.T, consts)


def _moe_row_copy(tok_ref, x_hbm, xbuf, sem, blk, slot, r):
    tok = tok_ref[blk * MOE_BLOCK + r]
    return pltpu.make_async_copy(x_hbm.at[pl.ds(tok, 1)], xbuf.at[slot, pl.ds(r, 1)], sem.at[slot])


def _moe_kernel(be_ref, nu_ref, tok_ref, x_hbm, wg_ref, wu_ref, wd_ref, o_ref, xbuf, sem, g_s, u_s, d_s):
    i = pl.program_id(0)
    used = i < nu_ref[0]
    changed = jnp.logical_or(i == 0, be_ref[i] != be_ref[jnp.maximum(i - 1, 0)])
    slot = i % 2

    def gather(blk, into):
        def start(r, carry):
            _moe_row_copy(tok_ref, x_hbm, xbuf, sem, blk, into, r).start()
            return carry
        lax.fori_loop(0, MOE_BLOCK, start, 0)

    @pl.when(i == 0)
    def _():
        gather(0, 0)

    @pl.when(i + 1 < nu_ref[0])
    def _():
        gather(i + 1, 1 - slot)

    @pl.when(jnp.logical_and(used, changed))
    def _():
        g_s[...] = wg_ref[...].astype(BF16)
        u_s[...] = wu_ref[...].astype(BF16)
        d_s[...] = wd_ref[...].astype(BF16)

    @pl.when(used)
    def _():
        def wait(r, carry):
            _moe_row_copy(tok_ref, x_hbm, xbuf, sem, i, slot, r).wait()
            return carry
        lax.fori_loop(0, MOE_BLOCK, wait, 0)
        x = xbuf[slot].astype(BF16)
        g = jnp.dot(x, g_s[...], preferred_element_type=F32)
        u = jnp.dot(x, u_s[...], preferred_element_type=F32)
        h = (g * jax.nn.sigmoid(g) * u).astype(BF16)
        o_ref[...] = jnp.dot(h, d_s[...], preferred_element_type=F32)

    @pl.when(jnp.logical_not(used))
    def _():
        o_ref[...] = jnp.zeros_like(o_ref)


def moe_ffn(block_exp, n_used, slot_tok, x, wg, wu, wd, layer):
    cap = slot_tok.shape[0]
    d = x.shape[1]
    hid = wg.shape[-1]
    nb = cap // MOE_BLOCK
    once = pl.Buffered(1)
    wspec = lambda shape: pl.BlockSpec((None, None) + shape, lambda i, be, nu, tok: (layer, be[i], 0, 0),
                                       pipeline_mode=once)
    grid_spec = pltpu.PrefetchScalarGridSpec(
        num_scalar_prefetch=3,
        grid=(nb,),
        in_specs=[pl.BlockSpec(memory_space=pl.ANY), wspec((d, hid)), wspec((d, hid)), wspec((hid, d))],
        out_specs=pl.BlockSpec((MOE_BLOCK, d), lambda i, be, nu, tok: (i, 0)),
        scratch_shapes=[pltpu.VMEM((2, MOE_BLOCK, d), F32), pltpu.SemaphoreType.DMA((2,)),
                        pltpu.VMEM((d, hid), BF16), pltpu.VMEM((d, hid), BF16), pltpu.VMEM((hid, d), BF16)],
    )
    return pl.pallas_call(
        _moe_kernel,
        grid_spec=grid_spec,
        out_shape=jax.ShapeDtypeStruct((cap, d), F32),
        compiler_params=_cparams(("arbitrary",)),
        name="moe_ffn",
    )(block_exp, n_used, slot_tok, x, wg, wu, wd)


def _rms(x, g):
    return x * lax.rsqrt(jnp.mean(x * x, axis=-1, keepdims=True) + NORM_EPS) * g


def _axial_rope(rows, dim):
    n_freq = dim // 4
    inv = ROPE_BASE ** (-jnp.arange(n_freq, dtype=F32) / n_freq)
    row = jnp.repeat(jnp.arange(rows, dtype=F32), GRID_W)
    col = jnp.tile(jnp.arange(GRID_W, dtype=F32), rows)
    ang = jnp.concatenate([row[:, None] * inv, col[:, None] * inv], axis=-1)
    return jnp.cos(ang), jnp.sin(ang)


def _rope(x, cos, sin):
    half = x.shape[-1] // 2
    x1, x2 = x[..., :half], x[..., half:]
    return jnp.concatenate([x1 * cos - x2 * sin, x1 * sin + x2 * cos], axis=-1)


def _rope_lat(x, cos, sin, lc):
    extra = (None,) * (x.ndim - 3)
    cs = cos[(slice(None),) + extra]
    sn = sin[(slice(None),) + extra]
    return jnp.concatenate([x[:, :lc], _rope(x[:, lc:], cs, sn)], axis=1)


def _seq_rev(t, lc):
    return jnp.concatenate([jnp.flip(t[:, :lc], 1), jnp.flip(t[:, lc:], 1)], axis=1)


def _mm3(x, w, **kw):
    b, n, k = x.shape
    return matmul(x.reshape(b * n, k), w, **kw).reshape(b, n, -1)


def _mla_mixer(cq, ckv, krope, lc, q_a_norm, kv_a_norm, w_uq, w_ukv, q_norm, k_norm, cos, sin):
    b, n, _ = cq.shape
    pad = HEAD_TILE - MLA_ROPE
    wq = w_uq.reshape(MLA_Q_RANK, MLA_HEADS, MLA_NOPE + MLA_ROPE)
    wq = jnp.concatenate([wq[:, :, :MLA_NOPE].reshape(MLA_Q_RANK, -1),
                          jnp.pad(wq[:, :, MLA_NOPE:], ((0, 0), (0, 0), (0, pad))).reshape(MLA_Q_RANK, -1)], axis=1)
    q = _mm3(_rms(cq, q_a_norm), wq).reshape(b, n, 2, MLA_HEADS, HEAD_TILE)
    kv = _mm3(_rms(ckv, kv_a_norm), w_ukv).reshape(b, n, MLA_HEADS, MLA_NOPE + MLA_V)
    scale = float(MLA_NOPE + MLA_ROPE) ** -0.5
    q_nope = _rms(q[:, :, 0], q_norm[:MLA_NOPE]) * scale
    q_rope = _rope_lat(_rms(q[:, :, 1, :, :MLA_ROPE], q_norm[MLA_NOPE:]), cos, sin, lc) * scale
    q_rope = jnp.pad(q_rope, ((0, 0), (0, 0), (0, 0), (0, pad)))
    k_nope = _rms(kv[..., :MLA_NOPE], k_norm[:MLA_NOPE])
    k_rope = _rope_lat(_rms(krope, k_norm[MLA_NOPE:]), cos, sin, lc)
    qf = jnp.stack([q_nope, q_rope], axis=2).reshape(b, n, 2 * MLA_HEADS * HEAD_TILE).astype(BF16)
    kvf = jnp.concatenate([k_nope, kv[..., MLA_NOPE:]], axis=-1).reshape(b, n, -1).astype(BF16)
    return mla_attention(qf, kvf, jnp.pad(k_rope, ((0, 0), (0, 0), (0, pad))).astype(BF16), lc)


def _diff_mixer(dq, dk, dv, lc, q_norm, k_norm, lam_vecs, sub_norm, lam_init, cos, sin):
    b, n, _ = dq.shape
    q = _rope_lat(_rms(dq.reshape(b, n, DIFF_HEADS, 2, DIFF_HEAD), q_norm), cos, sin, lc) * float(DIFF_HEAD) ** -0.5
    k = _rope_lat(_rms(dk.reshape(b, n, DIFF_HEADS, 2, DIFF_HEAD), k_norm), cos, sin, lc)
    lam = jnp.exp(jnp.sum(lam_vecs[0] * lam_vecs[1])) - jnp.exp(jnp.sum(lam_vecs[2] * lam_vecs[3])) + lam_init
    flat = lambda x: x.reshape(b, n, GROUP_W).astype(BF16)
    return diff_attention(flat(q), flat(k), dv.astype(BF16), lam, sub_norm, lc, 1.0 - lam_init)


SHIFT_HALO = 8


def _shift_kernel(z_ref, p_ref, n_ref, mu_ref, o_ref, *, blocks_lc, blocks):
    i = pl.program_id(1)
    z = z_ref[0]
    tb = z.shape[0]
    row = lax.broadcasted_iota(jnp.int32, (tb, 1), 0)
    starts = jnp.logical_or(i == 0, i == blocks_lc)
    ends = jnp.logical_or(i == blocks_lc - 1, i == blocks - 1)
    before = jnp.where(starts, 0.0, p_ref[0, SHIFT_HALO - 1:SHIFT_HALO, :])
    after = jnp.where(ends, 0.0, n_ref[0, 0:1, :])
    z_prev = jnp.where(row == 0, before, pltpu.roll(z, 1, axis=0))
    z_next = jnp.where(row == tb - 1, after, pltpu.roll(z, tb - 1, axis=0))
    o_ref[0] = z + mu_ref[0:1] * (z_prev - z) + mu_ref[1:2] * (z_next - z)


def centred_shift(z, mu, lc):
    b, n, _ = z.shape
    c = mu.shape[1]
    tb = lc
    per = tb // SHIFT_HALO
    kern = functools.partial(_shift_kernel, blocks_lc=lc // tb, blocks=n // tb)
    return pl.pallas_call(
        kern,
        grid=(b, n // tb),
        in_specs=[pl.BlockSpec((1, tb, c), lambda bi, i: (bi, i, 0)),
                  pl.BlockSpec((1, SHIFT_HALO, c), lambda bi, i: (bi, jnp.maximum(i * per - 1, 0), 0)),
                  pl.BlockSpec((1, SHIFT_HALO, c), lambda bi, i: (bi, jnp.minimum((i + 1) * per, n // SHIFT_HALO - 1), 0)),
                  pl.BlockSpec((2, c), lambda bi, i: (0, 0))],
        out_specs=pl.BlockSpec((1, tb, c), lambda bi, i: (bi, i, 0)),
        out_shape=jax.ShapeDtypeStruct((b, n, c), F32),
        compiler_params=_cparams(("parallel", "parallel")),
        name="centred_shift",
    )(z, z, z, mu)


def _seg_sum(x, j):
    return sum(jnp.dot(piece, j, preferred_element_type=F32) for piece in _split_bf16(x, 2))


def _rwkv_finish_kernel(of_ref, ob_ref, g_ref, lng_ref, lnb_ref, j_ref, o_ref):
    j = j_ref[...]
    for jt in range(RW_TILES):
        sl = slice(jt * 128, (jt + 1) * 128)
        x = of_ref[:, sl] + ob_ref[:, sl]
        xc = x - _seg_sum(x, j) * (1.0 / RW_HEAD)
        var = _seg_sum(xc * xc, j) * (1.0 / RW_HEAD)
        y = xc * lax.rsqrt(var + RW_GN_EPS)
        o_ref[:, sl] = ((y * lng_ref[:, sl] + lnb_ref[:, sl]) * g_ref[:, sl]).astype(o_ref.dtype)


def rwkv_finish(o_f, o_b, lr, ln_g, ln_b):
    t = o_f.shape[0]
    tm = _pick(t, (512, 256, 128))
    half = jnp.arange(128) // RW_HEAD
    ones = (half[:, None] == half[None, :]).astype(BF16)
    row = pl.BlockSpec((tm, GROUP_W), lambda i: (i, 0))
    vec = pl.BlockSpec((1, GROUP_W), lambda i: (0, 0))
    return pl.pallas_call(
        _rwkv_finish_kernel,
        grid=(t // tm,),
        in_specs=[row, row, pl.BlockSpec((tm, GROUP_W), lambda i: (i, 4)), vec, vec,
                  pl.BlockSpec((128, 128), lambda i: (0, 0))],
        out_specs=row,
        out_shape=jax.ShapeDtypeStruct((t, GROUP_W), BF16),
        compiler_params=_cparams(("parallel",)),
        name="rwkv_finish",
    )(o_f, o_b, lr, ln_g.reshape(1, GROUP_W), ln_b.reshape(1, GROUP_W), ones)


def _rwkv_mixer(z, lc, mu, w0, w2, a0, a2, g2, k_k, k_a, r_k, ln_g, ln_b):
    b, n, _ = z.shape
    z = centred_shift(z, mu, lc)
    o = 3 * GROUP_W
    wl, al, gl = z[..., o:o + 64], z[..., o + 64:o + 128], z[..., o + 128:o + 256]
    lr_in = jnp.concatenate([jnp.tanh(wl), al, jax.nn.sigmoid(gl)], axis=-1)
    zw = jnp.zeros((RW_W_RANK, GROUP_W), F32)
    zg = jnp.zeros((RW_G_RANK, GROUP_W), F32)
    w_lr = jnp.concatenate([
        jnp.concatenate([w2[0], w2[1], zw, zw, zw], axis=1),
        jnp.concatenate([zw, zw, a2[0], a2[1], zw], axis=1),
        jnp.concatenate([zg, zg, zg, zg, g2], axis=1)], axis=0)
    lr = _mm3(lr_in, w_lr)

    def chain_tile(p_fwd, p_bwd):
        per_dir = [p.reshape(RW_TILES, 2, RW_HEAD).transpose(2, 1, 0) for p in (p_fwd, p_bwd)]
        return jnp.stack([per_dir[0]] * b + [per_dir[1]] * b, axis=-1).reshape(RW_HEAD, 128)

    consts = jnp.stack([chain_tile(w0[0], w0[1]), chain_tile(a0[0], a0[1]), chain_tile(k_k, k_k),
                        chain_tile(k_a, k_a), chain_tile(r_k.reshape(-1), r_k.reshape(-1))])
    o_f, o_b = rwkv_scan(z, lr, consts, lc)
    flat = lambda t: t.reshape(b * n, t.shape[-1])
    return rwkv_finish(flat(o_f), flat(o_b), flat(lr), ln_g, ln_b)


def _hier_moe(h, wr, br_coarse, br_fine, wg, wu, wd, layer):
    t, d = h.shape
    logits = matmul(h, wr)
    p_group = jax.nn.softmax(logits[:, :MOE_GROUPS] + br_coarse, axis=-1)
    group = jnp.argmax(p_group, axis=-1).astype(jnp.int32)
    p_sel = jnp.take_along_axis(p_group, group[:, None], axis=-1)
    fine = (logits[:, MOE_GROUPS:MOE_GROUPS + MOE_EXPERTS] + br_fine).reshape(t, MOE_GROUPS, MOE_PER_GROUP)
    fine = jnp.take_along_axis(fine, group[:, None, None], axis=1)[:, 0]
    top_v, top_i = lax.top_k(fine, MOE_TOP_K)
    weight = p_sel * jax.nn.softmax(top_v, axis=-1)
    expert = group[:, None] * MOE_PER_GROUP + top_i.astype(jnp.int32)
    flat_e = expert.reshape(-1)
    n_assign = flat_e.shape[0]
    onehot = (flat_e[:, None] == jnp.arange(MOE_EXPERTS, dtype=jnp.int32)[None, :]).astype(jnp.int32)
    running = jnp.cumsum(onehot, axis=0)
    counts = running[-1]
    padded = (counts + MOE_BLOCK - 1) // MOE_BLOCK * MOE_BLOCK
    pad_end = jnp.cumsum(padded)
    pad_start = pad_end - padded
    dest = jnp.sum(onehot * (running - 1 + pad_start[None, :]), axis=1).astype(jnp.int32)
    n_blocks = -(-(n_assign + MOE_EXPERTS * (MOE_BLOCK - 1)) // MOE_BLOCK)
    cap = n_blocks * MOE_BLOCK
    slot_tok = jnp.zeros((cap,), jnp.int32).at[dest].set(jnp.arange(n_assign, dtype=jnp.int32) // MOE_TOP_K)
    block_start = jnp.arange(n_blocks, dtype=jnp.int32) * MOE_BLOCK
    block_exp = jnp.minimum(jnp.sum((pad_end[None, :] <= block_start[:, None]).astype(jnp.int32), axis=1),
                            MOE_EXPERTS - 1)
    n_used = (pad_end[-1:] // MOE_BLOCK).astype(jnp.int32)
    yb = moe_ffn(block_exp, n_used, slot_tok, h, wg, wu, wd, layer)
    slot_of = dest.reshape(t, MOE_TOP_K)
    return weight, yb[slot_of[:, 0]], yb[slot_of[:, 1]]


def _combine_kernel(x_ref, y0_ref, y1_ref, w_ref, g_ref, o_ref):
    w = w_ref[0]
    o_ref[0] = x_ref[0] + g_ref[0] * (w[:, 0:1] * y0_ref[0] + w[:, 1:2] * y1_ref[0])


def moe_combine(xs, weight, y0, y1, gate, lc, skip):
    b, ntok, d = xs.shape
    r = ntok - skip
    tm, tn = lc, 1024
    off = skip // tm
    blk = lambda idx: pl.BlockSpec((1, tm, tn), idx)
    rows = lambda bi, i, j: (bi, i, j)
    return pl.pallas_call(
        _combine_kernel,
        grid=(b, r // tm, d // tn),
        in_specs=[blk(lambda bi, i, j: (bi, off + i, j)), blk(rows), blk(rows),
                  pl.BlockSpec((1, tm, MOE_TOP_K), lambda bi, i, j: (bi, i, 0)),
                  pl.BlockSpec((1, 1, tn), lambda bi, i, j: (bi * 2 + jnp.minimum(off + i, 1), 0, j))],
        out_specs=blk(rows),
        out_shape=jax.ShapeDtypeStruct((b, r, d), F32),
        compiler_params=_cparams(("parallel", "parallel", "parallel")),
        name="moe_combine",
    )(xs, y0.reshape(b, r, d), y1.reshape(b, r, d), weight.reshape(b, r, MOE_TOP_K), gate.reshape(b * 2, 1, d))


def kernel(x, c, ctx, c_ctx, mod_w, mod_b, norm1_g, norm2_g, w_in, w_out, s5_a_re, s5_a_im, s5_log_step, s5_b_re, s5_b_im, s5_c_re, s5_c_im, s5_d, s5_w_glu, mla_q_a_norm, mla_kv_a_norm, mla_w_uq, mla_w_ukv, mla_q_norm, mla_k_norm, rw_mu, rw_w0, rw_w2, rw_a0, rw_a2, rw_g2, rw_k_k, rw_k_a, rw_r_k, rw_ln_g, rw_ln_b, diff_q_norm, diff_k_norm, diff_lambda, diff_sub_norm, moe_wr_coarse, moe_br_coarse, moe_wr_fine, moe_br_fine, moe_w_gate, moe_w_up, moe_w_down):
    b, n, d = x.shape
    lc = ctx.shape[1]
    ntok = lc + n
    depth = mod_w.shape[0]
    rows = n // GRID_W
    cos_m, sin_m = _axial_rope(rows, MLA_ROPE)
    cos_d, sin_d = _axial_rope(rows, DIFF_HEAD)
    act = jnp.concatenate([jax.nn.silu(c), jax.nn.silu(c_ctx)[None],
                           jnp.zeros((8 - b - 1, d), F32)], axis=0)
    mod_all = mod_matmul(act, mod_w) + mod_b[:, None, :]
    xs = jnp.concatenate([ctx, x], axis=1)

    for l in range(depth):
        last = l == depth - 1
        lam_init = 0.8 - 0.6 * math.exp(-0.3 * l)
        mod_lat = mod_all[l, :b].reshape(b, N_MOD, d)
        mod_ctx = jnp.broadcast_to(mod_all[l, b].reshape(1, N_MOD, d), (b, N_MOD, d))
        mods = jnp.stack([mod_ctx, mod_lat], axis=2)

        h = normmod(xs, norm1_g[l], mods[:, 0], mods[:, 1], lc).reshape(b * ntok, d)
        tm_in = _pick(b * ntok, (1024, 512, 256, 128))
        za = matmul(h, w_in[l][:, :ZA_WIDTH].astype(BF16), tn=ZA_WIDTH).reshape(b, ntok, ZA_WIDTH)
        zb = matmul(h, w_in[l][:, ZA_SPLIT:].astype(BF16), tm=tm_in,
                    tn=_pick(ZB_WIDTH, (1280, 640, 128))).reshape(b, ntok, ZB_WIDTH)

        wd, wr, lam = s5_operators(s5_a_re[l], s5_a_im[l], s5_log_step[l], s5_b_re[l], s5_b_im[l],
                                   s5_c_re[l], s5_c_im[l], b)
        yf, yr = s5_scan(za, lc, wd, wr, lam)
        ya = s5_post(yf.reshape(b * ntok, GROUP_W), yr.reshape(b * ntok, GROUP_W),
                     za.reshape(b * ntok, ZA_WIDTH), s5_d[l], s5_w_glu[l].astype(BF16))
        yb = _mla_mixer(za[..., ZA_CQ:ZA_CKV], za[..., ZA_CKV:ZA_KR], za[..., ZA_KR:ZA_SPLIT], lc,
                        mla_q_a_norm[l], mla_kv_a_norm[l], mla_w_uq[l], mla_w_ukv[l],
                        mla_q_norm[l], mla_k_norm[l], cos_m, sin_m)
        yc = _rwkv_mixer(zb, lc, rw_mu[l], rw_w0[l], rw_w2[l], rw_a0[l], rw_a2[l],
                         rw_g2[l], rw_k_k[l], rw_k_a[l], rw_r_k[l], rw_ln_g[l], rw_ln_b[l])
        yd = _diff_mixer(zb[..., ZB_DQ:ZB_DK], zb[..., ZB_DK:ZB_DV], zb[..., ZB_DV:], lc,
                         diff_q_norm[l], diff_k_norm[l], diff_lambda[l], diff_sub_norm[l], lam_init,
                         cos_d, sin_d)
        flat = lambda y: y.reshape(b * ntok, GROUP_W)
        xs = out_proj((ya, flat(yb), flat(yc), flat(yd)), w_out[l].astype(BF16), xs, mods[:, 2], lc)

        h2 = normmod(xs, norm2_g[l], mods[:, 3], mods[:, 4], lc, out_dtype=F32)
        wrt = jnp.concatenate([moe_wr_coarse[l], moe_wr_fine[l],
                               jnp.zeros((d, 128 - MOE_GROUPS - MOE_EXPERTS), F32)], axis=1)
        experts = (moe_w_gate, moe_w_up, moe_w_down, l)
        skip = lc if last else 0
        routed = _hier_moe(h2[:, skip:].reshape(b * (ntok - skip), d), wrt, moe_br_coarse[l], moe_br_fine[l],
                           *experts)
        xs = moe_combine(xs, *routed, mods[:, 5], lc, skip)
    return xs
```

```python
import functools
import math

import jax
import jax.numpy as jnp
from jax import lax
from jax.experimental import pallas as pl
from jax.experimental.pallas import tpu as pltpu

F32 = jnp.float32
BF16 = jnp.bfloat16

D_MODEL = 4096
N_MOD = 6
GROUP_W = D_MODEL // 4
GRID_W = 64
ROPE_BASE = 10000.0
NORM_EPS = 1e-6

S5_GROUP_CH = 16
S5_GROUPS = GROUP_W // S5_GROUP_CH
S5_STATE = 64

MLA_HEADS = 8
MLA_NOPE = 128
MLA_ROPE = 64
MLA_V = GROUP_W // MLA_HEADS
MLA_Q_RANK = 512
MLA_KV_RANK = 256

RW_HEAD = 64
RW_HEADS = GROUP_W // RW_HEAD
RW_W_RANK = 64
RW_A_RANK = 64
RW_G_RANK = 128
RW_GN_EPS = 64e-5
RW_IN = 3 * GROUP_W + RW_W_RANK + RW_A_RANK + RW_G_RANK

DIFF_HEAD = 64
DIFF_HEADS = GROUP_W // (2 * DIFF_HEAD)

MOE_GROUPS = 4
MOE_PER_GROUP = 8
MOE_EXPERTS = MOE_GROUPS * MOE_PER_GROUP
MOE_TOP_K = 2
MOE_BLOCK = 128

ZA_CQ = GROUP_W
ZA_CKV = ZA_CQ + MLA_Q_RANK
ZA_KR = ZA_CKV + MLA_KV_RANK
ZA_SPLIT = ZA_KR + MLA_ROPE
ZA_WIDTH = ZA_KR + 128
ZB_DQ = RW_IN
ZB_DK = ZB_DQ + GROUP_W
ZB_DV = ZB_DK + GROUP_W
ZB_WIDTH = ZB_DV + GROUP_W

VMEM_LIMIT = 56 * 1024 * 1024


def _cparams(sem):
    return pltpu.CompilerParams(dimension_semantics=sem, vmem_limit_bytes=VMEM_LIMIT)


def _pick(n, prefs):
    for p in prefs:
        if n % p == 0:
            return p
    return n


def _mm_kernel(a_ref, b_ref, o_ref):
    a = a_ref[...].astype(BF16)
    b = b_ref[...].astype(BF16)
    o_ref[...] = jnp.dot(a, b, preferred_element_type=F32).astype(o_ref.dtype)


def matmul(a, b, out_dtype=F32, tm=None, tn=None, layer=None, n_cols=None):
    m, k = a.shape
    n = n_cols or b.shape[-1]
    tm = tm or _pick(m, (512, 256, 128))
    tn = tn or _pick(n, (1024, 768, 512, 256, 128))
    if layer is None:
        b_spec = pl.BlockSpec((k, tn), lambda i, j: (0, j))
    else:
        b_spec = pl.BlockSpec((None, k, tn), lambda i, j: (layer, 0, j))
    return pl.pallas_call(
        _mm_kernel,
        grid=(m // tm, n // tn),
        in_specs=[pl.BlockSpec((tm, k), lambda i, j: (i, 0)), b_spec],
        out_specs=pl.BlockSpec((tm, tn), lambda i, j: (i, j)),
        out_shape=jax.ShapeDtypeStruct((m, n), out_dtype),
        compiler_params=_cparams(("parallel", "parallel")),
        name="matmul",
    )(a, b)


def mod_matmul(act, mod_w, tn=1024):
    m, k = act.shape
    depth, _, n = mod_w.shape
    return pl.pallas_call(
        _mm_kernel,
        grid=(depth, n // tn),
        in_specs=[pl.BlockSpec((m, k), lambda l, j: (0, 0)),
                  pl.BlockSpec((None, k, tn), lambda l, j: (l, 0, j))],
        out_specs=pl.BlockSpec((None, m, tn), lambda l, j: (l, 0, j)),
        out_shape=jax.ShapeDtypeStruct((depth, m, n), F32),
        compiler_params=_cparams(("parallel", "parallel")),
        name="mod_matmul",
    )(act, mod_w)


def _normmod_kernel(x_ref, g_ref, sh_ref, sc_ref, o_ref):
    x = x_ref[0]
    ms = jnp.mean(x * x, axis=-1, keepdims=True)
    y = x * lax.rsqrt(ms + NORM_EPS) * g_ref[...]
    o_ref[0] = (y * (1.0 + sc_ref[0]) + sh_ref[0]).astype(o_ref.dtype)


def normmod(xs, g, shift, scale, lc, out_dtype=BF16):
    b, n, d = xs.shape
    tb = lc
    sh = shift.reshape(b * 2, 1, d)
    sc = scale.reshape(b * 2, 1, d)
    seg = lambda bi, i: (bi * 2 + jnp.minimum(i, 1), 0, 0)
    return pl.pallas_call(
        _normmod_kernel,
        grid=(b, n // tb),
        in_specs=[pl.BlockSpec((1, tb, d), lambda bi, i: (bi, i, 0)),
                  pl.BlockSpec((1, d), lambda bi, i: (0, 0)),
                  pl.BlockSpec((1, 1, d), seg),
                  pl.BlockSpec((1, 1, d), seg)],
        out_specs=pl.BlockSpec((1, tb, d), lambda bi, i: (bi, i, 0)),
        out_shape=jax.ShapeDtypeStruct((b, n, d), out_dtype),
        compiler_params=_cparams(("parallel", "parallel")),
        name="normmod",
    )(xs, g.reshape(1, d), sh, sc)


def _out_proj_kernel(ya_ref, yb_ref, yc_ref, yd_ref, w_ref, x_ref, g_ref, o_ref, *, lc, blocks_per_seq):
    acc = jnp.dot(ya_ref[...], w_ref[0:GROUP_W], preferred_element_type=F32)
    acc += jnp.dot(yb_ref[...], w_ref[GROUP_W:2 * GROUP_W], preferred_element_type=F32)
    acc += jnp.dot(yc_ref[...], w_ref[2 * GROUP_W:3 * GROUP_W], preferred_element_type=F32)
    acc += jnp.dot(yd_ref[...], w_ref[3 * GROUP_W:4 * GROUP_W], preferred_element_type=F32)
    tm = acc.shape[0]
    tok = lax.broadcasted_iota(jnp.int32, (tm, 1), 0) + (pl.program_id(0) % blocks_per_seq) * tm
    gate = jnp.where(tok < lc, g_ref[0:1], g_ref[1:2])
    o_ref[...] = x_ref[...] + gate * acc


def out_proj(groups, w, xs, gate, lc):
    b, n, d = xs.shape
    tm = _pick(n, (768, 384, 256, 128))
    tn = 1024
    bps = n // tm
    row = pl.BlockSpec((tm, GROUP_W), lambda i, j: (i, 0))
    kern = functools.partial(_out_proj_kernel, lc=lc, blocks_per_seq=bps)
    out = pl.pallas_call(
        kern,
        grid=(b * bps, d // tn),
        in_specs=[row, row, row, row,
                  pl.BlockSpec((4 * GROUP_W, tn), lambda i, j: (0, j)),
                  pl.BlockSpec((tm, tn), lambda i, j: (i, j)),
                  pl.BlockSpec((None, 2, tn), lambda i, j: (i // bps, 0, j))],
        out_specs=pl.BlockSpec((tm, tn), lambda i, j: (i, j)),
        out_shape=jax.ShapeDtypeStruct((b * n, d), F32),
        compiler_params=_cparams(("parallel", "parallel")),
        name="out_proj",
    )(*groups, w, xs.reshape(b * n, d), gate)
    return out.reshape(b, n, d)


HEAD_TILE = 128


def _softmax_pv(q, k, v):
    s = lax.dot_general(q, k, (((1,), (1,)), ((), ())), preferred_element_type=F32)
    p = jnp.exp(s - jnp.max(s, axis=-1, keepdims=True))
    o = jnp.dot(p.astype(BF16), v, preferred_element_type=F32)
    return o, jnp.sum(p, axis=-1, keepdims=True)


def _per_segment(attend, lc, n):
    i = pl.program_id(2)

    @pl.when(i == 0)
    def _():
        attend(lc)

    @pl.when(i > 0)
    def _():
        attend(n)


ATT_HEADS = 4
ATT_W = ATT_HEADS * HEAD_TILE


def _head(x, j, width=HEAD_TILE, stride=HEAD_TILE):
    return x[:, j * stride:j * stride + width]


def _mla_attn_kernel(qn_ref, qr_ref, kv_ref, kr_ref, o_ref, *, lc):
    def attend(nk):
        kv, kr = kv_ref[0, :nk], kr_ref[0, :nk]
        for j in range(ATT_HEADS):
            q = jnp.concatenate([_head(qn_ref[0], j), _head(qr_ref[0], j)], axis=1)
            k = jnp.concatenate([_head(kv, 2 * j), kr], axis=1)
            o, l = _softmax_pv(q, k, _head(kv, 2 * j + 1))
            o_ref[0, :, j * HEAD_TILE:(j + 1) * HEAD_TILE] = (o / l).astype(o_ref.dtype)

    _per_segment(attend, lc, kv_ref.shape[1])


def mla_attention(q, kv, kr, lc):
    b, n, w = q.shape
    groups = w // (2 * ATT_W)
    tq = lc
    kern = functools.partial(_mla_attn_kernel, lc=lc)
    qspec = lambda off: pl.BlockSpec((1, tq, ATT_W), lambda bi, gi, i: (bi, i, off + gi))
    return pl.pallas_call(
        kern,
        grid=(b, groups, n // tq),
        in_specs=[qspec(0), qspec(groups),
                  pl.BlockSpec((1, n, 2 * ATT_W), lambda bi, gi, i: (bi, 0, gi)),
                  pl.BlockSpec((1, n, HEAD_TILE), lambda bi, gi, i: (bi, 0, 0))],
        out_specs=qspec(0),
        out_shape=jax.ShapeDtypeStruct((b, n, w // 2), BF16),
        compiler_params=_cparams(("parallel", "parallel", "arbitrary")),
        name="mla_attention",
    )(q, q, kv, kr)


def _diff_attn_kernel(q_ref, k_ref, v_ref, lam_ref, g_ref, o_ref, *, lc, post):
    lam = lam_ref[...]
    first = lax.broadcasted_iota(jnp.int32, (q_ref.shape[1], HEAD_TILE), 1) < DIFF_HEAD

    def attend(nk):
        for j in range(ATT_HEADS):
            q = _head(q_ref[0], j)
            k, v = _head(k_ref[0, :nk], j), _head(v_ref[0, :nk], j)
            o0, l0 = _softmax_pv(jnp.where(first, q, jnp.zeros_like(q)), k, v)
            o1, l1 = _softmax_pv(jnp.where(first, jnp.zeros_like(q), q), k, v)
            o = o0 / l0 - o1 * (lam[:, :1] / l1)
            ms = jnp.mean(o * o, axis=-1, keepdims=True)
            o_ref[0, :, j * HEAD_TILE:(j + 1) * HEAD_TILE] = (
                o * lax.rsqrt(ms + NORM_EPS) * g_ref[...] * post).astype(o_ref.dtype)

    _per_segment(attend, lc, v_ref.shape[1])


def diff_attention(q, k, v, lam, sub_norm, lc, post):
    b, n, w = q.shape
    tq = lc
    kern = functools.partial(_diff_attn_kernel, lc=lc, post=post)
    qspec = pl.BlockSpec((1, tq, ATT_W), lambda bi, gi, i: (bi, i, gi))
    kspec = pl.BlockSpec((1, n, ATT_W), lambda bi, gi, i: (bi, 0, gi))
    vec = pl.BlockSpec((1, HEAD_TILE), lambda bi, gi, i: (0, 0))
    return pl.pallas_call(
        kern,
        grid=(b, w // ATT_W, n // tq),
        in_specs=[qspec, kspec, kspec, vec, vec],
        out_specs=qspec,
        out_shape=jax.ShapeDtypeStruct((b, n, w), BF16),
        compiler_params=_cparams(("parallel", "parallel", "arbitrary")),
        name="diff_attention",
    )(q, k, v, jnp.broadcast_to(lam.reshape(1, 1), (1, HEAD_TILE)).astype(F32),
      sub_norm.reshape(1, HEAD_TILE))


S5_TB = 64
S5_GPT = 128 // S5_GROUP_CH
S5_TILES_IN = GROUP_W // 128
S5_TILES_ST = S5_GROUPS * S5_STATE // 128
S5_ST_PER_IN = S5_TILES_ST // S5_TILES_IN


def _s5_kernel(uf_ref, ub_ref, wd_ref, wr_ref, tab_ref, yf_ref, yb_ref, bu_re, bu_im, st_re, st_im):
    nb, tb, _ = uf_ref.shape
    rows = nb * tb
    half = S5_ST_PER_IN * 128

    @pl.when(pl.program_id(0) == 0)
    def _():
        st_re[...] = jnp.zeros_like(st_re)
        st_im[...] = jnp.zeros_like(st_im)

    for d, u_ref in enumerate((uf_ref, ub_ref)):
        x = u_ref[...].reshape(rows, GROUP_W).astype(BF16)
        for jt in range(S5_TILES_IN):
            bu = jnp.dot(x[:, jt * 128:(jt + 1) * 128], wd_ref[d, jt], preferred_element_type=F32)
            for c in range(S5_ST_PER_IN):
                bu_re[jt * S5_ST_PER_IN + c, d * rows:(d + 1) * rows, :] = bu[:, c * 128:(c + 1) * 128]
                bu_im[jt * S5_ST_PER_IN + c, d * rows:(d + 1) * rows, :] = bu[:, half + c * 128:half + (c + 1) * 128]

    def lane_tile(c, carry):
        for d in range(2):
            levels = [(tab_ref[d, 2 * k, c], tab_ref[d, 2 * k + 1, c], (1 << k) if d == 0 else 8 - (1 << k))
                      for k in range(3)]
            cr, ci = tab_ref[d, 6, c], tab_ref[d, 7, c]
            tiles = range(tb // 8) if d == 0 else range(tb // 8 - 1, -1, -1)
            for bi in range(nb):
                row = d * nb + bi
                car_r = st_re[c, row:row + 1, :]
                car_i = st_im[c, row:row + 1, :]
                for tt in tiles:
                    off = row * tb + tt * 8
                    hr = bu_re[c, off:off + 8, :]
                    hi = bu_im[c, off:off + 8, :]
                    for ar, ai, shift in levels:
                        pr = pltpu.roll(hr, shift, axis=0)
                        pi = pltpu.roll(hi, shift, axis=0)
                        hr, hi = hr + (ar * pr - ai * pi), hi + (ar * pi + ai * pr)
                    hr, hi = hr + (cr * car_r - ci * car_i), hi + (cr * car_i + ci * car_r)
                    bu_re[c, off:off + 8, :] = hr
                    bu_im[c, off:off + 8, :] = hi
                    last = 7 if d == 0 else 0
                    car_r, car_i = hr[last:last + 1], hi[last:last + 1]
                st_re[c, row:row + 1, :] = car_r
                st_im[c, row:row + 1, :] = car_i
        return carry

    lax.fori_loop(0, S5_TILES_ST, lane_tile, 0)

    for d, y_ref in enumerate((yf_ref, yb_ref)):
        for jt in range(S5_TILES_IN):
            tiles = [bu_re[jt * S5_ST_PER_IN + c, d * rows:(d + 1) * rows, :] for c in range(S5_ST_PER_IN)]
            tiles += [bu_im[jt * S5_ST_PER_IN + c, d * rows:(d + 1) * rows, :] for c in range(S5_ST_PER_IN)]
            h = jnp.concatenate(tiles, axis=1).astype(BF16)
            y = jnp.dot(h, wr_ref[d, jt], preferred_element_type=F32)
            y_ref[:, :, jt * 128:(jt + 1) * 128] = y.reshape(nb, tb, 128)


def _rev_block(i, nlc, ntot):
    return jnp.where(i < nlc, nlc - 1 - i, ntot - 1 - (i - nlc))


def s5_scan(z, lc, wd, wr, tab):
    b, n, _ = z.shape
    tb = min(S5_TB, lc)
    nlc, ntot = lc // tb, n // tb
    blk = (b, tb, GROUP_W)
    full = lambda a: pl.BlockSpec(a.shape, lambda i: (0,) * a.ndim)
    return pl.pallas_call(
        _s5_kernel,
        grid=(ntot,),
        in_specs=[pl.BlockSpec(blk, lambda i: (0, i, 0)),
                  pl.BlockSpec(blk, lambda i: (0, _rev_block(i, nlc, ntot), 0)),
                  full(wd), full(wr), full(tab)],
        out_specs=[pl.BlockSpec(blk, lambda i: (0, i, 0)),
                   pl.BlockSpec(blk, lambda i: (0, _rev_block(i, nlc, ntot), 0))],
        out_shape=[jax.ShapeDtypeStruct((b, n, GROUP_W), F32)] * 2,
        scratch_shapes=[pltpu.VMEM((S5_TILES_ST, 2 * b * tb, 128), F32)] * 2
        + [pltpu.VMEM((S5_TILES_ST, 8, 128), F32)] * 2,
        compiler_params=_cparams(("arbitrary",)),
        name="s5_scan",
    )(z, z, wd, wr, tab)


def _cmul(ar, ai, br, bi):
    return ar * br - ai * bi, ar * bi + ai * br


def s5_operators(a_re, a_im, log_step, b_re, b_im, c_re, c_im, nb):
    step = jnp.exp(log_step)[..., None]
    mag = jnp.exp(a_re * step)
    lr, li = mag * jnp.cos(a_im * step), mag * jnp.sin(a_im * step)
    den = a_re * a_re + a_im * a_im
    nr, ni = lr - 1.0, li
    fr = (nr * a_re + ni * a_im) / den
    fi = (ni * a_re - nr * a_im) / den
    bbr, bbi = _cmul(fr[..., None], fi[..., None], b_re, b_im)
    eye = jnp.eye(S5_GPT, dtype=F32)

    def drive(w):
        w = w.reshape(2, S5_TILES_IN, S5_GPT, S5_STATE, S5_GROUP_CH)
        w = jnp.einsum('dtgph,gk->dtghkp', w, eye)
        return w.reshape(2, S5_TILES_IN, 128, S5_GPT * S5_STATE)

    def readout(w):
        w = w.reshape(2, S5_TILES_IN, S5_GPT, S5_GROUP_CH, S5_STATE)
        w = jnp.einsum('dtghp,gk->dtgpkh', w, eye)
        return w.reshape(2, S5_TILES_IN, S5_GPT * S5_STATE, 128)

    wd = jnp.concatenate([drive(bbr), drive(bbi)], axis=-1).astype(BF16)
    wr = jnp.concatenate([readout(c_re), readout(-c_im)], axis=-2).astype(BF16)

    log_mag = (a_re * step).reshape(2, S5_TILES_ST, 1, 128)
    ang = (a_im * step).reshape(2, S5_TILES_ST, 1, 128)
    r8 = jnp.arange(8, dtype=F32).reshape(1, 1, 8, 1)

    def power(e, keep):
        e = jnp.broadcast_to(jnp.asarray(e, F32), (2, S5_TILES_ST, 8, 128))
        m = jnp.exp(log_mag * e)
        k = jnp.broadcast_to(keep, e.shape)
        return jnp.where(k, m * jnp.cos(ang * e), 0.0), jnp.where(k, m * jnp.sin(ang * e), 0.0)

    is_fwd = jnp.arange(2).reshape(2, 1, 1, 1) == 0
    tabs = []
    for s in (1, 2, 4):
        tabs += power(float(s), jnp.where(is_fwd, r8 >= s, r8 < 8 - s))
    tabs += power(jnp.where(is_fwd, r8 + 1.0, 8.0 - r8), True)
    return wd, wr, jnp.stack(tabs, axis=1)


def _s5_post_kernel(yf_ref, yb_ref, u_ref, d_ref, w_ref, o_ref):
    y = jax.nn.gelu(yf_ref[...] + yb_ref[...] + d_ref[...] * u_ref[...])
    gv = jnp.dot(y.astype(BF16), w_ref[...], preferred_element_type=F32)
    o_ref[...] = (gv[:, :GROUP_W] * jax.nn.sigmoid(gv[:, GROUP_W:])).astype(o_ref.dtype)


def s5_post(yf, yb, z, d_skip, w_glu):
    t = yf.shape[0]
    tm = _pick(t, (512, 256, 128))
    row = pl.BlockSpec((tm, GROUP_W), lambda i: (i, 0))
    return pl.pallas_call(
        _s5_post_kernel,
        grid=(t // tm,),
        in_specs=[row, row, row, pl.BlockSpec((1, GROUP_W), lambda i: (0, 0)),
                  pl.BlockSpec((GROUP_W, 2 * GROUP_W), lambda i: (0, 0))],
        out_specs=row,
        out_shape=jax.ShapeDtypeStruct((t, GROUP_W), BF16),
        compiler_params=_cparams(("parallel",)),
        name="s5_post",
    )(yf, yb, z, d_skip.reshape(1, GROUP_W), w_glu)


RW_TB = 16
RW_ROWS = 8
RW_TILES = GROUP_W // 128


def _split_bf16(x, parts):
    out = []
    for _ in range(parts - 1):
        hi = x.astype(BF16)
        out.append(hi)
        x = x - hi.astype(F32)
    out.append(x.astype(BF16))
    return out


def _permute_rows(p, x, parts):
    return sum(jnp.dot(p, piece, preferred_element_type=F32) for piece in _split_bf16(x, parts))


def _rwkv_to_lanes(fwd_ref, bwd_ref, p_ref, xt_s, dst_ref, parts):
    nb, tb, _ = fwd_ref.shape
    x = jnp.concatenate([fwd_ref[...].reshape(nb * tb, GROUP_W), bwd_ref[...].reshape(nb * tb, GROUP_W)], axis=0)
    xt_s[...] = _permute_rows(p_ref[...], x, parts)
    low = lax.broadcasted_iota(jnp.int32, (RW_HEAD, 128), 1) < RW_HEAD

    def pair(tp, carry):
        z = xt_s[pl.ds(pl.multiple_of(tp * 2 * RW_ROWS, 2 * RW_ROWS), 2 * RW_ROWS), :]
        zz = jnp.concatenate([z[p * RW_ROWS:(p + 1) * RW_ROWS, jt * 128:(jt + 1) * 128]
                              for p in range(2) for jt in range(RW_TILES)], axis=0)
        tt = zz.T
        top, bot = tt[:RW_HEAD], tt[RW_HEAD:]
        dst_ref[2 * tp] = jnp.where(low, top, pltpu.roll(bot, RW_HEAD, axis=1))
        dst_ref[2 * tp + 1] = jnp.where(low, pltpu.roll(top, RW_HEAD, axis=1), bot)
        return carry

    lax.fori_loop(0, tb // 2, pair, 0, unroll=8)


def _rwkv_from_lanes(y_s, pt_ref, xt_s, of_ref, ob_ref):
    nb, tb, _ = of_ref.shape
    low = lax.broadcasted_iota(jnp.int32, (RW_HEAD, 128), 1) < RW_HEAD

    def pair(tp, carry):
        y0, y1 = y_s[2 * tp], y_s[2 * tp + 1]
        top = jnp.where(low, y0, pltpu.roll(y1, RW_HEAD, axis=1))
        bot = jnp.where(low, pltpu.roll(y0, RW_HEAD, axis=1), y1)
        zz = jnp.concatenate([top, bot], axis=0).T
        base = pl.multiple_of(tp * 2 * RW_ROWS, 2 * RW_ROWS)
        for p in range(2):
            for jt in range(RW_TILES):
                blk = zz[(p * RW_TILES + jt) * RW_ROWS:(p * RW_TILES + jt + 1) * RW_ROWS]
                xt_s[pl.ds(base + p * RW_ROWS, RW_ROWS), jt * 128:(jt + 1) * 128] = blk
        return carry

    lax.fori_loop(0, tb // 2, pair, 0, unroll=8)
    y = _permute_rows(pt_ref[...], xt_s[...], 2)
    of_ref[...] = y[:nb * tb].reshape(nb, tb, GROUP_W)
    ob_ref[...] = y[nb * tb:].reshape(nb, tb, GROUP_W)


def _rwkv_kernel(rf_ref, kf_ref, vf_ref, wf_ref, af_ref, rb_ref, kb_ref, vb_ref, wb_ref, ab_ref,
                 p_ref, pt_ref, c_ref, of_ref, ob_ref,
                 r_s, k_s, v_s, w_s, a_s, y_s, xt_s, kk_s, bb_s, s_ref):
    tb = rf_ref.shape[1]

    @pl.when(pl.program_id(0) == 0)
    def _():
        s_ref[...] = jnp.zeros_like(s_ref)

    _rwkv_to_lanes(rf_ref, rb_ref, p_ref, xt_s, r_s, 2)
    _rwkv_to_lanes(kf_ref, kb_ref, p_ref, xt_s, k_s, 2)
    _rwkv_to_lanes(vf_ref, vb_ref, p_ref, xt_s, v_s, 2)
    _rwkv_to_lanes(wf_ref, wb_ref, p_ref, xt_s, w_s, 3)
    _rwkv_to_lanes(af_ref, ab_ref, p_ref, xt_s, a_s, 2)

    def step(t, carry):
        k = k_s[t]
        kk = k * c_ref[2]
        kk = kk * lax.rsqrt(jnp.maximum(jnp.sum(kk * kk, axis=0, keepdims=True), 1e-12))
        z = -(w_s[t] + c_ref[0])
        w = -(jnp.maximum(z, 0.0) + jnp.log(1.0 + jnp.exp(-jnp.abs(z)))) - 0.5
        a = jax.nn.sigmoid(a_s[t] + c_ref[1])
        kd = k * (1.0 + (a - 1.0) * c_ref[3])
        bonus = jnp.sum(r_s[t] * kd * c_ref[4], axis=0, keepdims=True)
        w_s[t] = jnp.exp(-jnp.exp(w))
        k_s[t] = kd
        kk_s[...] = kk
        bb_s[...] = kk * a

        def vgroup(vg, carry2):
            off = pl.multiple_of(vg * 8, 8)
            vrows = v_s[t, pl.ds(off, 8), :]
            outs = []
            for j in range(8):
                sv = s_ref[off + j]
                sa = jnp.sum(sv * kk_s[...], axis=0, keepdims=True)
                sn = sv * w_s[t] - sa * bb_s[...] + vrows[j:j + 1] * k_s[t]
                s_ref[off + j] = sn
                outs.append(jnp.sum(sn * r_s[t], axis=0, keepdims=True))
            y_s[t, pl.ds(off, 8), :] = jnp.concatenate(outs, axis=0) + bonus * vrows
            return carry2

        return lax.fori_loop(0, RW_HEAD // 8, vgroup, carry)

    lax.fori_loop(0, tb, step, 0)
    _rwkv_from_lanes(y_s, pt_ref, xt_s, of_ref, ob_ref)


def rwkv_scan(zs, lr, consts, lc):
    b, n, _ = zs.shape
    assert 2 * b == RW_ROWS
    tb = RW_TB
    nlc, ntot = lc // tb, n // tb
    rows = RW_ROWS * tb
    r_out = jnp.arange(rows)
    step, db = r_out // RW_ROWS, r_out % RW_ROWS
    src = db * tb + jnp.where(db < b, step, tb - 1 - step)
    perm = (src[:, None] == jnp.arange(rows)[None, :]).astype(BF16)
    blk = (b, tb, GROUP_W)
    fwd = lambda col: pl.BlockSpec(blk, lambda i: (0, i, col))
    bwd = lambda col: pl.BlockSpec(blk, lambda i: (0, _rev_block(i, nlc, ntot), col))
    full = lambda a: pl.BlockSpec(a.shape, lambda i: (0,) * a.ndim)
    tile = pltpu.VMEM((tb, RW_HEAD, 128), F32)
    return pl.pallas_call(
        _rwkv_kernel,
        grid=(ntot,),
        in_specs=[fwd(0), fwd(1), fwd(2), fwd(0), fwd(2), bwd(0), bwd(1), bwd(2), bwd(1), bwd(3),
                  full(perm), full(perm), full(consts)],
        out_specs=[fwd(0), bwd(0)],
        out_shape=[jax.ShapeDtypeStruct((b, n, GROUP_W), F32)] * 2,
        scratch_shapes=[tile] * 6 + [pltpu.VMEM((rows, GROUP_W), F32),
                                     pltpu.VMEM((RW_HEAD, 128), F32), pltpu.VMEM((RW_HEAD, 128), F32),
                                     pltpu.VMEM((RW_HEAD, RW_HEAD, 128), F32)],
        compiler_params=_cparams(("arbitrary",)),
        name="rwkv_scan",
    )(zs, zs, zs, lr, lr, zs, zs, zs, lr, lr, perm, perm.T, consts)


def _moe_row_copy(tok_ref, x_hbm, xbuf, sem, blk, slot, r):
    tok = tok_ref[blk * MOE_BLOCK + r]
    return pltpu.make_async_copy(x_hbm.at[pl.ds(tok, 1)], xbuf.at[slot, pl.ds(r, 1)], sem.at[slot])


def _moe_kernel(be_ref, nu_ref, tok_ref, x_hbm, wg_ref, wu_ref, wd_ref, o_ref, xbuf, sem, g_s, u_s, d_s):
    i = pl.program_id(0)
    used = i < nu_ref[0]
    changed = jnp.logical_or(i == 0, be_ref[i] != be_ref[jnp.maximum(i - 1, 0)])
    slot = i % 2

    def gather(blk, into):
        def start(r, carry):
            _moe_row_copy(tok_ref, x_hbm, xbuf, sem, blk, into, r).start()
            return carry
        lax.fori_loop(0, MOE_BLOCK, start, 0)

    @pl.when(i == 0)
    def _():
        gather(0, 0)

    @pl.when(i + 1 < nu_ref[0])
    def _():
        gather(i + 1, 1 - slot)

    @pl.when(jnp.logical_and(used, changed))
    def _():
        g_s[...] = wg_ref[...].astype(BF16)
        u_s[...] = wu_ref[...].astype(BF16)
        d_s[...] = wd_ref[...].astype(BF16)

    @pl.when(used)
    def _():
        def wait(r, carry):
            _moe_row_copy(tok_ref, x_hbm, xbuf, sem, i, slot, r).wait()
            return carry
        lax.fori_loop(0, MOE_BLOCK, wait, 0)
        x = xbuf[slot].astype(BF16)
        g = jnp.dot(x, g_s[...], preferred_element_type=F32)
        u = jnp.dot(x, u_s[...], preferred_element_type=F32)
        h = (g * jax.nn.sigmoid(g) * u).astype(BF16)
        o_ref[...] = jnp.dot(h, d_s[...], preferred_element_type=F32)

    @pl.when(jnp.logical_not(used))
    def _():
        o_ref[...] = jnp.zeros_like(o_ref)


def moe_ffn(block_exp, n_used, slot_tok, x, wg, wu, wd, layer):
    cap = slot_tok.shape[0]
    d = x.shape[1]
    hid = wg.shape[-1]
    nb = cap // MOE_BLOCK
    once = pl.Buffered(1)
    wspec = lambda shape: pl.BlockSpec((None, None) + shape, lambda i, be, nu, tok: (layer, be[i], 0, 0),
                                       pipeline_mode=once)
    grid_spec = pltpu.PrefetchScalarGridSpec(
        num_scalar_prefetch=3,
        grid=(nb,),
        in_specs=[pl.BlockSpec(memory_space=pl.ANY), wspec((d, hid)), wspec((d, hid)), wspec((hid, d))],
        out_specs=pl.BlockSpec((MOE_BLOCK, d), lambda i, be, nu, tok: (i, 0)),
        scratch_shapes=[pltpu.VMEM((2, MOE_BLOCK, d), F32), pltpu.SemaphoreType.DMA((2,)),
                        pltpu.VMEM((d, hid), BF16), pltpu.VMEM((d, hid), BF16), pltpu.VMEM((hid, d), BF16)],
    )
    return pl.pallas_call(
        _moe_kernel,
        grid_spec=grid_spec,
        out_shape=jax.ShapeDtypeStruct((cap, d), F32),
        compiler_params=_cparams(("arbitrary",)),
        name="moe_ffn",
    )(block_exp, n_used, slot_tok, x, wg, wu, wd)


def _rms(x, g):
    return x * lax.rsqrt(jnp.mean(x * x, axis=-1, keepdims=True) + NORM_EPS) * g


def _axial_rope(rows, dim):
    n_freq = dim // 4
    inv = ROPE_BASE ** (-jnp.arange(n_freq, dtype=F32) / n_freq)
    row = jnp.repeat(jnp.arange(rows, dtype=F32), GRID_W)
    col = jnp.tile(jnp.arange(GRID_W, dtype=F32), rows)
    ang = jnp.concatenate([row[:, None] * inv, col[:, None] * inv], axis=-1)
    return jnp.cos(ang), jnp.sin(ang)


def _rope(x, cos, sin):
    half = x.shape[-1] // 2
    x1, x2 = x[..., :half], x[..., half:]
    return jnp.concatenate([x1 * cos - x2 * sin, x1 * sin + x2 * cos], axis=-1)


def _rope_lat(x, cos, sin, lc):
    extra = (None,) * (x.ndim - 3)
    cs = cos[(slice(None),) + extra]
    sn = sin[(slice(None),) + extra]
    return jnp.concatenate([x[:, :lc], _rope(x[:, lc:], cs, sn)], axis=1)


def _seq_rev(t, lc):
    return jnp.concatenate([jnp.flip(t[:, :lc], 1), jnp.flip(t[:, lc:], 1)], axis=1)


def _mm3(x, w, **kw):
    b, n, k = x.shape
    return matmul(x.reshape(b * n, k), w, **kw).reshape(b, n, -1)


def _mla_mixer(cq, ckv, krope, lc, q_a_norm, kv_a_norm, w_uq, w_ukv, q_norm, k_norm, cos, sin):
    b, n, _ = cq.shape
    pad = HEAD_TILE - MLA_ROPE
    wq = w_uq.reshape(MLA_Q_RANK, MLA_HEADS, MLA_NOPE + MLA_ROPE)
    wq = jnp.concatenate([wq[:, :, :MLA_NOPE].reshape(MLA_Q_RANK, -1),
                          jnp.pad(wq[:, :, MLA_NOPE:], ((0, 0), (0, 0), (0, pad))).reshape(MLA_Q_RANK, -1)], axis=1)
    q = _mm3(_rms(cq, q_a_norm), wq).reshape(b, n, 2, MLA_HEADS, HEAD_TILE)
    kv = _mm3(_rms(ckv, kv_a_norm), w_ukv).reshape(b, n, MLA_HEADS, MLA_NOPE + MLA_V)
    scale = float(MLA_NOPE + MLA_ROPE) ** -0.5
    q_nope = _rms(q[:, :, 0], q_norm[:MLA_NOPE]) * scale
    q_rope = _rope_lat(_rms(q[:, :, 1, :, :MLA_ROPE], q_norm[MLA_NOPE:]), cos, sin, lc) * scale
    q_rope = jnp.pad(q_rope, ((0, 0), (0, 0), (0, 0), (0, pad)))
    k_nope = _rms(kv[..., :MLA_NOPE], k_norm[:MLA_NOPE])
    k_rope = _rope_lat(_rms(krope, k_norm[MLA_NOPE:]), cos, sin, lc)
    qf = jnp.stack([q_nope, q_rope], axis=2).reshape(b, n, 2 * MLA_HEADS * HEAD_TILE).astype(BF16)
    kvf = jnp.concatenate([k_nope, kv[..., MLA_NOPE:]], axis=-1).reshape(b, n, -1).astype(BF16)
    return mla_attention(qf, kvf, jnp.pad(k_rope, ((0, 0), (0, 0), (0, pad))).astype(BF16), lc)


def _diff_mixer(dq, dk, dv, lc, q_norm, k_norm, lam_vecs, sub_norm, lam_init, cos, sin):
    b, n, _ = dq.shape
    q = _rope_lat(_rms(dq.reshape(b, n, DIFF_HEADS, 2, DIFF_HEAD), q_norm), cos, sin, lc) * float(DIFF_HEAD) ** -0.5
    k = _rope_lat(_rms(dk.reshape(b, n, DIFF_HEADS, 2, DIFF_HEAD), k_norm), cos, sin, lc)
    lam = jnp.exp(jnp.sum(lam_vecs[0] * lam_vecs[1])) - jnp.exp(jnp.sum(lam_vecs[2] * lam_vecs[3])) + lam_init
    flat = lambda x: x.reshape(b, n, GROUP_W).astype(BF16)
    return diff_attention(flat(q), flat(k), dv.astype(BF16), lam, sub_norm, lc, 1.0 - lam_init)


SHIFT_HALO = 8


def _shift_kernel(z_ref, p_ref, n_ref, mu_ref, o_ref, *, blocks_lc, blocks):
    i = pl.program_id(1)
    z = z_ref[0]
    tb = z.shape[0]
    row = lax.broadcasted_iota(jnp.int32, (tb, 1), 0)
    starts = jnp.logical_or(i == 0, i == blocks_lc)
    ends = jnp.logical_or(i == blocks_lc - 1, i == blocks - 1)
    before = jnp.where(starts, 0.0, p_ref[0, SHIFT_HALO - 1:SHIFT_HALO, :])
    after = jnp.where(ends, 0.0, n_ref[0, 0:1, :])
    z_prev = jnp.where(row == 0, before, pltpu.roll(z, 1, axis=0))
    z_next = jnp.where(row == tb - 1, after, pltpu.roll(z, tb - 1, axis=0))
    o_ref[0] = z + mu_ref[0:1] * (z_prev - z) + mu_ref[1:2] * (z_next - z)


def centred_shift(z, mu, lc):
    b, n, _ = z.shape
    c = mu.shape[1]
    tb = lc
    per = tb // SHIFT_HALO
    kern = functools.partial(_shift_kernel, blocks_lc=lc // tb, blocks=n // tb)
    return pl.pallas_call(
        kern,
        grid=(b, n // tb),
        in_specs=[pl.BlockSpec((1, tb, c), lambda bi, i: (bi, i, 0)),
                  pl.BlockSpec((1, SHIFT_HALO, c), lambda bi, i: (bi, jnp.maximum(i * per - 1, 0), 0)),
                  pl.BlockSpec((1, SHIFT_HALO, c), lambda bi, i: (bi, jnp.minimum((i + 1) * per, n // SHIFT_HALO - 1), 0)),
                  pl.BlockSpec((2, c), lambda bi, i: (0, 0))],
        out_specs=pl.BlockSpec((1, tb, c), lambda bi, i: (bi, i, 0)),
        out_shape=jax.ShapeDtypeStruct((b, n, c), F32),
        compiler_params=_cparams(("parallel", "parallel")),
        name="centred_shift",
    )(z, z, z, mu)


def _seg_sum(x, j):
    return sum(jnp.dot(piece, j, preferred_element_type=F32) for piece in _split_bf16(x, 2))


def _rwkv_finish_kernel(of_ref, ob_ref, g_ref, lng_ref, lnb_ref, j_ref, o_ref):
    j = j_ref[...]
    for jt in range(RW_TILES):
        sl = slice(jt * 128, (jt + 1) * 128)
        x = of_ref[:, sl] + ob_ref[:, sl]
        xc = x - _seg_sum(x, j) * (1.0 / RW_HEAD)
        var = _seg_sum(xc * xc, j) * (1.0 / RW_HEAD)
        y = xc * lax.rsqrt(var + RW_GN_EPS)
        o_ref[:, sl] = ((y * lng_ref[:, sl] + lnb_ref[:, sl]) * g_ref[:, sl]).astype(o_ref.dtype)


def rwkv_finish(o_f, o_b, lr, ln_g, ln_b):
    t = o_f.shape[0]
    tm = _pick(t, (512, 256, 128))
    half = jnp.arange(128) // RW_HEAD
    ones = (half[:, None] == half[None, :]).astype(BF16)
    row = pl.BlockSpec((tm, GROUP_W), lambda i: (i, 0))
    vec = pl.BlockSpec((1, GROUP_W), lambda i: (0, 0))
    return pl.pallas_call(
        _rwkv_finish_kernel,
        grid=(t // tm,),
        in_specs=[row, row, pl.BlockSpec((tm, GROUP_W), lambda i: (i, 4)), vec, vec,
                  pl.BlockSpec((128, 128), lambda i: (0, 0))],
        out_specs=row,
        out_shape=jax.ShapeDtypeStruct((t, GROUP_W), BF16),
        compiler_params=_cparams(("parallel",)),
        name="rwkv_finish",
    )(o_f, o_b, lr, ln_g.reshape(1, GROUP_W), ln_b.reshape(1, GROUP_W), ones)


def _rwkv_mixer(z, lc, mu, w0, w2, a0, a2, g2, k_k, k_a, r_k, ln_g, ln_b):
    b, n, _ = z.shape
    z = centred_shift(z, mu, lc)
    o = 3 * GROUP_W
    wl, al, gl = z[..., o:o + 64], z[..., o + 64:o + 128], z[..., o + 128:o + 256]
    lr_in = jnp.concatenate([jnp.tanh(wl), al, jax.nn.sigmoid(gl)], axis=-1)
    zw = jnp.zeros((RW_W_RANK, GROUP_W), F32)
    zg = jnp.zeros((RW_G_RANK, GROUP_W), F32)
    w_lr = jnp.concatenate([
        jnp.concatenate([w2[0], w2[1], zw, zw, zw], axis=1),
        jnp.concatenate([zw, zw, a2[0], a2[1], zw], axis=1),
        jnp.concatenate([zg, zg, zg, zg, g2], axis=1)], axis=0)
    lr = _mm3(lr_in, w_lr)

    def chain_tile(p_fwd, p_bwd):
        per_dir = [p.reshape(RW_TILES, 2, RW_HEAD).transpose(2, 1, 0) for p in (p_fwd, p_bwd)]
        return jnp.stack([per_dir[0]] * b + [per_dir[1]] * b, axis=-1).reshape(RW_HEAD, 128)

    consts = jnp.stack([chain_tile(w0[0], w0[1]), chain_tile(a0[0], a0[1]), chain_tile(k_k, k_k),
                        chain_tile(k_a, k_a), chain_tile(r_k.reshape(-1), r_k.reshape(-1))])
    o_f, o_b = rwkv_scan(z, lr, consts, lc)
    flat = lambda t: t.reshape(b * n, t.shape[-1])
    return rwkv_finish(flat(o_f), flat(o_b), flat(lr), ln_g, ln_b)


def _hier_moe(h, wr, br_coarse, br_fine, wg, wu, wd, layer):
    t, d = h.shape
    logits = matmul(h, wr)
    p_group = jax.nn.softmax(logits[:, :MOE_GROUPS] + br_coarse, axis=-1)
    group = jnp.argmax(p_group, axis=-1).astype(jnp.int32)
    p_sel = jnp.take_along_axis(p_group, group[:, None], axis=-1)
    fine = (logits[:, MOE_GROUPS:MOE_GROUPS + MOE_EXPERTS] + br_fine).reshape(t, MOE_GROUPS, MOE_PER_GROUP)
    fine = jnp.take_along_axis(fine, group[:, None, None], axis=1)[:, 0]
    top_v, top_i = lax.top_k(fine, MOE_TOP_K)
    weight = p_sel * jax.nn.softmax(top_v, axis=-1)
    expert = group[:, None] * MOE_PER_GROUP + top_i.astype(jnp.int32)
    flat_e = expert.reshape(-1)
    n_assign = flat_e.shape[0]
    onehot = (flat_e[:, None] == jnp.arange(MOE_EXPERTS, dtype=jnp.int32)[None, :]).astype(jnp.int32)
    running = jnp.cumsum(onehot, axis=0)
    counts = running[-1]
    padded = (counts + MOE_BLOCK - 1) // MOE_BLOCK * MOE_BLOCK
    pad_end = jnp.cumsum(padded)
    pad_start = pad_end - padded
    dest = jnp.sum(onehot * (running - 1 + pad_start[None, :]), axis=1).astype(jnp.int32)
    n_blocks = -(-(n_assign + MOE_EXPERTS * (MOE_BLOCK - 1)) // MOE_BLOCK)
    cap = n_blocks * MOE_BLOCK
    slot_tok = jnp.zeros((cap,), jnp.int32).at[dest].set(jnp.arange(n_assign, dtype=jnp.int32) // MOE_TOP_K)
    block_start = jnp.arange(n_blocks, dtype=jnp.int32) * MOE_BLOCK
    block_exp = jnp.minimum(jnp.sum((pad_end[None, :] <= block_start[:, None]).astype(jnp.int32), axis=1),
                            MOE_EXPERTS - 1)
    n_used = (pad_end[-1:] // MOE_BLOCK).astype(jnp.int32)
    yb = moe_ffn(block_exp, n_used, slot_tok, h, wg, wu, wd, layer)
    slot_of = dest.reshape(t, MOE_TOP_K)
    return weight, yb[slot_of[:, 0]], yb[slot_of[:, 1]]


def _combine_kernel(x_ref, y0_ref, y1_ref, w_ref, g_ref, o_ref):
    w = w_ref[0]
    o_ref[0] = x_ref[0] + g_ref[0] * (w[:, 0:1] * y0_ref[0] + w[:, 1:2] * y1_ref[0])


def moe_combine(xs, weight, y0, y1, gate, lc, skip):
    b, ntok, d = xs.shape
    r = ntok - skip
    tm, tn = lc, 1024
    off = skip // tm
    blk = lambda idx: pl.BlockSpec((1, tm, tn), idx)
    rows = lambda bi, i, j: (bi, i, j)
    return pl.pallas_call(
        _combine_kernel,
        grid=(b, r // tm, d // tn),
        in_specs=[blk(lambda bi, i, j: (bi, off + i, j)), blk(rows), blk(rows),
                  pl.BlockSpec((1, tm, MOE_TOP_K), lambda bi, i, j: (bi, i, 0)),
                  pl.BlockSpec((1, 1, tn), lambda bi, i, j: (bi * 2 + jnp.minimum(off + i, 1), 0, j))],
        out_specs=blk(rows),
        out_shape=jax.ShapeDtypeStruct((b, r, d), F32),
        compiler_params=_cparams(("parallel", "parallel", "parallel")),
        name="moe_combine",
    )(xs, y0.reshape(b, r, d), y1.reshape(b, r, d), weight.reshape(b, r, MOE_TOP_K), gate.reshape(b * 2, 1, d))


def kernel(x, c, ctx, c_ctx, mod_w, mod_b, norm1_g, norm2_g, w_in, w_out, s5_a_re, s5_a_im, s5_log_step, s5_b_re, s5_b_im, s5_c_re, s5_c_im, s5_d, s5_w_glu, mla_q_a_norm, mla_kv_a_norm, mla_w_uq, mla_w_ukv, mla_q_norm, mla_k_norm, rw_mu, rw_w0, rw_w2, rw_a0, rw_a2, rw_g2, rw_k_k, rw_k_a, rw_r_k, rw_ln_g, rw_ln_b, diff_q_norm, diff_k_norm, diff_lambda, diff_sub_norm, moe_wr_coarse, moe_br_coarse, moe_wr_fine, moe_br_fine, moe_w_gate, moe_w_up, moe_w_down):
    b, n, d = x.shape
    lc = ctx.shape[1]
    ntok = lc + n
    depth = mod_w.shape[0]
    rows = n // GRID_W
    cos_m, sin_m = _axial_rope(rows, MLA_ROPE)
    cos_d, sin_d = _axial_rope(rows, DIFF_HEAD)
    act = jnp.concatenate([jax.nn.silu(c), jax.nn.silu(c_ctx)[None],
                           jnp.zeros((8 - b - 1, d), F32)], axis=0)
    mod_all = mod_matmul(act, mod_w) + mod_b[:, None, :]
    xs = jnp.concatenate([ctx, x], axis=1)

    for l in range(depth):
        last = l == depth - 1
        lam_init = 0.8 - 0.6 * math.exp(-0.3 * l)
        mod_lat = mod_all[l, :b].reshape(b, N_MOD, d)
        mod_ctx = jnp.broadcast_to(mod_all[l, b].reshape(1, N_MOD, d), (b, N_MOD, d))
        mods = jnp.stack([mod_ctx, mod_lat], axis=2)

        h = normmod(xs, norm1_g[l], mods[:, 0], mods[:, 1], lc).reshape(b * ntok, d)
        tm_in = _pick(b * ntok, (1024, 512, 256, 128))
        za = matmul(h, w_in[l][:, :ZA_WIDTH].astype(BF16), tn=ZA_WIDTH).reshape(b, ntok, ZA_WIDTH)
        zb = matmul(h, w_in[l][:, ZA_SPLIT:].astype(BF16), tm=tm_in,
                    tn=_pick(ZB_WIDTH, (1280, 640, 128))).reshape(b, ntok, ZB_WIDTH)

        wd, wr, lam = s5_operators(s5_a_re[l], s5_a_im[l], s5_log_step[l], s5_b_re[l], s5_b_im[l],
                                   s5_c_re[l], s5_c_im[l], b)
        yf, yr = s5_scan(za, lc, wd, wr, lam)
        ya = s5_post(yf.reshape(b * ntok, GROUP_W), yr.reshape(b * ntok, GROUP_W),
                     za.reshape(b * ntok, ZA_WIDTH), s5_d[l], s5_w_glu[l].astype(BF16))
        yb = _mla_mixer(za[..., ZA_CQ:ZA_CKV], za[..., ZA_CKV:ZA_KR], za[..., ZA_KR:ZA_SPLIT], lc,
                        mla_q_a_norm[l], mla_kv_a_norm[l], mla_w_uq[l], mla_w_ukv[l],
                        mla_q_norm[l], mla_k_norm[l], cos_m, sin_m)
        yc = _rwkv_mixer(zb, lc, rw_mu[l], rw_w0[l], rw_w2[l], rw_a0[l], rw_a2[l],
                         rw_g2[l], rw_k_k[l], rw_k_a[l], rw_r_k[l], rw_ln_g[l], rw_ln_b[l])
        yd = _diff_mixer(zb[..., ZB_DQ:ZB_DK], zb[..., ZB_DK:ZB_DV], zb[..., ZB_DV:], lc,
                         diff_q_norm[l], diff_k_norm[l], diff_lambda[l], diff_sub_norm[l], lam_init,
                         cos_d, sin_d)
        flat = lambda y: y.reshape(b * ntok, GROUP_W)
        xs = out_proj((ya, flat(yb), flat(yc), flat(yd)), w_out[l].astype(BF16), xs, mods[:, 2], lc)

        h2 = normmod(xs, norm2_g[l], mods[:, 3], mods[:, 4], lc, out_dtype=F32)
        wrt = jnp.concatenate([moe_wr_coarse[l], moe_wr_fine[l],
                               jnp.zeros((d, 128 - MOE_GROUPS - MOE_EXPERTS), F32)], axis=1)
        experts = (moe_w_gate, moe_w_up, moe_w_down, l)
        skip = lc if last else 0
        routed = _hier_moe(h2[:, skip:].reshape(b * (ntok - skip), d), wrt, moe_br_coarse[l], moe_br_fine[l],
                           *experts)
        xs = moe_combine(xs, *routed, mods[:, 5], lc, skip)
    return xs
```

```python
import functools
import math

import jax
import jax.numpy as jnp
from jax import lax
from jax.experimental import pallas as pl
from jax.experimental.pallas import tpu as pltpu

F32 = jnp.float32
BF16 = jnp.bfloat16

D_MODEL = 4096
N_MOD = 6
GROUP_W = D_MODEL // 4
GRID_W = 64
ROPE_BASE = 10000.0
NORM_EPS = 1e-6

S5_GROUP_CH = 16
S5_GROUPS = GROUP_W // S5_GROUP_CH
S5_STATE = 64

MLA_HEADS = 8
MLA_NOPE = 128
MLA_ROPE = 64
MLA_V = GROUP_W // MLA_HEADS
MLA_Q_RANK = 512
MLA_KV_RANK = 256

RW_HEAD = 64
RW_HEADS = GROUP_W // RW_HEAD
RW_W_RANK = 64
RW_A_RANK = 64
RW_G_RANK = 128
RW_GN_EPS = 64e-5
RW_IN = 3 * GROUP_W + RW_W_RANK + RW_A_RANK + RW_G_RANK

DIFF_HEAD = 64
DIFF_HEADS = GROUP_W // (2 * DIFF_HEAD)

MOE_GROUPS = 4
MOE_PER_GROUP = 8
MOE_EXPERTS = MOE_GROUPS * MOE_PER_GROUP
MOE_TOP_K = 2
MOE_BLOCK = 128

ZA_CQ = GROUP_W
ZA_CKV = ZA_CQ + MLA_Q_RANK
ZA_KR = ZA_CKV + MLA_KV_RANK
ZA_SPLIT = ZA_KR + MLA_ROPE
ZA_WIDTH = ZA_KR + 128
ZB_DQ = RW_IN
ZB_DK = ZB_DQ + GROUP_W
ZB_DV = ZB_DK + GROUP_W
ZB_WIDTH = ZB_DV + GROUP_W

VMEM_LIMIT = 56 * 1024 * 1024


def _cparams(sem):
    return pltpu.CompilerParams(dimension_semantics=sem, vmem_limit_bytes=VMEM_LIMIT)


def _pick(n, prefs):
    for p in prefs:
        if n % p == 0:
            return p
    return n


def _mm_kernel(a_ref, b_ref, o_ref):
    a = a_ref[...].astype(BF16)
    b = b_ref[...].astype(BF16)
    o_ref[...] = jnp.dot(a, b, preferred_element_type=F32).astype(o_ref.dtype)


def matmul(a, b, out_dtype=F32, tm=None, tn=None, layer=None, n_cols=None):
    m, k = a.shape
    n = n_cols or b.shape[-1]
    tm = tm or _pick(m, (512, 256, 128))
    tn = tn or _pick(n, (1024, 768, 512, 256, 128))
    if layer is None:
        b_spec = pl.BlockSpec((k, tn), lambda i, j: (0, j))
    else:
        b_spec = pl.BlockSpec((None, k, tn), lambda i, j: (layer, 0, j))
    return pl.pallas_call(
        _mm_kernel,
        grid=(m // tm, n // tn),
        in_specs=[pl.BlockSpec((tm, k), lambda i, j: (i, 0)), b_spec],
        out_specs=pl.BlockSpec((tm, tn), lambda i, j: (i, j)),
        out_shape=jax.ShapeDtypeStruct((m, n), out_dtype),
        compiler_params=_cparams(("parallel", "parallel")),
        name="matmul",
    )(a, b)


def mod_matmul(act, mod_w, tn=1024):
    m, k = act.shape
    depth, _, n = mod_w.shape
    return pl.pallas_call(
        _mm_kernel,
        grid=(depth, n // tn),
        in_specs=[pl.BlockSpec((m, k), lambda l, j: (0, 0)),
                  pl.BlockSpec((None, k, tn), lambda l, j: (l, 0, j))],
        out_specs=pl.BlockSpec((None, m, tn), lambda l, j: (l, 0, j)),
        out_shape=jax.ShapeDtypeStruct((depth, m, n), F32),
        compiler_params=_cparams(("parallel", "parallel")),
        name="mod_matmul",
    )(act, mod_w)


def _normmod_kernel(x_ref, g_ref, sh_ref, sc_ref, o_ref):
    x = x_ref[0]
    ms = jnp.mean(x * x, axis=-1, keepdims=True)
    y = x * lax.rsqrt(ms + NORM_EPS) * g_ref[...]
    o_ref[0] = (y * (1.0 + sc_ref[0]) + sh_ref[0]).astype(o_ref.dtype)


def normmod(xs, g, shift, scale, lc, out_dtype=BF16):
    b, n, d = xs.shape
    tb = lc
    sh = shift.reshape(b * 2, 1, d)
    sc = scale.reshape(b * 2, 1, d)
    seg = lambda bi, i: (bi * 2 + jnp.minimum(i, 1), 0, 0)
    return pl.pallas_call(
        _normmod_kernel,
        grid=(b, n // tb),
        in_specs=[pl.BlockSpec((1, tb, d), lambda bi, i: (bi, i, 0)),
                  pl.BlockSpec((1, d), lambda bi, i: (0, 0)),
                  pl.BlockSpec((1, 1, d), seg),
                  pl.BlockSpec((1, 1, d), seg)],
        out_specs=pl.BlockSpec((1, tb, d), lambda bi, i: (bi, i, 0)),
        out_shape=jax.ShapeDtypeStruct((b, n, d), out_dtype),
        compiler_params=_cparams(("parallel", "parallel")),
        name="normmod",
    )(xs, g.reshape(1, d), sh, sc)


def _out_proj_kernel(ya_ref, yb_ref, yc_ref, yd_ref, w_ref, x_ref, g_ref, o_ref, *, lc, blocks_per_seq):
    acc = jnp.dot(ya_ref[...], w_ref[0:GROUP_W], preferred_element_type=F32)
    acc += jnp.dot(yb_ref[...], w_ref[GROUP_W:2 * GROUP_W], preferred_element_type=F32)
    acc += jnp.dot(yc_ref[...], w_ref[2 * GROUP_W:3 * GROUP_W], preferred_element_type=F32)
    acc += jnp.dot(yd_ref[...], w_ref[3 * GROUP_W:4 * GROUP_W], preferred_element_type=F32)
    tm = acc.shape[0]
    tok = lax.broadcasted_iota(jnp.int32, (tm, 1), 0) + (pl.program_id(0) % blocks_per_seq) * tm
    gate = jnp.where(tok < lc, g_ref[0:1], g_ref[1:2])
    o_ref[...] = x_ref[...] + gate * acc


def out_proj(groups, w, xs, gate, lc):
    b, n, d = xs.shape
    tm = _pick(n, (768, 384, 256, 128))
    tn = 1024
    bps = n // tm
    row = pl.BlockSpec((tm, GROUP_W), lambda i, j: (i, 0))
    kern = functools.partial(_out_proj_kernel, lc=lc, blocks_per_seq=bps)
    out = pl.pallas_call(
        kern,
        grid=(b * bps, d // tn),
        in_specs=[row, row, row, row,
                  pl.BlockSpec((4 * GROUP_W, tn), lambda i, j: (0, j)),
                  pl.BlockSpec((tm, tn), lambda i, j: (i, j)),
                  pl.BlockSpec((None, 2, tn), lambda i, j: (i // bps, 0, j))],
        out_specs=pl.BlockSpec((tm, tn), lambda i, j: (i, j)),
        out_shape=jax.ShapeDtypeStruct((b * n, d), F32),
        compiler_params=_cparams(("parallel", "parallel")),
        name="out_proj",
    )(*groups, w, xs.reshape(b * n, d), gate)
    return out.reshape(b, n, d)


HEAD_TILE = 128


def _softmax_pv(q, k, v):
    s = lax.dot_general(q, k, (((1,), (1,)), ((), ())), preferred_element_type=F32)
    p = jnp.exp(s - jnp.max(s, axis=-1, keepdims=True))
    o = jnp.dot(p.astype(BF16), v, preferred_element_type=F32)
    return o, jnp.sum(p, axis=-1, keepdims=True)


def _per_segment(attend, lc, n):
    i = pl.program_id(2)

    @pl.when(i == 0)
    def _():
        attend(lc)

    @pl.when(i > 0)
    def _():
        attend(n)


ATT_HEADS = 4
ATT_W = ATT_HEADS * HEAD_TILE


def _head(x, j, width=HEAD_TILE, stride=HEAD_TILE):
    return x[:, j * stride:j * stride + width]


def _mla_attn_kernel(qn_ref, qr_ref, kv_ref, kr_ref, o_ref, *, lc):
    def attend(nk):
        kv, kr = kv_ref[0, :nk], kr_ref[0, :nk]
        for j in range(ATT_HEADS):
            q = jnp.concatenate([_head(qn_ref[0], j), _head(qr_ref[0], j)], axis=1)
            k = jnp.concatenate([_head(kv, 2 * j), kr], axis=1)
            o, l = _softmax_pv(q, k, _head(kv, 2 * j + 1))
            o_ref[0, :, j * HEAD_TILE:(j + 1) * HEAD_TILE] = (o / l).astype(o_ref.dtype)

    _per_segment(attend, lc, kv_ref.shape[1])


def mla_attention(q, kv, kr, lc):
    b, n, w = q.shape
    groups = w // (2 * ATT_W)
    tq = lc
    kern = functools.partial(_mla_attn_kernel, lc=lc)
    qspec = lambda off: pl.BlockSpec((1, tq, ATT_W), lambda bi, gi, i: (bi, i, off + gi))
    return pl.pallas_call(
        kern,
        grid=(b, groups, n // tq),
        in_specs=[qspec(0), qspec(groups),
                  pl.BlockSpec((1, n, 2 * ATT_W), lambda bi, gi, i: (bi, 0, gi)),
                  pl.BlockSpec((1, n, HEAD_TILE), lambda bi, gi, i: (bi, 0, 0))],
        out_specs=qspec(0),
        out_shape=jax.ShapeDtypeStruct((b, n, w // 2), BF16),
        compiler_params=_cparams(("parallel", "parallel", "arbitrary")),
        name="mla_attention",
    )(q, q, kv, kr)


def _diff_attn_kernel(q_ref, k_ref, v_ref, lam_ref, g_ref, o_ref, *, lc, post):
    lam = lam_ref[...]
    first = lax.broadcasted_iota(jnp.int32, (q_ref.shape[1], HEAD_TILE), 1) < DIFF_HEAD

    def attend(nk):
        for j in range(ATT_HEADS):
            q = _head(q_ref[0], j)
            k, v = _head(k_ref[0, :nk], j), _head(v_ref[0, :nk], j)
            o0, l0 = _softmax_pv(jnp.where(first, q, jnp.zeros_like(q)), k, v)
            o1, l1 = _softmax_pv(jnp.where(first, jnp.zeros_like(q), q), k, v)
            o = o0 / l0 - o1 * (lam[:, :1] / l1)
            ms = jnp.mean(o * o, axis=-1, keepdims=True)
            o_ref[0, :, j * HEAD_TILE:(j + 1) * HEAD_TILE] = (
                o * lax.rsqrt(ms + NORM_EPS) * g_ref[...] * post).astype(o_ref.dtype)

    _per_segment(attend, lc, v_ref.shape[1])


def diff_attention(q, k, v, lam, sub_norm, lc, post):
    b, n, w = q.shape
    tq = lc
    kern = functools.partial(_diff_attn_kernel, lc=lc, post=post)
    qspec = pl.BlockSpec((1, tq, ATT_W), lambda bi, gi, i: (bi, i, gi))
    kspec = pl.BlockSpec((1, n, ATT_W), lambda bi, gi, i: (bi, 0, gi))
    vec = pl.BlockSpec((1, HEAD_TILE), lambda bi, gi, i: (0, 0))
    return pl.pallas_call(
        kern,
        grid=(b, w // ATT_W, n // tq),
        in_specs=[qspec, kspec, kspec, vec, vec],
        out_specs=qspec,
        out_shape=jax.ShapeDtypeStruct((b, n, w), BF16),
        compiler_params=_cparams(("parallel", "parallel", "arbitrary")),
        name="diff_attention",
    )(q, k, v, jnp.broadcast_to(lam.reshape(1, 1), (1, HEAD_TILE)).astype(F32),
      sub_norm.reshape(1, HEAD_TILE))


S5_TB = 64
S5_GPT = 128 // S5_GROUP_CH
S5_TILES_IN = GROUP_W // 128
S5_TILES_ST = S5_GROUPS * S5_STATE // 128
S5_ST_PER_IN = S5_TILES_ST // S5_TILES_IN


def _s5_kernel(uf_ref, ub_ref, wd_ref, wr_ref, tab_ref, yf_ref, yb_ref, bu_re, bu_im, st_re, st_im):
    nb, tb, _ = uf_ref.shape
    rows = nb * tb
    half = S5_ST_PER_IN * 128

    @pl.when(pl.program_id(0) == 0)
    def _():
        st_re[...] = jnp.zeros_like(st_re)
        st_im[...] = jnp.zeros_like(st_im)

    for d, u_ref in enumerate((uf_ref, ub_ref)):
        x = u_ref[...].reshape(rows, GROUP_W).astype(BF16)
        for jt in range(S5_TILES_IN):
            bu = jnp.dot(x[:, jt * 128:(jt + 1) * 128], wd_ref[d, jt], preferred_element_type=F32)
            for c in range(S5_ST_PER_IN):
                bu_re[jt * S5_ST_PER_IN + c, d * rows:(d + 1) * rows, :] = bu[:, c * 128:(c + 1) * 128]
                bu_im[jt * S5_ST_PER_IN + c, d * rows:(d + 1) * rows, :] = bu[:, half + c * 128:half + (c + 1) * 128]

    def lane_tile(c, carry):
        for d in range(2):
            levels = [(tab_ref[d, 2 * k, c], tab_ref[d, 2 * k + 1, c], (1 << k) if d == 0 else 8 - (1 << k))
                      for k in range(3)]
            cr, ci = tab_ref[d, 6, c], tab_ref[d, 7, c]
            tiles = range(tb // 8) if d == 0 else range(tb // 8 - 1, -1, -1)
            for bi in range(nb):
                row = d * nb + bi
                car_r = st_re[c, row:row + 1, :]
                car_i = st_im[c, row:row + 1, :]
                for tt in tiles:
                    off = row * tb + tt * 8
                    hr = bu_re[c, off:off + 8, :]
                    hi = bu_im[c, off:off + 8, :]
                    for ar, ai, shift in levels:
                        pr = pltpu.roll(hr, shift, axis=0)
                        pi = pltpu.roll(hi, shift, axis=0)
                        hr, hi = hr + (ar * pr - ai * pi), hi + (ar * pi + ai * pr)
                    hr, hi = hr + (cr * car_r - ci * car_i), hi + (cr * car_i + ci * car_r)
                    bu_re[c, off:off + 8, :] = hr
                    bu_im[c, off:off + 8, :] = hi
                    last = 7 if d == 0 else 0
                    car_r, car_i = hr[last:last + 1], hi[last:last + 1]
                st_re[c, row:row + 1, :] = car_r
                st_im[c, row:row + 1, :] = car_i
        return carry

    lax.fori_loop(0, S5_TILES_ST, lane_tile, 0)

    for d, y_ref in enumerate((yf_ref, yb_ref)):
        for jt in range(S5_TILES_IN):
            tiles = [bu_re[jt * S5_ST_PER_IN + c, d * rows:(d + 1) * rows, :] for c in range(S5_ST_PER_IN)]
            tiles += [bu_im[jt * S5_ST_PER_IN + c, d * rows:(d + 1) * rows, :] for c in range(S5_ST_PER_IN)]
            h = jnp.concatenate(tiles, axis=1).astype(BF16)
            y = jnp.dot(h, wr_ref[d, jt], preferred_element_type=F32)
            y_ref[:, :, jt * 128:(jt + 1) * 128] = y.reshape(nb, tb, 128)


def _rev_block(i, nlc, ntot):
    return jnp.where(i < nlc, nlc - 1 - i, ntot - 1 - (i - nlc))


def s5_scan(z, lc, wd, wr, tab):
    b, n, _ = z.shape
    tb = min(S5_TB, lc)
    nlc, ntot = lc // tb, n // tb
    blk = (b, tb, GROUP_W)
    full = lambda a: pl.BlockSpec(a.shape, lambda i: (0,) * a.ndim)
    return pl.pallas_call(
        _s5_kernel,
        grid=(ntot,),
        in_specs=[pl.BlockSpec(blk, lambda i: (0, i, 0)),
                  pl.BlockSpec(blk, lambda i: (0, _rev_block(i, nlc, ntot), 0)),
                  full(wd), full(wr), full(tab)],
        out_specs=[pl.BlockSpec(blk, lambda i: (0, i, 0)),
                   pl.BlockSpec(blk, lambda i: (0, _rev_block(i, nlc, ntot), 0))],
        out_shape=[jax.ShapeDtypeStruct((b, n, GROUP_W), F32)] * 2,
        scratch_shapes=[pltpu.VMEM((S5_TILES_ST, 2 * b * tb, 128), F32)] * 2
        + [pltpu.VMEM((S5_TILES_ST, 8, 128), F32)] * 2,
        compiler_params=_cparams(("arbitrary",)),
        name="s5_scan",
    )(z, z, wd, wr, tab)


def _cmul(ar, ai, br, bi):
    return ar * br - ai * bi, ar * bi + ai * br


def s5_operators(a_re, a_im, log_step, b_re, b_im, c_re, c_im, nb):
    step = jnp.exp(log_step)[..., None]
    mag = jnp.exp(a_re * step)
    lr, li = mag * jnp.cos(a_im * step), mag * jnp.sin(a_im * step)
    den = a_re * a_re + a_im * a_im
    nr, ni = lr - 1.0, li
    fr = (nr * a_re + ni * a_im) / den
    fi = (ni * a_re - nr * a_im) / den
    bbr, bbi = _cmul(fr[..., None], fi[..., None], b_re, b_im)
    eye = jnp.eye(S5_GPT, dtype=F32)

    def drive(w):
        w = w.reshape(2, S5_TILES_IN, S5_GPT, S5_STATE, S5_GROUP_CH)
        w = jnp.einsum('dtgph,gk->dtghkp', w, eye)
        return w.reshape(2, S5_TILES_IN, 128, S5_GPT * S5_STATE)

    def readout(w):
        w = w.reshape(2, S5_TILES_IN, S5_GPT, S5_GROUP_CH, S5_STATE)
        w = jnp.einsum('dtghp,gk->dtgpkh', w, eye)
        return w.reshape(2, S5_TILES_IN, S5_GPT * S5_STATE, 128)

    wd = jnp.concatenate([drive(bbr), drive(bbi)], axis=-1).astype(BF16)
    wr = jnp.concatenate([readout(c_re), readout(-c_im)], axis=-2).astype(BF16)

    log_mag = (a_re * step).reshape(2, S5_TILES_ST, 1, 128)
    ang = (a_im * step).reshape(2, S5_TILES_ST, 1, 128)
    r8 = jnp.arange(8, dtype=F32).reshape(1, 1, 8, 1)

    def power(e, keep):
        e = jnp.broadcast_to(jnp.asarray(e, F32), (2, S5_TILES_ST, 8, 128))
        m = jnp.exp(log_mag * e)
        k = jnp.broadcast_to(keep, e.shape)
        return jnp.where(k, m * jnp.cos(ang * e), 0.0), jnp.where(k, m * jnp.sin(ang * e), 0.0)

    is_fwd = jnp.arange(2).reshape(2, 1, 1, 1) == 0
    tabs = []
    for s in (1, 2, 4):
        tabs += power(float(s), jnp.where(is_fwd, r8 >= s, r8 < 8 - s))
    tabs += power(jnp.where(is_fwd, r8 + 1.0, 8.0 - r8), True)
    return wd, wr, jnp.stack(tabs, axis=1)


def _s5_post_kernel(yf_ref, yb_ref, u_ref, d_ref, w_ref, o_ref):
    y = jax.nn.gelu(yf_ref[...] + yb_ref[...] + d_ref[...] * u_ref[...])
    gv = jnp.dot(y.astype(BF16), w_ref[...], preferred_element_type=F32)
    o_ref[...] = (gv[:, :GROUP_W] * jax.nn.sigmoid(gv[:, GROUP_W:])).astype(o_ref.dtype)


def s5_post(yf, yb, z, d_skip, w_glu):
    t = yf.shape[0]
    tm = _pick(t, (512, 256, 128))
    row = pl.BlockSpec((tm, GROUP_W), lambda i: (i, 0))
    return pl.pallas_call(
        _s5_post_kernel,
        grid=(t // tm,),
        in_specs=[row, row, row, pl.BlockSpec((1, GROUP_W), lambda i: (0, 0)),
                  pl.BlockSpec((GROUP_W, 2 * GROUP_W), lambda i: (0, 0))],
        out_specs=row,
        out_shape=jax.ShapeDtypeStruct((t, GROUP_W), BF16),
        compiler_params=_cparams(("parallel",)),
        name="s5_post",
    )(yf, yb, z, d_skip.reshape(1, GROUP_W), w_glu)


RW_TB = 16
RW_ROWS = 8
RW_TILES = GROUP_W // 128


def _split_bf16(x, parts):
    out = []
    for _ in range(parts - 1):
        hi = x.astype(BF16)
        out.append(hi)
        x = x - hi.astype(F32)
    out.append(x.astype(BF16))
    return out


def _permute_rows(p, x, parts):
    return sum(jnp.dot(p, piece, preferred_element_type=F32) for piece in _split_bf16(x, parts))


def _rwkv_to_lanes(fwd_ref, bwd_ref, p_ref, xt_s, dst_ref, parts):
    nb, tb, _ = fwd_ref.shape
    x = jnp.concatenate([fwd_ref[...].reshape(nb * tb, GROUP_W), bwd_ref[...].reshape(nb * tb, GROUP_W)], axis=0)
    xt_s[...] = _permute_rows(p_ref[...], x, parts)
    low = lax.broadcasted_iota(jnp.int32, (RW_HEAD, 128), 1) < RW_HEAD

    def pair(tp, carry):
        z = xt_s[pl.ds(pl.multiple_of(tp * 2 * RW_ROWS, 2 * RW_ROWS), 2 * RW_ROWS), :]
        zz = jnp.concatenate([z[p * RW_ROWS:(p + 1) * RW_ROWS, jt * 128:(jt + 1) * 128]
                              for p in range(2) for jt in range(RW_TILES)], axis=0)
        tt = zz.T
        top, bot = tt[:RW_HEAD], tt[RW_HEAD:]
        dst_ref[2 * tp] = jnp.where(low, top, pltpu.roll(bot, RW_HEAD, axis=1))
        dst_ref[2 * tp + 1] = jnp.where(low, pltpu.roll(top, RW_HEAD, axis=1), bot)
        return carry

    lax.fori_loop(0, tb // 2, pair, 0, unroll=8)


def _rwkv_from_lanes(y_s, pt_ref, xt_s, of_ref, ob_ref):
    nb, tb, _ = of_ref.shape
    low = lax.broadcasted_iota(jnp.int32, (RW_HEAD, 128), 1) < RW_HEAD

    def pair(tp, carry):
        y0, y1 = y_s[2 * tp], y_s[2 * tp + 1]
        top = jnp.where(low, y0, pltpu.roll(y1, RW_HEAD, axis=1))
        bot = jnp.where(low, pltpu.roll(y0, RW_HEAD, axis=1), y1)
        zz = jnp.concatenate([top, bot], axis=0).T
        base = pl.multiple_of(tp * 2 * RW_ROWS, 2 * RW_ROWS)
        for p in range(2):
            for jt in range(RW_TILES):
                blk = zz[(p * RW_TILES + jt) * RW_ROWS:(p * RW_TILES + jt + 1) * RW_ROWS]
                xt_s[pl.ds(base + p * RW_ROWS, RW_ROWS), jt * 128:(jt + 1) * 128] = blk
        return carry

    lax.fori_loop(0, tb // 2, pair, 0, unroll=8)
    y = _permute_rows(pt_ref[...], xt_s[...], 2)
    of_ref[...] = y[:nb * tb].reshape(nb, tb, GROUP_W)
    ob_ref[...] = y[nb * tb:].reshape(nb, tb, GROUP_W)


def _rwkv_kernel(rf_ref, kf_ref, vf_ref, wf_ref, af_ref, rb_ref, kb_ref, vb_ref, wb_ref, ab_ref,
                 p_ref, pt_ref, c_ref, of_ref, ob_ref,
                 r_s, k_s, v_s, w_s, a_s, y_s, xt_s, kk_s, bb_s, s_ref):
    tb = rf_ref.shape[1]

    @pl.when(pl.program_id(0) == 0)
    def _():
        s_ref[...] = jnp.zeros_like(s_ref)

    _rwkv_to_lanes(rf_ref, rb_ref, p_ref, xt_s, r_s, 2)
    _rwkv_to_lanes(kf_ref, kb_ref, p_ref, xt_s, k_s, 2)
    _rwkv_to_lanes(vf_ref, vb_ref, p_ref, xt_s, v_s, 2)
    _rwkv_to_lanes(wf_ref, wb_ref, p_ref, xt_s, w_s, 3)
    _rwkv_to_lanes(af_ref, ab_ref, p_ref, xt_s, a_s, 2)

    def step(t, carry):
        k = k_s[t]
        kk = k * c_ref[2]
        kk = kk * lax.rsqrt(jnp.maximum(jnp.sum(kk * kk, axis=0, keepdims=True), 1e-12))
        z = -(w_s[t] + c_ref[0])
        w = -(jnp.maximum(z, 0.0) + jnp.log(1.0 + jnp.exp(-jnp.abs(z)))) - 0.5
        a = jax.nn.sigmoid(a_s[t] + c_ref[1])
        kd = k * (1.0 + (a - 1.0) * c_ref[3])
        bonus = jnp.sum(r_s[t] * kd * c_ref[4], axis=0, keepdims=True)
        w_s[t] = jnp.exp(-jnp.exp(w))
        k_s[t] = kd
        kk_s[...] = kk
        bb_s[...] = kk * a

        def vgroup(vg, carry2):
            off = pl.multiple_of(vg * 8, 8)
            vrows = v_s[t, pl.ds(off, 8), :]
            outs = []
            for j in range(8):
                sv = s_ref[off + j]
                sa = jnp.sum(sv * kk_s[...], axis=0, keepdims=True)
                sn = sv * w_s[t] - sa * bb_s[...] + vrows[j:j + 1] * k_s[t]
                s_ref[off + j] = sn
                outs.append(jnp.sum(sn * r_s[t], axis=0, keepdims=True))
            y_s[t, pl.ds(off, 8), :] = jnp.concatenate(outs, axis=0) + bonus * vrows
            return carry2

        return lax.fori_loop(0, RW_HEAD // 8, vgroup, carry)

    lax.fori_loop(0, tb, step, 0)
    _rwkv_from_lanes(y_s, pt_ref, xt_s, of_ref, ob_ref)


def rwkv_scan(zs, lr, consts, lc):
    b, n, _ = zs.shape
    assert 2 * b == RW_ROWS
    tb = RW_TB
    nlc, ntot = lc // tb, n // tb
    rows = RW_ROWS * tb
    r_out = jnp.arange(rows)
    step, db = r_out // RW_ROWS, r_out % RW_ROWS
    src = db * tb + jnp.where(db < b, step, tb - 1 - step)
    perm = (src[:, None] == jnp.arange(rows)[None, :]).astype(BF16)
    blk = (b, tb, GROUP_W)
    fwd = lambda col: pl.BlockSpec(blk, lambda i: (0, i, col))
    bwd = lambda col: pl.BlockSpec(blk, lambda i: (0, _rev_block(i, nlc, ntot), col))
    full = lambda a: pl.BlockSpec(a.shape, lambda i: (0,) * a.ndim)
    tile = pltpu.VMEM((tb, RW_HEAD, 128), F32)
    return pl.pallas_call(
        _rwkv_kernel,
        grid=(ntot,),
        in_specs=[fwd(0), fwd(1), fwd(2), fwd(0), fwd(2), bwd(0), bwd(1), bwd(2), bwd(1), bwd(3),
                  full(perm), full(perm), full(consts)],
        out_specs=[fwd(0), bwd(0)],
        out_shape=[jax.ShapeDtypeStruct((b, n, GROUP_W), F32)] * 2,
        scratch_shapes=[tile] * 6 + [pltpu.VMEM((rows, GROUP_W), F32),
                                     pltpu.VMEM((RW_HEAD, 128), F32), pltpu.VMEM((RW_HEAD, 128), F32),
                                     pltpu.VMEM((RW_HEAD, RW_HEAD, 128), F32)],
        compiler_params=_cparams(("arbitrary",)),
        name="rwkv_scan",
    )(zs, zs, zs, lr, lr, zs, zs, zs, lr, lr, perm, perm.T, consts)


def _moe_row_copy(tok_ref, x_hbm, xbuf, sem, blk, slot, r):
    tok = tok_ref[blk * MOE_BLOCK + r]
    return pltpu.make_async_copy(x_hbm.at[pl.ds(tok, 1)], xbuf.at[slot, pl.ds(r, 1)], sem.at[slot])


def _moe_kernel(be_ref, nu_ref, tok_ref, x_hbm, wg_ref, wu_ref, wd_ref, o_ref, xbuf, sem, g_s, u_s, d_s):
    i = pl.program_id(0)
    used = i < nu_ref[0]
    changed = jnp.logical_or(i == 0, be_ref[i] != be_ref[jnp.maximum(i - 1, 0)])
    slot = i % 2

    def gather(blk, into):
        def start(r, carry):
            _moe_row_copy(tok_ref, x_hbm, xbuf, sem, blk, into, r).start()
            return carry
        lax.fori_loop(0, MOE_BLOCK, start, 0)

    @pl.when(i == 0)
    def _():
        gather(0, 0)

    @pl.when(i + 1 < nu_ref[0])
    def _():
        gather(i + 1, 1 - slot)

    @pl.when(jnp.logical_and(used, changed))
    def _():
        g_s[...] = wg_ref[...].astype(BF16)
        u_s[...] = wu_ref[...].astype(BF16)
        d_s[...] = wd_ref[...].astype(BF16)

    @pl.when(used)
    def _():
        pltpu.make_async_copy(x_hbm.at[pl.ds(0, MOE_BLOCK)], xbuf.at[slot], sem.at[slot]).wait()
        x = xbuf[slot].astype(BF16)
        g = jnp.dot(x, g_s[...], preferred_element_type=F32)
        u = jnp.dot(x, u_s[...], preferred_element_type=F32)
        h = (g * jax.nn.sigmoid(g) * u).astype(BF16)
        o_ref[...] = jnp.dot(h, d_s[...], preferred_element_type=F32)

    @pl.when(jnp.logical_not(used))
    def _():
        o_ref[...] = jnp.zeros_like(o_ref)


def moe_ffn(block_exp, n_used, slot_tok, x, wg, wu, wd, layer):
    cap = slot_tok.shape[0]
    d = x.shape[1]
    hid = wg.shape[-1]
    nb = cap // MOE_BLOCK
    once = pl.Buffered(1)
    wspec = lambda shape: pl.BlockSpec((None, None) + shape, lambda i, be, nu, tok: (layer, be[i], 0, 0),
                                       pipeline_mode=once)
    grid_spec = pltpu.PrefetchScalarGridSpec(
        num_scalar_prefetch=3,
        grid=(nb,),
        in_specs=[pl.BlockSpec(memory_space=pl.ANY), wspec((d, hid)), wspec((d, hid)), wspec((hid, d))],
        out_specs=pl.BlockSpec((MOE_BLOCK, d), lambda i, be, nu, tok: (i, 0)),
        scratch_shapes=[pltpu.VMEM((2, MOE_BLOCK, d), F32), pltpu.SemaphoreType.DMA((2,)),
                        pltpu.VMEM((d, hid), BF16), pltpu.VMEM((d, hid), BF16), pltpu.VMEM((hid, d), BF16)],
    )
    return pl.pallas_call(
        _moe_kernel,
        grid_spec=grid_spec,
        out_shape=jax.ShapeDtypeStruct((cap, d), F32),
        compiler_params=_cparams(("arbitrary",)),
        name="moe_ffn",
    )(block_exp, n_used, slot_tok, x, wg, wu, wd)


def _rms(x, g):
    return x * lax.rsqrt(jnp.mean(x * x, axis=-1, keepdims=True) + NORM_EPS) * g


def _axial_rope(rows, dim):
    n_freq = dim // 4
    inv = ROPE_BASE ** (-jnp.arange(n_freq, dtype=F32) / n_freq)
    row = jnp.repeat(jnp.arange(rows, dtype=F32), GRID_W)
    col = jnp.tile(jnp.arange(GRID_W, dtype=F32), rows)
    ang = jnp.concatenate([row[:, None] * inv, col[:, None] * inv], axis=-1)
    return jnp.cos(ang), jnp.sin(ang)


def _rope(x, cos, sin):
    half = x.shape[-1] // 2
    x1, x2 = x[..., :half], x[..., half:]
    return jnp.concatenate([x1 * cos - x2 * sin, x1 * sin + x2 * cos], axis=-1)


def _rope_lat(x, cos, sin, lc):
    extra = (None,) * (x.ndim - 3)
    cs = cos[(slice(None),) + extra]
    sn = sin[(slice(None),) + extra]
    return jnp.concatenate([x[:, :lc], _rope(x[:, lc:], cs, sn)], axis=1)


def _seq_rev(t, lc):
    return jnp.concatenate([jnp.flip(t[:, :lc], 1), jnp.flip(t[:, lc:], 1)], axis=1)


def _mm3(x, w, **kw):
    b, n, k = x.shape
    return matmul(x.reshape(b * n, k), w, **kw).reshape(b, n, -1)


def _mla_mixer(cq, ckv, krope, lc, q_a_norm, kv_a_norm, w_uq, w_ukv, q_norm, k_norm, cos, sin):
    b, n, _ = cq.shape
    pad = HEAD_TILE - MLA_ROPE
    wq = w_uq.reshape(MLA_Q_RANK, MLA_HEADS, MLA_NOPE + MLA_ROPE)
    wq = jnp.concatenate([wq[:, :, :MLA_NOPE].reshape(MLA_Q_RANK, -1),
                          jnp.pad(wq[:, :, MLA_NOPE:], ((0, 0), (0, 0), (0, pad))).reshape(MLA_Q_RANK, -1)], axis=1)
    q = _mm3(_rms(cq, q_a_norm), wq).reshape(b, n, 2, MLA_HEADS, HEAD_TILE)
    kv = _mm3(_rms(ckv, kv_a_norm), w_ukv).reshape(b, n, MLA_HEADS, MLA_NOPE + MLA_V)
    scale = float(MLA_NOPE + MLA_ROPE) ** -0.5
    q_nope = _rms(q[:, :, 0], q_norm[:MLA_NOPE]) * scale
    q_rope = _rope_lat(_rms(q[:, :, 1, :, :MLA_ROPE], q_norm[MLA_NOPE:]), cos, sin, lc) * scale
    q_rope = jnp.pad(q_rope, ((0, 0), (0, 0), (0, 0), (0, pad)))
    k_nope = _rms(kv[..., :MLA_NOPE], k_norm[:MLA_NOPE])
    k_rope = _rope_lat(_rms(krope, k_norm[MLA_NOPE:]), cos, sin, lc)
    qf = jnp.stack([q_nope, q_rope], axis=2).reshape(b, n, 2 * MLA_HEADS * HEAD_TILE).astype(BF16)
    kvf = jnp.concatenate([k_nope, kv[..., MLA_NOPE:]], axis=-1).reshape(b, n, -1).astype(BF16)
    return mla_attention(qf, kvf, jnp.pad(k_rope, ((0, 0), (0, 0), (0, pad))).astype(BF16), lc)


def _diff_mixer(dq, dk, dv, lc, q_norm, k_norm, lam_vecs, sub_norm, lam_init, cos, sin):
    b, n, _ = dq.shape
    q = _rope_lat(_rms(dq.reshape(b, n, DIFF_HEADS, 2, DIFF_HEAD), q_norm), cos, sin, lc) * float(DIFF_HEAD) ** -0.5
    k = _rope_lat(_rms(dk.reshape(b, n, DIFF_HEADS, 2, DIFF_HEAD), k_norm), cos, sin, lc)
    lam = jnp.exp(jnp.sum(lam_vecs[0] * lam_vecs[1])) - jnp.exp(jnp.sum(lam_vecs[2] * lam_vecs[3])) + lam_init
    flat = lambda x: x.reshape(b, n, GROUP_W).astype(BF16)
    return diff_attention(flat(q), flat(k), dv.astype(BF16), lam, sub_norm, lc, 1.0 - lam_init)


SHIFT_HALO = 8


def _shift_kernel(z_ref, p_ref, n_ref, mu_ref, o_ref, *, blocks_lc, blocks):
    i = pl.program_id(1)
    z = z_ref[0]
    tb = z.shape[0]
    row = lax.broadcasted_iota(jnp.int32, (tb, 1), 0)
    starts = jnp.logical_or(i == 0, i == blocks_lc)
    ends = jnp.logical_or(i == blocks_lc - 1, i == blocks - 1)
    before = jnp.where(starts, 0.0, p_ref[0, SHIFT_HALO - 1:SHIFT_HALO, :])
    after = jnp.where(ends, 0.0, n_ref[0, 0:1, :])
    z_prev = jnp.where(row == 0, before, pltpu.roll(z, 1, axis=0))
    z_next = jnp.where(row == tb - 1, after, pltpu.roll(z, tb - 1, axis=0))
    o_ref[0] = z + mu_ref[0:1] * (z_prev - z) + mu_ref[1:2] * (z_next - z)


def centred_shift(z, mu, lc):
    b, n, _ = z.shape
    c = mu.shape[1]
    tb = lc
    per = tb // SHIFT_HALO
    kern = functools.partial(_shift_kernel, blocks_lc=lc // tb, blocks=n // tb)
    return pl.pallas_call(
        kern,
        grid=(b, n // tb),
        in_specs=[pl.BlockSpec((1, tb, c), lambda bi, i: (bi, i, 0)),
                  pl.BlockSpec((1, SHIFT_HALO, c), lambda bi, i: (bi, jnp.maximum(i * per - 1, 0), 0)),
                  pl.BlockSpec((1, SHIFT_HALO, c), lambda bi, i: (bi, jnp.minimum((i + 1) * per, n // SHIFT_HALO - 1), 0)),
                  pl.BlockSpec((2, c), lambda bi, i: (0, 0))],
        out_specs=pl.BlockSpec((1, tb, c), lambda bi, i: (bi, i, 0)),
        out_shape=jax.ShapeDtypeStruct((b, n, c), F32),
        compiler_params=_cparams(("parallel", "parallel")),
        name="centred_shift",
    )(z, z, z, mu)


def _seg_sum(x, j):
    return sum(jnp.dot(piece, j, preferred_element_type=F32) for piece in _split_bf16(x, 2))


def _rwkv_finish_kernel(of_ref, ob_ref, g_ref, lng_ref, lnb_ref, j_ref, o_ref):
    j = j_ref[...]
    for jt in range(RW_TILES):
        sl = slice(jt * 128, (jt + 1) * 128)
        x = of_ref[:, sl] + ob_ref[:, sl]
        xc = x - _seg_sum(x, j) * (1.0 / RW_HEAD)
        var = _seg_sum(xc * xc, j) * (1.0 / RW_HEAD)
        y = xc * lax.rsqrt(var + RW_GN_EPS)
        o_ref[:, sl] = ((y * lng_ref[:, sl] + lnb_ref[:, sl]) * g_ref[:, sl]).astype(o_ref.dtype)


def rwkv_finish(o_f, o_b, lr, ln_g, ln_b):
    t = o_f.shape[0]
    tm = _pick(t, (512, 256, 128))
    half = jnp.arange(128) // RW_HEAD
    ones = (half[:, None] == half[None, :]).astype(BF16)
    row = pl.BlockSpec((tm, GROUP_W), lambda i: (i, 0))
    vec = pl.BlockSpec((1, GROUP_W), lambda i: (0, 0))
    return pl.pallas_call(
        _rwkv_finish_kernel,
        grid=(t // tm,),
        in_specs=[row, row, pl.BlockSpec((tm, GROUP_W), lambda i: (i, 4)), vec, vec,
                  pl.BlockSpec((128, 128), lambda i: (0, 0))],
        out_specs=row,
        out_shape=jax.ShapeDtypeStruct((t, GROUP_W), BF16),
        compiler_params=_cparams(("parallel",)),
        name="rwkv_finish",
    )(o_f, o_b, lr, ln_g.reshape(1, GROUP_W), ln_b.reshape(1, GROUP_W), ones)


def _rwkv_mixer(z, lc, mu, w0, w2, a0, a2, g2, k_k, k_a, r_k, ln_g, ln_b):
    b, n, _ = z.shape
    z = centred_shift(z, mu, lc)
    o = 3 * GROUP_W
    wl, al, gl = z[..., o:o + 64], z[..., o + 64:o + 128], z[..., o + 128:o + 256]
    lr_in = jnp.concatenate([jnp.tanh(wl), al, jax.nn.sigmoid(gl)], axis=-1)
    zw = jnp.zeros((RW_W_RANK, GROUP_W), F32)
    zg = jnp.zeros((RW_G_RANK, GROUP_W), F32)
    w_lr = jnp.concatenate([
        jnp.concatenate([w2[0], w2[1], zw, zw, zw], axis=1),
        jnp.concatenate([zw, zw, a2[0], a2[1], zw], axis=1),
        jnp.concatenate([zg, zg, zg, zg, g2], axis=1)], axis=0)
    lr = _mm3(lr_in, w_lr)

    def chain_tile(p_fwd, p_bwd):
        per_dir = [p.reshape(RW_TILES, 2, RW_HEAD).transpose(2, 1, 0) for p in (p_fwd, p_bwd)]
        return jnp.stack([per_dir[0]] * b + [per_dir[1]] * b, axis=-1).reshape(RW_HEAD, 128)

    consts = jnp.stack([chain_tile(w0[0], w0[1]), chain_tile(a0[0], a0[1]), chain_tile(k_k, k_k),
                        chain_tile(k_a, k_a), chain_tile(r_k.reshape(-1), r_k.reshape(-1))])
    o_f, o_b = rwkv_scan(z, lr, consts, lc)
    flat = lambda t: t.reshape(b * n, t.shape[-1])
    return rwkv_finish(flat(o_f), flat(o_b), flat(lr), ln_g, ln_b)


def _hier_moe(h, wr, br_coarse, br_fine, wg, wu, wd, layer):
    t, d = h.shape
    logits = matmul(h, wr)
    p_group = jax.nn.softmax(logits[:, :MOE_GROUPS] + br_coarse, axis=-1)
    group = jnp.argmax(p_group, axis=-1).astype(jnp.int32)
    p_sel = jnp.take_along_axis(p_group, group[:, None], axis=-1)
    fine = (logits[:, MOE_GROUPS:MOE_GROUPS + MOE_EXPERTS] + br_fine).reshape(t, MOE_GROUPS, MOE_PER_GROUP)
    fine = jnp.take_along_axis(fine, group[:, None, None], axis=1)[:, 0]
    top_v, top_i = lax.top_k(fine, MOE_TOP_K)
    weight = p_sel * jax.nn.softmax(top_v, axis=-1)
    expert = group[:, None] * MOE_PER_GROUP + top_i.astype(jnp.int32)
    flat_e = expert.reshape(-1)
    n_assign = flat_e.shape[0]
    onehot = (flat_e[:, None] == jnp.arange(MOE_EXPERTS, dtype=jnp.int32)[None, :]).astype(jnp.int32)
    running = jnp.cumsum(onehot, axis=0)
    counts = running[-1]
    padded = (counts + MOE_BLOCK - 1) // MOE_BLOCK * MOE_BLOCK
    pad_end = jnp.cumsum(padded)
    pad_start = pad_end - padded
    dest = jnp.sum(onehot * (running - 1 + pad_start[None, :]), axis=1).astype(jnp.int32)
    n_blocks = -(-(n_assign + MOE_EXPERTS * (MOE_BLOCK - 1)) // MOE_BLOCK)
    cap = n_blocks * MOE_BLOCK
    slot_tok = jnp.zeros((cap,), jnp.int32).at[dest].set(jnp.arange(n_assign, dtype=jnp.int32) // MOE_TOP_K)
    block_start = jnp.arange(n_blocks, dtype=jnp.int32) * MOE_BLOCK
    block_exp = jnp.minimum(jnp.sum((pad_end[None, :] <= block_start[:, None]).astype(jnp.int32), axis=1),
                            MOE_EXPERTS - 1)
    n_used = (pad_end[-1:] // MOE_BLOCK).astype(jnp.int32)
    yb = moe_ffn(block_exp, n_used, slot_tok, h, wg, wu, wd, layer)
    slot_of = dest.reshape(t, MOE_TOP_K)
    return weight, yb[slot_of[:, 0]], yb[slot_of[:, 1]]


def _combine_kernel(x_ref, y0_ref, y1_ref, w_ref, g_ref, o_ref):
    w = w_ref[0]
    o_ref[0] = x_ref[0] + g_ref[0] * (w[:, 0:1] * y0_ref[0] + w[:, 1:2] * y1_ref[0])


def moe_combine(xs, weight, y0, y1, gate, lc, skip):
    b, ntok, d = xs.shape
    r = ntok - skip
    tm, tn = lc, 1024
    off = skip // tm
    blk = lambda idx: pl.BlockSpec((1, tm, tn), idx)
    rows = lambda bi, i, j: (bi, i, j)
    return pl.pallas_call(
        _combine_kernel,
        grid=(b, r // tm, d // tn),
        in_specs=[blk(lambda bi, i, j: (bi, off + i, j)), blk(rows), blk(rows),
                  pl.BlockSpec((1, tm, MOE_TOP_K), lambda bi, i, j: (bi, i, 0)),
                  pl.BlockSpec((1, 1, tn), lambda bi, i, j: (bi * 2 + jnp.minimum(off + i, 1), 0, j))],
        out_specs=blk(rows),
        out_shape=jax.ShapeDtypeStruct((b, r, d), F32),
        compiler_params=_cparams(("parallel", "parallel", "parallel")),
        name="moe_combine",
    )(xs, y0.reshape(b, r, d), y1.reshape(b, r, d), weight.reshape(b, r, MOE_TOP_K), gate.reshape(b * 2, 1, d))


def kernel(x, c, ctx, c_ctx, mod_w, mod_b, norm1_g, norm2_g, w_in, w_out, s5_a_re, s5_a_im, s5_log_step, s5_b_re, s5_b_im, s5_c_re, s5_c_im, s5_d, s5_w_glu, mla_q_a_norm, mla_kv_a_norm, mla_w_uq, mla_w_ukv, mla_q_norm, mla_k_norm, rw_mu, rw_w0, rw_w2, rw_a0, rw_a2, rw_g2, rw_k_k, rw_k_a, rw_r_k, rw_ln_g, rw_ln_b, diff_q_norm, diff_k_norm, diff_lambda, diff_sub_norm, moe_wr_coarse, moe_br_coarse, moe_wr_fine, moe_br_fine, moe_w_gate, moe_w_up, moe_w_down):
    b, n, d = x.shape
    lc = ctx.shape[1]
    ntok = lc + n
    depth = mod_w.shape[0]
    rows = n // GRID_W
    cos_m, sin_m = _axial_rope(rows, MLA_ROPE)
    cos_d, sin_d = _axial_rope(rows, DIFF_HEAD)
    act = jnp.concatenate([jax.nn.silu(c), jax.nn.silu(c_ctx)[None],
                           jnp.zeros((8 - b - 1, d), F32)], axis=0)
    mod_all = mod_matmul(act, mod_w) + mod_b[:, None, :]
    xs = jnp.concatenate([ctx, x], axis=1)

    for l in range(depth):
        last = l == depth - 1
        lam_init = 0.8 - 0.6 * math.exp(-0.3 * l)
        mod_lat = mod_all[l, :b].reshape(b, N_MOD, d)
        mod_ctx = jnp.broadcast_to(mod_all[l, b].reshape(1, N_MOD, d), (b, N_MOD, d))
        mods = jnp.stack([mod_ctx, mod_lat], axis=2)

        h = normmod(xs, norm1_g[l], mods[:, 0], mods[:, 1], lc).reshape(b * ntok, d)
        tm_in = _pick(b * ntok, (1024, 512, 256, 128))
        za = matmul(h, w_in[l][:, :ZA_WIDTH].astype(BF16), tn=ZA_WIDTH).reshape(b, ntok, ZA_WIDTH)
        zb = matmul(h, w_in[l][:, ZA_SPLIT:].astype(BF16), tm=tm_in,
                    tn=_pick(ZB_WIDTH, (1280, 640, 128))).reshape(b, ntok, ZB_WIDTH)

        wd, wr, lam = s5_operators(s5_a_re[l], s5_a_im[l], s5_log_step[l], s5_b_re[l], s5_b_im[l],
                                   s5_c_re[l], s5_c_im[l], b)
        yf, yr = s5_scan(za, lc, wd, wr, lam)
        ya = s5_post(yf.reshape(b * ntok, GROUP_W), yr.reshape(b * ntok, GROUP_W),
                     za.reshape(b * ntok, ZA_WIDTH), s5_d[l], s5_w_glu[l].astype(BF16))
        yb = _mla_mixer(za[..., ZA_CQ:ZA_CKV], za[..., ZA_CKV:ZA_KR], za[..., ZA_KR:ZA_SPLIT], lc,
                        mla_q_a_norm[l], mla_kv_a_norm[l], mla_w_uq[l], mla_w_ukv[l],
                        mla_q_norm[l], mla_k_norm[l], cos_m, sin_m)
        yc = _rwkv_mixer(zb, lc, rw_mu[l], rw_w0[l], rw_w2[l], rw_a0[l], rw_a2[l],
                         rw_g2[l], rw_k_k[l], rw_k_a[l], rw_r_k[l], rw_ln_g[l], rw_ln_b[l])
        yd = _diff_mixer(zb[..., ZB_DQ:ZB_DK], zb[..., ZB_DK:ZB_DV], zb[..., ZB_DV:], lc,
                         diff_q_norm[l], diff_k_norm[l], diff_lambda[l], diff_sub_norm[l], lam_init,
                         cos_d, sin_d)
        flat = lambda y: y.reshape(b * ntok, GROUP_W)
        xs = out_proj((ya, flat(yb), flat(yc), flat(yd)), w_out[l].astype(BF16), xs, mods[:, 2], lc)

        h2 = normmod(xs, norm2_g[l], mods[:, 3], mods[:, 4], lc, out_dtype=F32)
        wrt = jnp.concatenate([moe_wr_coarse[l], moe_wr_fine[l],
                               jnp.zeros((d, 128 - MOE_GROUPS - MOE_EXPERTS), F32)], axis=1)
        experts = (moe_w_gate, moe_w_up, moe_w_down, l)
        skip = lc if last else 0
        routed = _hier_moe(h2[:, skip:].reshape(b * (ntok - skip), d), wrt, moe_br_coarse[l], moe_br_fine[l],
                           *experts)
        xs = moe_combine(xs, *routed, mods[:, 5], lc, skip)
    return xs
```
